```python
import math
import jax
import jax.numpy as jnp
from jax import lax
import numpy as np

D_MODEL = 1024
BATCH = 4
SEQ = 4096
DEPTH = 2

D_MIX = D_MODEL
W_GROUP = D_MIX // 4
M_HEADDIM = 64
M_HEADS = W_GROUP // M_HEADDIM
M_STATE = 64
M_BC_GROUPS = 2
M_CONV = 4
M_CHUNK = 128
M_CONV_CH = W_GROUP + 2 * M_BC_GROUPS * M_STATE
M_IN = W_GROUP + M_CONV_CH + M_HEADS
R_HEADDIM = 64
R_HEADS = W_GROUP // R_HEADDIM
R_LORA_W = 32
R_LORA_A = 32
R_LORA_G = 64
R_GN_EPS = 64e-5
R_IN = 3 * W_GROUP + R_LORA_W + R_LORA_A + R_LORA_G
C_HEADDIM = 64
C_HEADS = W_GROUP // C_HEADDIM
C_Q = C_HEADS * C_HEADDIM
C_KV_RANK = 128
I_HEADS = 8
I_DIM = 32
TOPK_MAX = 256
Q_BLOCK = 128
C_IN = C_Q + C_KV_RANK + I_HEADS * I_DIM + I_DIM + I_HEADS
REL_BUCKETS = 32
REL_MAX_DIST = 128
S_GROUP_CH = 16
S_GROUPS = W_GROUP // S_GROUP_CH
S_STATE = 64
S_IN = W_GROUP
N_IN = M_IN + R_IN + C_IN + S_IN
E_GROUPS = 4
E_PER_GROUP = 8
D_EXPERT = 256
TOPK_INNER = 2
NORM_EPS = 1e-6

kernel_name = 'hybrid_hymba_ssd_rwkv7_dsa_s5_hmoe'


def rmsnorm(x, w):
    xf = x.astype(jnp.float32)
    y = xf * lax.rsqrt(jnp.mean(xf * xf, axis=-1, keepdims=True) + NORM_EPS)
    return (y * w.astype(jnp.float32)).astype(x.dtype)


def causal_dwconv(x, w, b):
    k, ch = w.shape
    y = lax.conv_general_dilated(x, w.astype(x.dtype)[:, None, :], window_strides=(1,),
                                 padding=[(k - 1, 0)], dimension_numbers=('NWC', 'WIO', 'NWC'),
                                 feature_group_count=ch)
    return y + b.astype(x.dtype)


def ssd_chunked(xdt, a, bm, cm):
    b, L, h, p = xdt.shape
    n = bm.shape[-1]
    c = L // M_CHUNK
    xdt = xdt.reshape(b, c, M_CHUNK, h, p)
    bm = bm.reshape(b, c, M_CHUNK, h, n)
    cm = cm.reshape(b, c, M_CHUNK, h, n)
    a_cum = jnp.cumsum(jnp.moveaxis(a.reshape(b, c, M_CHUNK, h), 2, 3), axis=-1)
    pos = jnp.arange(M_CHUNK)
    causal = pos[:, None] >= pos[None, :]
    decay_ls = jnp.exp(jnp.where(causal, a_cum[..., :, None] - a_cum[..., None, :], -jnp.inf))
    scores = jnp.einsum('bclhn,bcshn->bchls', cm, bm) * decay_ls
    y_diag = jnp.einsum('bchls,bcshp->bclhp', scores, xdt)
    decay_to_end = jnp.moveaxis(jnp.exp(a_cum[..., -1:] - a_cum), 2, 3)
    chunk_states = jnp.einsum('bclhn,bclhp->bchpn', bm * decay_to_end[..., None], xdt)
    chunk_decay = jnp.exp(a_cum[..., -1])

    def step(state, inp):
        st, dec = inp
        return state * dec[:, :, None, None] + st, state

    s0 = jnp.zeros((b, h, p, n), jnp.float32)
    _, start_states = lax.scan(step, s0, (jnp.moveaxis(chunk_states, 1, 0), jnp.moveaxis(chunk_decay, 1, 0)))
    start_states = jnp.moveaxis(start_states, 0, 1)
    decay_in = jnp.moveaxis(jnp.exp(a_cum), 2, 3)[..., None]
    y_off = jnp.einsum('bclhn,bchpn->bclhp', cm, start_states) * decay_in
    return (y_diag + y_off).reshape(b, L, h, p)


def mamba2_group(u, conv_w, conv_b, dt_bias, a_log, d_skip, norm_w):
    f32 = jnp.float32
    bsz, L, _ = u.shape
    z, xbc, dt = jnp.split(u, [W_GROUP, W_GROUP + M_CONV_CH], axis=-1)
    xbc = jax.nn.silu(causal_dwconv(xbc, conv_w, conv_b)).astype(f32)
    xs, bm, cm = jnp.split(xbc, [W_GROUP, W_GROUP + M_BC_GROUPS * M_STATE], axis=-1)
    rep = M_HEADS // M_BC_GROUPS
    xs = xs.reshape(bsz, L, M_HEADS, M_HEADDIM)
    bm = jnp.repeat(bm.reshape(bsz, L, M_BC_GROUPS, M_STATE), rep, axis=2)
    cm = jnp.repeat(cm.reshape(bsz, L, M_BC_GROUPS, M_STATE), rep, axis=2)
    dt = jax.nn.softplus(dt.astype(f32) + dt_bias.astype(f32))
    a_head = -jnp.exp(a_log.astype(f32))
    y = ssd_chunked(xs * dt[..., None], dt * a_head, bm, cm)
    y = (y + xs * d_skip.astype(f32)[:, None]).reshape(bsz, L, W_GROUP)
    return rmsnorm(y * jax.nn.silu(z.astype(f32)), norm_w)


def rwkv7_group(u, mu, w0, w2, a0, a2, g2, k_k, k_a, r_k, ln_w, ln_b):
    f32 = jnp.float32
    bsz, L, _ = u.shape
    u = u.astype(f32)
    u_prev = jnp.pad(u, ((0, 0), (1, 0), (0, 0)))[:, :-1]
    u = u + mu.astype(f32) * (u_prev - u)
    s1 = W_GROUP
    r, k, v, dw, da, dg = jnp.split(u, [s1, 2 * s1, 3 * s1, 3 * s1 + R_LORA_W, 3 * s1 + R_LORA_W + R_LORA_A], axis=-1)
    w_log = -jax.nn.softplus(-(w0.astype(f32) + jnp.tanh(dw) @ w2.astype(f32))) - 0.5
    decay = jnp.exp(-jnp.exp(w_log))
    a = jax.nn.sigmoid(a0.astype(f32) + da @ a2.astype(f32))
    g = jax.nn.sigmoid(dg) @ g2.astype(f32)

    def heads(t):
        return t.reshape(bsz, L, R_HEADS, R_HEADDIM)

    kk = heads(k * k_k.astype(f32))
    kk = kk * lax.rsqrt(jnp.maximum(jnp.sum(kk * kk, axis=-1, keepdims=True), 1e-24))
    k = k * (1.0 + (a - 1.0) * k_a.astype(f32))
    r, decay, k, v, a = heads(r), heads(decay), heads(k), heads(v), heads(a)

    def step(state, inp):
        r_t, w_t, k_t, v_t, a_t, b_t = inp
        sa = jnp.einsum('bhvk,bhk->bhv', state, a_t)
        state = state * w_t[:, :, None, :] + sa[..., None] * b_t[:, :, None, :] + v_t[..., None] * k_t[:, :, None, :]
        return state, jnp.einsum('bhvk,bhk->bhv', state, r_t)

    def tm(t):
        return jnp.swapaxes(t, 0, 1)

    s0 = jnp.zeros((bsz, R_HEADS, R_HEADDIM, R_HEADDIM), f32)
    _, o = lax.scan(step, s0, (tm(r), tm(decay), tm(k), tm(v), tm(-kk), tm(kk * a)))
    o = tm(o)
    mean = jnp.mean(o, axis=-1, keepdims=True)
    var = jnp.mean(jnp.square(o - mean), axis=-1, keepdims=True)
    o = ((o - mean) * lax.rsqrt(var + R_GN_EPS)).reshape(bsz, L, W_GROUP) * ln_w.astype(f32) + ln_b.astype(f32)
    o = o + (jnp.sum(r * k * r_k.astype(f32), axis=-1, keepdims=True) * v).reshape(bsz, L, W_GROUP)
    return o * g


def t5_bucket(dist):
    n = jnp.maximum(dist, 0)
    max_exact = REL_BUCKETS // 2
    log_ratio = jnp.log(jnp.maximum(n, 1).astype(jnp.float32) / max_exact) / math.log(REL_MAX_DIST / max_exact)
    large = jnp.minimum(max_exact + (log_ratio * (REL_BUCKETS - max_exact)).astype(jnp.int32), REL_BUCKETS - 1)
    return jnp.where(n < max_exact, n, large)


def dsa_group(u, kv_norm_w, w_uk, w_uv, rel_bias):
    f32 = jnp.float32
    bsz, L, _ = u.shape
    q, c_kv, q_idx, k_idx, w_idx = jnp.split(
        u, [C_Q, C_Q + C_KV_RANK, C_Q + C_KV_RANK + I_HEADS * I_DIM, C_Q + C_KV_RANK + I_HEADS * I_DIM + I_DIM], axis=-1)
    c_kv = rmsnorm(c_kv, kv_norm_w)
    k = (c_kv @ w_uk).astype(f32)
    v = (c_kv @ w_uv).astype(f32)
    k_idx = k_idx.astype(f32)
    n_blk = L // Q_BLOCK
    top_k = min(TOPK_MAX, L // 4)

    def to_blocks(t):
        return jnp.moveaxis(t.reshape((bsz, n_blk, Q_BLOCK) + t.shape[2:]), 1, 0)

    qb_all = to_blocks(q.astype(f32).reshape(bsz, L, C_HEADS, C_HEADDIM))
    qi_all = to_blocks(q_idx.astype(f32).reshape(bsz, L, I_HEADS, I_DIM))
    wi_all = to_blocks(w_idx.astype(f32) * I_HEADS ** -0.5)
    starts = jnp.arange(n_blk, dtype=jnp.int32) * Q_BLOCK
    key_pos = jnp.arange(L, dtype=jnp.int32)
    bias_table = rel_bias.astype(f32)
    gather = jax.vmap(lambda table, idx: table[idx])

    def block(args):
        qb, qib, wib, start = args
        qpos = start + jnp.arange(Q_BLOCK, dtype=jnp.int32)
        idx_logits = jnp.einsum('bqhd,bsd->bqhs', qib, k_idx) * I_DIM ** -0.5
        score = jnp.einsum('bqhs,bqh->bqs', jax.nn.relu(idx_logits), wib)
        score = jnp.where(key_pos[None, None, :] <= qpos[None, :, None], score, -jnp.inf)
        _, sel = lax.top_k(score, top_k)
        k_sel = gather(k, sel)
        v_sel = gather(v, sel)
        dist = qpos[None, :, None] - sel
        bias = jnp.moveaxis(bias_table[t5_bucket(dist)], -1, 2)
        logits = jnp.einsum('bqhd,bqkd->bqhk', qb, k_sel) * C_HEADDIM ** -0.5 + bias
        logits = jnp.where((dist >= 0)[:, :, None, :], logits, -jnp.inf)
        p = jax.nn.softmax(logits, axis=-1)
        return jnp.einsum('bqhk,bqkd->bqhd', p, v_sel)

    out = lax.map(block, (qb_all, qi_all, wi_all, starts))
    return jnp.moveaxis(out, 0, 1).reshape(bsz, L, W_GROUP)


def _cplx_affine_combine(e1, e2):
    a1r, a1i, b1r, b1i = e1
    a2r, a2i, b2r, b2i = e2
    return (a1r * a2r - a1i * a2i, a1r * a2i + a1i * a2r,
            a2r * b1r - a2i * b1i + b2r, a2r * b1i + a2i * b1r + b2i)


def s5_group(u, a_re, a_im, b_re, b_im, c_re, c_im, d_skip, log_dt, glu_w, glu_b):
    f32 = jnp.float32
    bsz, L, _ = u.shape
    uf = u.astype(f32).reshape(bsz, L, S_GROUPS, S_GROUP_CH)
    lam_re = jnp.minimum(a_re.astype(f32), -1e-4)
    lam_im = a_im.astype(f32)
    dt = jnp.exp(log_dt.astype(f32))[:, None]
    mag = jnp.exp(dt * lam_re)
    ab_re = mag * jnp.cos(dt * lam_im)
    ab_im = mag * jnp.sin(dt * lam_im)
    den = lam_re * lam_re + lam_im * lam_im
    f_re = ((ab_re - 1.0) * lam_re + ab_im * lam_im) / den
    f_im = (ab_im * lam_re - (ab_re - 1.0) * lam_im) / den
    b_re = b_re.astype(f32)
    b_im = b_im.astype(f32)
    bb_re = f_re[..., None] * b_re - f_im[..., None] * b_im
    bb_im = f_re[..., None] * b_im + f_im[..., None] * b_re
    bu_re = jnp.einsum('gnp,blgp->blgn', bb_re, uf)
    bu_im = jnp.einsum('gnp,blgp->blgn', bb_im, uf)
    a_re_t = jnp.broadcast_to(ab_re, bu_re.shape)
    a_im_t = jnp.broadcast_to(ab_im, bu_im.shape)
    _, _, h_re, h_im = lax.associative_scan(_cplx_affine_combine, (a_re_t, a_im_t, bu_re, bu_im), axis=1)
    y = jnp.einsum('gpn,blgn->blgp', c_re.astype(f32), h_re) - jnp.einsum('gpn,blgn->blgp', c_im.astype(f32), h_im)
    y = (y + d_skip.astype(f32).reshape(S_GROUPS, S_GROUP_CH) * uf).reshape(bsz, L, W_GROUP)
    y = jax.nn.gelu(y)
    gl = y @ glu_w.astype(f32) + glu_b.astype(f32)
    val, gate = jnp.split(gl, 2, axis=-1)
    return val * jax.nn.sigmoid(gate)


def hier_moe(h, wr_g, br_g, wr_e, br_e, w_gate, w_up, w_down):
    f32 = jnp.float32
    bsz, L, d = h.shape
    t = h.reshape(bsz * L, d)
    n_tok = bsz * L
    g_logits = (t @ wr_g).astype(f32) + br_g.astype(f32)
    g_prob = jax.nn.softmax(g_logits, axis=-1)
    g_sel = jnp.argmax(g_logits, axis=-1)
    g_gate = jnp.take_along_axis(g_prob, g_sel[:, None], axis=-1)
    e_logits = ((t @ wr_e).astype(f32) + br_e.astype(f32)).reshape(n_tok, E_GROUPS, E_PER_GROUP)
    e_logits = e_logits[jnp.arange(n_tok), g_sel]
    top_v, top_i = lax.top_k(e_logits, TOPK_INNER)
    top_w = jax.nn.softmax(top_v, axis=-1) * g_gate
    comb = jnp.einsum('tke,tk->te', jax.nn.one_hot(top_i, E_PER_GROUP, dtype=f32), top_w)
    out = jnp.zeros((n_tok, d), f32)
    for g in range(E_GROUPS):
        cw = comb * (g_sel == g).astype(f32)[:, None]
        hid = jax.nn.silu(jnp.einsum('td,edf->tef', t, w_gate[g])) * jnp.einsum('td,edf->tef', t, w_up[g])
        out = out + jnp.einsum('tef,efd->td', hid * cw[:, :, None], w_down[g])
    return out.reshape(bsz, L, d).astype(h.dtype)


def setup_inputs(seed: int = 0) -> dict:
    key = jax.random.key(seed)
    keys = iter(jax.random.split(key, 64))
    f32 = jnp.float32

    def nrm(shape, scale):
        return jax.random.normal(next(keys), shape, f32) * scale

    def uni(shape, lo, hi):
        return jax.random.uniform(next(keys), shape, f32, lo, hi)

    nl = DEPTH
    x = nrm((BATCH, SEQ, D_MODEL), 1.0)
    norm1_w = 1.0 + nrm((nl, D_MODEL), 0.02)
    w_in = nrm((nl, D_MODEL, N_IN), D_MODEL ** -0.5)
    m_conv_w = nrm((nl, M_CONV, M_CONV_CH), M_CONV ** -0.5)
    m_conv_b = nrm((nl, M_CONV_CH), 0.02)
    dt0 = jnp.exp(uni((nl, M_HEADS), math.log(1e-3), math.log(1e-1)))
    m_dt_bias = dt0 + jnp.log(-jnp.expm1(-dt0))
    m_a_log = jnp.log(uni((nl, M_HEADS), 1.0, 16.0))
    m_d = 1.0 + nrm((nl, M_HEADS), 0.02)
    m_norm_w = 1.0 + nrm((nl, W_GROUP), 0.02)
    r_mu = uni((nl, R_IN), 0.0, 1.0)
    ratio = (jnp.arange(W_GROUP, dtype=f32) / (W_GROUP - 1)) ** 0.9
    r_w0 = -6.0 + 5.0 * ratio + 0.5 + nrm((nl, W_GROUP), 0.05)
    r_w2 = nrm((nl, R_LORA_W, W_GROUP), 0.1)
    r_a0 = nrm((nl, W_GROUP), 0.1)
    r_a2 = nrm((nl, R_LORA_A, W_GROUP), R_LORA_A ** -0.5)
    r_g2 = nrm((nl, R_LORA_G, W_GROUP), R_LORA_G ** -0.5)
    r_k_k = 0.85 + nrm((nl, W_GROUP), 0.02)
    r_k_a = 1.0 + nrm((nl, W_GROUP), 0.02)
    r_r_k = nrm((nl, R_HEADS, R_HEADDIM), 0.1)
    r_ln_w = 1.0 + nrm((nl, W_GROUP), 0.02)
    r_ln_b = nrm((nl, W_GROUP), 0.02)
    c_kv_norm_w = 1.0 + nrm((nl, C_KV_RANK), 0.02)
    c_w_uk = nrm((nl, C_KV_RANK, C_HEADDIM), C_KV_RANK ** -0.5)
    c_w_uv = nrm((nl, C_KV_RANK, C_HEADDIM), C_KV_RANK ** -0.5)
    rel_bias = nrm((REL_BUCKETS, C_HEADS), 0.5)
    s_a_re = -0.5 + nrm((nl, S_GROUPS, S_STATE), 0.01)
    s_a_im = jnp.pi * jnp.arange(S_STATE, dtype=f32) + nrm((nl, S_GROUPS, S_STATE), 0.01)
    s_b_re = nrm((nl, S_GROUPS, S_STATE, S_GROUP_CH), (2 * S_GROUP_CH) ** -0.5)
    s_b_im = nrm((nl, S_GROUPS, S_STATE, S_GROUP_CH), (2 * S_GROUP_CH) ** -0.5)
    s_c_re = nrm((nl, S_GROUPS, S_GROUP_CH, S_STATE), S_STATE ** -0.5)
    s_c_im = nrm((nl, S_GROUPS, S_GROUP_CH, S_STATE), S_STATE ** -0.5)
    s_d = nrm((nl, W_GROUP), 1.0)
    s_log_dt = uni((nl, S_GROUPS), math.log(1e-3), math.log(1e-1))
    s_glu_w = nrm((nl, W_GROUP, 2 * W_GROUP), W_GROUP ** -0.5)
    s_glu_b = nrm((nl, 2 * W_GROUP), 0.02)
    w_out = nrm((nl, D_MIX, D_MODEL), D_MIX ** -0.5)
    norm2_w = 1.0 + nrm((nl, D_MODEL), 0.02)
    moe_wr_group = nrm((nl, D_MODEL, E_GROUPS), D_MODEL ** -0.5)
    moe_br_group = nrm((nl, E_GROUPS), 0.01)
    moe_wr_exp = nrm((nl, D_MODEL, E_GROUPS * E_PER_GROUP), D_MODEL ** -0.5)
    moe_br_exp = nrm((nl, E_GROUPS * E_PER_GROUP), 0.01)
    moe_w_gate = nrm((nl, E_GROUPS, E_PER_GROUP, D_MODEL, D_EXPERT), D_MODEL ** -0.5)
    moe_w_up = nrm((nl, E_GROUPS, E_PER_GROUP, D_MODEL, D_EXPERT), D_MODEL ** -0.5)
    moe_w_down = nrm((nl, E_GROUPS, E_PER_GROUP, D_EXPERT, D_MODEL), D_EXPERT ** -0.5)
    final_norm_w = 1.0 + nrm((D_MODEL,), 0.02)
    return {'x': x, 'norm1_w': norm1_w, 'w_in': w_in,
            'm_conv_w': m_conv_w, 'm_conv_b': m_conv_b, 'm_dt_bias': m_dt_bias, 'm_a_log': m_a_log,
            'm_d': m_d, 'm_norm_w': m_norm_w,
            'r_mu': r_mu, 'r_w0': r_w0, 'r_w2': r_w2, 'r_a0': r_a0, 'r_a2': r_a2, 'r_g2': r_g2,
            'r_k_k': r_k_k, 'r_k_a': r_k_a, 'r_r_k': r_r_k, 'r_ln_w': r_ln_w, 'r_ln_b': r_ln_b,
            'c_kv_norm_w': c_kv_norm_w, 'c_w_uk': c_w_uk, 'c_w_uv': c_w_uv, 'rel_bias': rel_bias,
            's_a_re': s_a_re, 's_a_im': s_a_im, 's_b_re': s_b_re, 's_b_im': s_b_im,
            's_c_re': s_c_re, 's_c_im': s_c_im, 's_d': s_d, 's_log_dt': s_log_dt,
            's_glu_w': s_glu_w, 's_glu_b': s_glu_b,
            'w_out': w_out, 'norm2_w': norm2_w,
            'moe_wr_group': moe_wr_group, 'moe_br_group': moe_br_group,
            'moe_wr_exp': moe_wr_exp, 'moe_br_exp': moe_br_exp,
            'moe_w_gate': moe_w_gate, 'moe_w_up': moe_w_up, 'moe_w_down': moe_w_down,
            'final_norm_w': final_norm_w}


def reference(x, norm1_w, w_in,
              m_conv_w, m_conv_b, m_dt_bias, m_a_log, m_d, m_norm_w,
              r_mu, r_w0, r_w2, r_a0, r_a2, r_g2, r_k_k, r_k_a, r_r_k, r_ln_w, r_ln_b,
              c_kv_norm_w, c_w_uk, c_w_uv, rel_bias,
              s_a_re, s_a_im, s_b_re, s_b_im, s_c_re, s_c_im, s_d, s_log_dt, s_glu_w, s_glu_b,
              w_out, norm2_w,
              moe_wr_group, moe_br_group, moe_wr_exp, moe_br_exp, moe_w_gate, moe_w_up, moe_w_down,
              final_norm_w):
    split_pts = [M_IN, M_IN + R_IN, M_IN + R_IN + C_IN]
    for i in range(DEPTH):
        h = rmsnorm(x, norm1_w[i])
        u = h @ w_in[i]
        u_m, u_r, u_c, u_s = jnp.split(u, split_pts, axis=-1)
        y_m = mamba2_group(u_m, m_conv_w[i], m_conv_b[i], m_dt_bias[i], m_a_log[i], m_d[i], m_norm_w[i])
        y_r = rwkv7_group(u_r, r_mu[i], r_w0[i], r_w2[i], r_a0[i], r_a2[i], r_g2[i],
                          r_k_k[i], r_k_a[i], r_r_k[i], r_ln_w[i], r_ln_b[i])
        y_c = dsa_group(u_c, c_kv_norm_w[i], c_w_uk[i], c_w_uv[i], rel_bias)
        y_s = s5_group(u_s, s_a_re[i], s_a_im[i], s_b_re[i], s_b_im[i], s_c_re[i], s_c_im[i],
                       s_d[i], s_log_dt[i], s_glu_w[i], s_glu_b[i])
        y = jnp.concatenate([y_m.astype(x.dtype), y_r.astype(x.dtype), y_c.astype(x.dtype), y_s.astype(x.dtype)], axis=-1)
        x = x + (y @ w_out[i]).astype(x.dtype)
        x = x + hier_moe(rmsnorm(x, norm2_w[i]), moe_wr_group[i], moe_br_group[i], moe_wr_exp[i], moe_br_exp[i],
                         moe_w_gate[i], moe_w_up[i], moe_w_down[i])
    return rmsnorm(x, final_norm_w)
```

```python
import functools
import math

import jax
import jax.numpy as jnp
from jax import lax
from jax.experimental import pallas as pl
from jax.experimental.pallas import tpu as pltpu

F32 = jnp.float32
BF16 = jnp.bfloat16
I32 = jnp.int32
HIGHEST = lax.Precision.HIGHEST

LANES = 128
VMEM_LIMIT = 56 * 1024 * 1024

NORM_EPS = 1e-6
W_GROUP = 256
HEAD = 64
N_HEADS = W_GROUP // HEAD
M_CONV_CH = 512
M_CONV = 4
M_COLS = 896
R_COLS = 896
C_COLS = 768
S_COLS = 256
SSD_CHUNK = 128
R_CHUNK = 64
R_BLOCK = 128
R_GN_EPS = 64e-5
I_HEADS = 8
I_DIM = 32
Q_BLOCK = 128
TOPK_MAX = 256
REL_BUCKETS = 32
REL_MAX_DIST = 128
S_GROUPS = 16
S_GROUP_CH = 16
S_STATE = 64
S_LANES = S_GROUPS * S_STATE
S_BLOCK = 256
E_GROUPS = 4
E_PER_GROUP = 8
N_EXPERTS = E_GROUPS * E_PER_GROUP
D_EXPERT = 256
INT_MIN = -2 ** 31
NEG = -1e30


def _mm(a, b, precise=False):
    if precise:
        return jnp.dot(a.astype(F32), b.astype(F32), precision=HIGHEST, preferred_element_type=F32)
    return jnp.dot(a.astype(BF16), b.astype(BF16), preferred_element_type=F32)


def _mm_nt(a, b, precise=False):
    dn = (((1,), (1,)), ((), ()))
    if precise:
        return lax.dot_general(a.astype(F32), b.astype(F32), dn, precision=HIGHEST, preferred_element_type=F32)
    return lax.dot_general(a.astype(BF16), b.astype(BF16), dn, preferred_element_type=F32)


def _mm_tn(a, b, precise=False):
    dn = (((0,), (0,)), ((), ()))
    if precise:
        return lax.dot_general(a.astype(F32), b.astype(F32), dn, precision=HIGHEST, preferred_element_type=F32)
    return lax.dot_general(a.astype(BF16), b.astype(BF16), dn, preferred_element_type=F32)


def _silu(x):
    return x * jax.nn.sigmoid(x)


def _softplus(x):
    return jnp.maximum(x, 0.0) + jnp.log(1.0 + jnp.exp(-jnp.abs(x)))


def _params(*sem):
    return pltpu.CompilerParams(dimension_semantics=sem, vmem_limit_bytes=VMEM_LIMIT)


def _full(shape):
    nd = len(shape)
    return pl.BlockSpec(shape, lambda *_: (0,) * nd)


def _in_proj_kernel(x_ref, nw_ref, w_ref, om_ref, or_ref, oc_ref, os_ref):
    x = x_ref[...]
    h = x * lax.rsqrt(jnp.mean(x * x, axis=-1, keepdims=True) + NORM_EPS) * nw_ref[...]
    h = h.astype(BF16)
    off = 0
    for o_ref, width in ((om_ref, M_COLS), (or_ref, R_COLS), (oc_ref, C_COLS), (os_ref, S_COLS)):
        o_ref[...] = jnp.dot(h, w_ref[:, off:off + width], preferred_element_type=F32)
        off += width


def _in_proj(x2, norm_w, w_pad, tm=512):
    t, d = x2.shape
    n = w_pad.shape[1]
    return pl.pallas_call(
        _in_proj_kernel,
        grid=(t // tm,),
        in_specs=[pl.BlockSpec((tm, d), lambda i: (i, 0)), _full((1, d)), _full((d, n))],
        out_specs=[pl.BlockSpec((tm, c), lambda i: (i, 0)) for c in (M_COLS, R_COLS, C_COLS, S_COLS)],
        out_shape=[jax.ShapeDtypeStruct((t, c), F32) for c in (M_COLS, R_COLS, C_COLS, S_COLS)],
        compiler_params=_params("parallel"),
        name="in_proj",
    )(x2, norm_w, w_pad)


def _mamba_kernel(u_ref, cw_ref, cb_ref, dtb_ref, alog_ref, dsk_ref, nw_ref, o_ref, xbuf, state):
    c = pl.program_id(1)
    n = SSD_CHUNK

    @pl.when(c == 0)
    def _():
        xbuf[0:8, :] = jnp.zeros((8, M_CONV_CH), F32)
        state[...] = jnp.zeros_like(state)

    u = u_ref[0]
    z = u[:, 0:W_GROUP]
    dtr = u[:, W_GROUP + M_CONV_CH:M_COLS]
    xbuf[8:8 + n, :] = u[:, W_GROUP:W_GROUP + M_CONV_CH]
    acc = jnp.broadcast_to(cb_ref[...], (n, M_CONV_CH))
    for j in range(M_CONV):
        acc = acc + cw_ref[j:j + 1, :] * xbuf[pl.ds(8 - (M_CONV - 1) + j, n), :]
    xbuf[0:8, :] = xbuf[n:n + 8, :]
    xc = _silu(acc)
    xs = xc[:, 0:W_GROUP]
    bm = xc[:, W_GROUP:W_GROUP + 2 * HEAD]
    cm = xc[:, W_GROUP + 2 * HEAD:]

    dt = _softplus(dtr + dtb_ref[...])
    a = dt * (-jnp.exp(alog_ref[...]))
    row = lax.broadcasted_iota(I32, (n, n), 0)
    col = lax.broadcasted_iota(I32, (n, n), 1)
    causal = row >= col
    a_cum = _mm(causal.astype(F32), a, precise=True)
    a_cum_t = a_cum.T
    a_last = a_cum[n - 1:n, :]

    ys = []
    for h in range(N_HEADS):
        g = h // 2
        b_h = bm[:, g * HEAD:(g + 1) * HEAD]
        c_h = cm[:, g * HEAD:(g + 1) * HEAD]
        x_h = xs[:, h * HEAD:(h + 1) * HEAD]
        xdt = x_h * dt[:, h:h + 1]
        ac = a_cum[:, h:h + 1]
        dec = jnp.exp(jnp.where(causal, ac - a_cum_t[h:h + 1, :], -jnp.inf))
        scores = _mm_nt(c_h, b_h) * dec
        y = _mm(scores, xdt)
        st = state[h]
        y = y + _mm_nt(c_h, st) * jnp.exp(ac)
        al = a_last[:, h:h + 1]
        state[h] = st * jnp.exp(al) + _mm_tn(xdt, b_h * jnp.exp(al - ac))
        ys.append(y + x_h * dsk_ref[:, h * HEAD:(h + 1) * HEAD])
    y = jnp.concatenate(ys, axis=1) * _silu(z)
    y = y * lax.rsqrt(jnp.mean(y * y, axis=-1, keepdims=True) + NORM_EPS) * nw_ref[...]
    o_ref[0] = y.astype(o_ref.dtype)


def _mamba(u_m, conv_w, conv_b, dt_bias, a_log, d_skip, norm_w):
    b, l, _ = u_m.shape
    pad = lambda v: jnp.pad(v.astype(F32), (0, LANES - v.shape[0]))[None, :]
    return pl.pallas_call(
        _mamba_kernel,
        grid=(b, l // SSD_CHUNK),
        in_specs=[pl.BlockSpec((1, SSD_CHUNK, M_COLS), lambda i, j: (i, j, 0)),
                  _full((M_CONV, M_CONV_CH)), _full((1, M_CONV_CH)), _full((1, LANES)), _full((1, LANES)),
                  _full((1, W_GROUP)), _full((1, W_GROUP))],
        out_specs=pl.BlockSpec((1, SSD_CHUNK, W_GROUP), lambda i, j: (i, j, 0)),
        out_shape=jax.ShapeDtypeStruct((b, l, W_GROUP), BF16),
        scratch_shapes=[pltpu.VMEM((SSD_CHUNK + 8, M_CONV_CH), F32), pltpu.VMEM((N_HEADS, HEAD, HEAD), F32)],
        compiler_params=_params("parallel", "arbitrary"),
        name="mamba_ssd",
    )(u_m, conv_w.astype(F32), conv_b.astype(F32)[None, :], pad(dt_bias), pad(a_log),
      jnp.repeat(d_skip.astype(F32), HEAD)[None, :], norm_w.astype(F32)[None, :])


def _rwkv_kernel(u_ref, mu_ref, wl_ref, w0_ref, a0_ref, kk_ref, ka_ref, rk_ref, lnw_ref, lnb_ref, seg_ref,
                 o_ref, ubuf, state):
    c = pl.program_id(1)
    n = R_BLOCK
    cs = R_CHUNK

    @pl.when(c == 0)
    def _():
        ubuf[0:8, :] = jnp.zeros((8, R_COLS), F32)
        state[...] = jnp.zeros_like(state)

    u = u_ref[0]
    ubuf[8:8 + n, :] = u
    u = u + mu_ref[...] * (ubuf[pl.ds(7, n), :] - u)
    ubuf[0:8, :] = ubuf[n:n + 8, :]
    r = u[:, 0:W_GROUP]
    k = u[:, W_GROUP:2 * W_GROUP]
    v = u[:, 2 * W_GROUP:3 * W_GROUP]
    lo = u[:, 3 * W_GROUP:]
    lane = lax.broadcasted_iota(I32, lo.shape, 1)
    lo = jnp.where(lane < 32, jnp.tanh(lo), jnp.where(lane < 64, lo, jax.nn.sigmoid(lo)))
    proj = _mm(lo, wl_ref[...], precise=True)
    w_log = -_softplus(-(w0_ref[...] + proj[:, 0:W_GROUP])) - 0.5
    ld = -jnp.exp(w_log)
    a_lr = jax.nn.sigmoid(a0_ref[...] + proj[:, W_GROUP:2 * W_GROUP])
    g = proj[:, 2 * W_GROUP:]
    seg = seg_ref[...]
    kk = k * kk_ref[...]
    kk = kk * lax.rsqrt(jnp.maximum(_mm(kk * kk, seg, precise=True), 1e-24))
    k = k * (1.0 + (a_lr - 1.0) * ka_ref[...])
    a_v = -kk
    b_v = kk * a_lr

    row = lax.broadcasted_iota(I32, (n, n), 0)
    col = lax.broadcasted_iota(I32, (n, n), 1)
    tri2 = ((row >= col) & ((row // cs) == (col // cs))).astype(F32)
    lw = _mm(tri2, ld, precise=True)
    w_inc = jnp.exp(lw)
    w_inv = jnp.exp(-lw)
    rt = r * w_inc
    at = a_v * jnp.exp(lw - ld)
    bt = b_v * w_inv
    kt = k * w_inv

    r2 = lax.broadcasted_iota(I32, (cs, cs), 0)
    c2 = lax.broadcasted_iota(I32, (cs, cs), 1)
    lower = r2 >= c2
    strict = r2 > c2
    eye = (r2 == c2).astype(F32)

    outs = [[None] * N_HEADS for _ in range(n // cs)]
    for h in range(N_HEADS):
        hs = slice(h * HEAD, (h + 1) * HEAD)
        s_cur = state[h]
        for ci in range(n // cs):
            rows = slice(ci * cs, (ci + 1) * cs)
            lw_end = lw[ci * cs + cs - 1:ci * cs + cs, hs]
            w_end = jnp.exp(lw_end)
            ratio = jnp.exp(lw_end - lw[rows, hs])
            at_c, rt_c, bt_c, kt_c = at[rows, hs], rt[rows, hs], bt[rows, hs], kt[rows, hs]
            v_c = v[rows, hs]
            bh = b_v[rows, hs] * ratio
            kh = k[rows, hs] * ratio
            a_ab = jnp.where(strict, _mm_nt(at_c, bt_c), 0.0)
            a_ak = jnp.where(strict, _mm_nt(at_c, kt_c), 0.0)
            a_rb = jnp.where(lower, _mm_nt(rt_c, bt_c), 0.0)
            a_rk = jnp.where(lower, _mm_nt(rt_c, kt_c), 0.0)
            tinv = eye + a_ab
            pw = a_ab
            for _ in range(int(math.log2(cs)) - 1):
                pw = _mm(pw, pw, precise=True)
                tinv = tinv + _mm(pw, tinv, precise=True)
            ta = _mm(tinv, at_c, precise=True)
            pv = _mm(tinv, _mm(a_ak, v_c))
            q_eff = rt_c + _mm(a_rb, ta)
            o_v = _mm(a_rb, pv) + _mm(a_rk, v_c)
            m_eff = eye * w_end + _mm_tn(ta, bh)
            s_v = _mm_tn(pv, bh) + _mm_tn(v_c, kh)
            outs[ci][h] = _mm_nt(q_eff, s_cur) + o_v
            s_cur = _mm(s_cur, m_eff) + s_v
        state[h] = s_cur
    o = jnp.concatenate([jnp.concatenate(oc, axis=1) for oc in outs], axis=0)

    inv = 1.0 / HEAD
    mean = _mm(o, seg, precise=True) * inv
    d = o - mean
    var = _mm(d * d, seg, precise=True) * inv
    o = d * lax.rsqrt(var + R_GN_EPS) * lnw_ref[...] + lnb_ref[...]
    o = o + _mm(r * k * rk_ref[...], seg, precise=True) * v
    o_ref[0] = (o * g).astype(o_ref.dtype)


def _rwkv(u_r, mu, w0, w2, a0, a2, g2, k_k, k_a, r_k, ln_w, ln_b):
    b, l, _ = u_r.shape
    f = lambda t: t.astype(F32).reshape(1, -1)
    w_lora = jnp.zeros((LANES, 3 * W_GROUP), F32)
    w_lora = w_lora.at[0:32, 0:W_GROUP].set(w2.astype(F32))
    w_lora = w_lora.at[32:64, W_GROUP:2 * W_GROUP].set(a2.astype(F32))
    w_lora = w_lora.at[64:128, 2 * W_GROUP:].set(g2.astype(F32))
    hid = jnp.arange(W_GROUP) // HEAD
    seg = (hid[:, None] == hid[None, :]).astype(F32)
    vec = _full((1, W_GROUP))
    return pl.pallas_call(
        _rwkv_kernel,
        grid=(b, l // R_BLOCK),
        in_specs=[pl.BlockSpec((1, R_BLOCK, R_COLS), lambda i, j: (i, j, 0)),
                  _full((1, R_COLS)), _full((LANES, 3 * W_GROUP)),
                  vec, vec, vec, vec, vec, vec, vec, _full((W_GROUP, W_GROUP))],
        out_specs=pl.BlockSpec((1, R_BLOCK, W_GROUP), lambda i, j: (i, j, 0)),
        out_shape=jax.ShapeDtypeStruct((b, l, W_GROUP), BF16),
        scratch_shapes=[pltpu.VMEM((R_BLOCK + 8, R_COLS), F32), pltpu.VMEM((N_HEADS, HEAD, HEAD), F32)],
        compiler_params=_params("parallel", "arbitrary"),
        name="rwkv7",
    )(u_r, f(mu), w_lora, f(w0), f(a0), f(k_k), f(k_a), f(r_k), f(ln_w), f(ln_b), seg)


def _dsa_kernel(u_ref, kvw_ref, wkv_ref, bias_ref, o_ref,
                kt_scr, kv_scr, kit_scr, key_scr, q_scr, qi_scr, acc_scr, m_scr, l_scr, *, top_k):
    qb = pl.program_id(1)
    n = Q_BLOCK
    seq = key_scr.shape[1]

    @pl.when(qb == 0)
    def _():
        kt_scr[...] = jnp.zeros_like(kt_scr)
        kv_scr[...] = jnp.zeros_like(kv_scr)
        kit_scr[...] = jnp.zeros_like(kit_scr)
        key_scr[...] = jnp.full(key_scr.shape, INT_MIN, I32)

    u = u_ref[0]
    q = u[:, 0:W_GROUP]
    ckv = u[:, W_GROUP:W_GROUP + LANES]
    qi = u[:, W_GROUP + LANES:2 * W_GROUP + LANES]
    tail = u[:, 2 * W_GROUP + LANES:]
    ckv = ckv * lax.rsqrt(jnp.mean(ckv * ckv, axis=-1, keepdims=True) + NORM_EPS) * kvw_ref[...]
    kv = _mm(ckv, wkv_ref[...])
    start = pl.multiple_of(qb * n, n)
    kt_scr[:, pl.ds(start, n)] = kv.T[0:HEAD, :]
    kv_scr[pl.ds(start, n), :] = kv
    kit_scr[:, pl.ds(start, n)] = tail.T[0:I_DIM, :]
    for h in range(N_HEADS):
        q_scr[h] = q[:, h * HEAD:(h + 1) * HEAD] * (HEAD ** -0.5)
    for h in range(I_HEADS):
        qi_scr[h] = qi[:, h * I_DIM:(h + 1) * I_DIM]
    wi = tail[:, I_DIM:I_DIM + I_HEADS] * (I_HEADS ** -0.5 * I_DIM ** -0.5)

    q_pos = start + lax.broadcasted_iota(I32, (n, n), 0)
    lane_n = lax.broadcasted_iota(I32, (n, n), 1)

    def idx_tile(j, carry):
        ks = pl.multiple_of(j * n, n)
        kit = kit_scr[:, pl.ds(ks, n)]
        s = jnp.zeros((n, n), F32)
        for h in range(I_HEADS):
            s = s + jnp.maximum(_mm(qi_scr[h], kit, precise=True), 0.0) * wi[:, h:h + 1]
        s = s + 0.0
        bits = pltpu.bitcast(s, I32)
        key = jnp.where(bits < 0, bits ^ jnp.int32(0x7FFFFFFF), bits)
        key = jnp.where(ks + lane_n <= q_pos, key, jnp.int32(INT_MIN))
        key_scr[:, pl.ds(ks, n)] = key
        return carry

    lax.fori_loop(0, qb + 1, idx_tile, 0)

    wide = 4 * n
    n_wide = (start + n + wide - 1) // wide

    def count(pred):
        def body(j, acc):
            ks = pl.multiple_of(j * wide, wide)
            kk = key_scr[:, pl.ds(ks, wide)]
            idx = ks + lax.broadcasted_iota(I32, (n, wide), 1)
            hit = pred(kk, idx).astype(I32)
            return acc + hit[:, 0:n] + hit[:, n:2 * n] + hit[:, 2 * n:3 * n] + hit[:, 3 * n:]
        acc = lax.fori_loop(0, n_wide, body, jnp.zeros((n, n), I32))
        return jnp.sum(acc, axis=1, keepdims=True)

    c0 = count(lambda kk, idx: kk >= 0)
    thr = jnp.where(c0 >= top_k, jnp.int32(0), jnp.int32(INT_MIN))
    for bit in range(30, -1, -1):
        cand = thr | jnp.int32(1 << bit)
        cnt = count(lambda kk, idx: kk >= cand)
        thr = jnp.where(cnt >= top_k, cand, thr)
    need = top_k - count(lambda kk, idx: kk > thr)
    n_eq = count(lambda kk, idx: kk == thr)
    tied = jnp.logical_and(n_eq > need, thr != INT_MIN)
    m_scr[0] = jnp.full((n, 1), seq, I32).astype(F32)

    @pl.when(jnp.max(tied.astype(I32)) > 0)
    def _():
        m = jnp.zeros((n, 1), I32)
        for bit in range(int(math.log2(seq)) - 1, -1, -1):
            cand = m | jnp.int32(1 << bit)
            cnt = count(lambda kk, idx: jnp.logical_and(kk == thr, idx < cand))
            m = jnp.where(cnt < need, cand, m)
        m_scr[0] = jnp.where(tied, m, seq).astype(F32)

    cut = m_scr[0].astype(I32)

    for h in range(N_HEADS):
        acc_scr[h] = jnp.zeros((n, LANES), F32)
        m_scr[1 + h] = jnp.full((n, 1), NEG, F32)
        l_scr[h] = jnp.zeros((n, 1), F32)

    def attn_tile(j, carry):
        ks = pl.multiple_of(j * n, n)
        kk = key_scr[:, pl.ds(ks, n)]
        idx = ks + lane_n
        sel = jnp.logical_or(kk > thr, jnp.logical_and(kk == thr, idx <= cut))
        sel = jnp.logical_and(sel, kk != INT_MIN)
        kt = kt_scr[:, pl.ds(ks, n)]
        kvt = kv_scr[pl.ds(ks, n), :]
        bsel = jnp.minimum(qb - j, 2)
        for h in range(N_HEADS):
            s = _mm(q_scr[h], kt) + bias_ref[bsel, h]
            s = jnp.where(sel, s, NEG)
            m_old = m_scr[1 + h]
            m_new = jnp.maximum(m_old, jnp.max(s, axis=1, keepdims=True))
            alpha = jnp.exp(m_old - m_new)
            p = jnp.where(sel, jnp.exp(s - m_new), 0.0)
            l_scr[h] = alpha * l_scr[h] + jnp.sum(p, axis=1, keepdims=True)
            acc_scr[h] = alpha * acc_scr[h] + _mm(p, kvt)
            m_scr[1 + h] = m_new
        return carry

    lax.fori_loop(0, qb + 1, attn_tile, 0)
    o_ref[0] = jnp.concatenate(
        [acc_scr[h][:, HEAD:] / l_scr[h] for h in range(N_HEADS)], axis=1).astype(o_ref.dtype)


def _t5_bucket(dist):
    n = jnp.maximum(dist, 0)
    max_exact = REL_BUCKETS // 2
    log_ratio = jnp.log(jnp.maximum(n, 1).astype(F32) / max_exact) / math.log(REL_MAX_DIST / max_exact)
    large = jnp.minimum(max_exact + (log_ratio * (REL_BUCKETS - max_exact)).astype(I32), REL_BUCKETS - 1)
    return jnp.where(n < max_exact, n, large)


def _dsa(u_c, kv_norm_w, w_uk, w_uv, rel_bias):
    b, l, _ = u_c.shape
    top_k = min(TOPK_MAX, l // 4)
    i = jnp.arange(Q_BLOCK, dtype=I32)
    dist = jnp.stack([i[:, None] - i[None, :] + off for off in (0, Q_BLOCK, 2 * Q_BLOCK)])
    bias = jnp.moveaxis(rel_bias.astype(F32)[_t5_bucket(dist)], -1, 1)
    w_kv = jnp.concatenate([w_uk, w_uv], axis=1).astype(F32)
    kern = functools.partial(_dsa_kernel, top_k=top_k)
    return pl.pallas_call(
        kern,
        grid=(b, l // Q_BLOCK),
        in_specs=[pl.BlockSpec((1, Q_BLOCK, C_COLS), lambda i, j: (i, j, 0)),
                  _full((1, LANES)), _full((LANES, LANES)), _full((3, N_HEADS, Q_BLOCK, Q_BLOCK))],
        out_specs=pl.BlockSpec((1, Q_BLOCK, W_GROUP), lambda i, j: (i, j, 0)),
        out_shape=jax.ShapeDtypeStruct((b, l, W_GROUP), BF16),
        scratch_shapes=[pltpu.VMEM((HEAD, l), F32), pltpu.VMEM((l, LANES), F32), pltpu.VMEM((I_DIM, l), F32),
                        pltpu.VMEM((Q_BLOCK, l), I32),
                        pltpu.VMEM((N_HEADS, Q_BLOCK, HEAD), F32), pltpu.VMEM((I_HEADS, Q_BLOCK, I_DIM), F32),
                        pltpu.VMEM((N_HEADS, Q_BLOCK, LANES), F32),
                        pltpu.VMEM((1 + N_HEADS, Q_BLOCK, 1), F32), pltpu.VMEM((N_HEADS, Q_BLOCK, 1), F32)],
        compiler_params=_params("parallel", "arbitrary"),
        name="dsa",
    )(u_c, kv_norm_w.astype(F32)[None, :], w_kv, bias)


def _s5_kernel(u_ref, are_ref, aim_ref, ldt_ref, bre_ref, bim_ref, cre_ref, cim_ref, dsk_ref, gw_ref, gb_ref,
               o_ref, pre, pim, bdr, bdi, hre, him, carry):
    first = jnp.logical_and(pl.program_id(0) == 0, pl.program_id(1) == 0)
    n = S_BLOCK
    pad = LANES

    @pl.when(first)
    def _():
        lam_re = jnp.minimum(are_ref[...], -1e-4)
        lam_im = aim_ref[...]
        dt = jnp.exp(ldt_ref[...])
        mag = jnp.exp(dt * lam_re)
        ab_re = mag * jnp.cos(dt * lam_im)
        ab_im = mag * jnp.sin(dt * lam_im)
        den = lam_re * lam_re + lam_im * lam_im
        f_re = ((ab_re - 1.0) * lam_re + ab_im * lam_im) / den
        f_im = (ab_im * lam_re - (ab_re - 1.0) * lam_im) / den
        b_re = bre_ref[...]
        b_im = bim_ref[...]
        bb_re = f_re * b_re - f_im * b_im
        bb_im = f_re * b_im + f_im * b_re
        rg = lax.broadcasted_iota(I32, (W_GROUP, S_LANES), 0) // S_GROUP_CH
        cg = lax.broadcasted_iota(I32, (W_GROUP, S_LANES), 1) // S_STATE
        same = rg == cg
        bdr[...] = jnp.where(same, jnp.concatenate([bb_re] * S_GROUPS, axis=0), 0.0)
        bdi[...] = jnp.where(same, jnp.concatenate([bb_im] * S_GROUPS, axis=0), 0.0)
        pre[0:1, :] = ab_re
        pim[0:1, :] = ab_im
        d = 1
        while d < n:
            sr = pre[d - 1:d, :]
            si = pim[d - 1:d, :]
            xr = pre[0:d, :]
            xi = pim[0:d, :]
            pre[d:2 * d, :] = xr * sr - xi * si
            pim[d:2 * d, :] = xr * si + xi * sr
            d *= 2
        hre[...] = jnp.zeros_like(hre)
        him[...] = jnp.zeros_like(him)

    @pl.when(pl.program_id(1) == 0)
    def _():
        carry[...] = jnp.zeros_like(carry)

    u = u_ref[0]
    hre[0, pad:pad + n, :] = _mm(u, bdr[...])
    him[0, pad:pad + n, :] = _mm(u, bdi[...])

    def col_block(cb, c_):
        cs = pl.ds(pl.multiple_of(cb * LANES, LANES), LANES)
        src = 0
        d = 1
        while d < n:
            ar = pre[d - 1:d, cs]
            ai = pim[d - 1:d, cs]
            xr = hre[src, pl.ds(pad - d, n), cs]
            xi = him[src, pl.ds(pad - d, n), cs]
            hre[1 - src, pad:pad + n, cs] = hre[src, pad:pad + n, cs] + ar * xr - ai * xi
            him[1 - src, pad:pad + n, cs] = him[src, pad:pad + n, cs] + ar * xi + ai * xr
            src = 1 - src
            d *= 2
        assert src == 0
        return c_

    lax.fori_loop(0, S_LANES // LANES, col_block, 0)

    c_re = carry[0:1, :]
    c_im = carry[1:2, :]
    p_re = pre[...]
    p_im = pim[...]
    h_re = hre[0, pad:pad + n, :] + p_re * c_re - p_im * c_im
    h_im = him[0, pad:pad + n, :] + p_re * c_im + p_im * c_re
    carry[0:1, :] = h_re[n - 1:n, :]
    carry[1:2, :] = h_im[n - 1:n, :]
    y = _mm(h_re, cre_ref[...]) - _mm(h_im, cim_ref[...]) + dsk_ref[...] * u
    y = jax.nn.gelu(y)
    gl = _mm(y, gw_ref[...]) + gb_ref[...]
    o_ref[0] = (gl[:, 0:W_GROUP] * jax.nn.sigmoid(gl[:, W_GROUP:])).astype(o_ref.dtype)


def _s5(u_s, a_re, a_im, b_re, b_im, c_re, c_im, d_skip, log_dt, glu_w, glu_b):
    b, l, _ = u_s.shape
    f = lambda t: t.astype(F32).reshape(1, -1)
    eye = jnp.eye(S_GROUPS, dtype=F32)
    bt = lambda t: jnp.transpose(t.astype(F32), (2, 0, 1)).reshape(S_GROUP_CH, S_LANES)
    cbd = lambda t: jnp.einsum('gpn,gh->gnhp', t.astype(F32), eye).reshape(S_LANES, W_GROUP)
    ldt = jnp.repeat(log_dt.astype(F32), S_STATE)[None, :]
    n_log = int(math.log2(S_BLOCK))
    assert n_log % 2 == 0
    return pl.pallas_call(
        _s5_kernel,
        grid=(b, l // S_BLOCK),
        in_specs=[pl.BlockSpec((1, S_BLOCK, S_COLS), lambda i, j: (i, j, 0)),
                  _full((1, S_LANES)), _full((1, S_LANES)), _full((1, S_LANES)),
                  _full((S_GROUP_CH, S_LANES)), _full((S_GROUP_CH, S_LANES)),
                  _full((S_LANES, W_GROUP)), _full((S_LANES, W_GROUP)),
                  _full((1, W_GROUP)), _full((W_GROUP, 2 * W_GROUP)), _full((1, 2 * W_GROUP))],
        out_specs=pl.BlockSpec((1, S_BLOCK, W_GROUP), lambda i, j: (i, j, 0)),
        out_shape=jax.ShapeDtypeStruct((b, l, W_GROUP), BF16),
        scratch_shapes=[pltpu.VMEM((S_BLOCK, S_LANES), F32), pltpu.VMEM((S_BLOCK, S_LANES), F32),
                        pltpu.VMEM((W_GROUP, S_LANES), F32), pltpu.VMEM((W_GROUP, S_LANES), F32),
                        pltpu.VMEM((2, S_BLOCK + LANES, S_LANES), F32), pltpu.VMEM((2, S_BLOCK + LANES, S_LANES), F32),
                        pltpu.VMEM((8, S_LANES), F32)],
        compiler_params=_params("arbitrary", "arbitrary"),
        name="s5",
    )(u_s, f(a_re), f(a_im), ldt, bt(b_re), bt(b_im), cbd(c_re), cbd(c_im), f(d_skip),
      glu_w.astype(F32), f(glu_b))


def _out_proj_kernel(x_ref, ym_ref, yr_ref, yc_ref, ys_ref, w_ref, o_ref):
    acc = x_ref[...]
    for i, y_ref in enumerate((ym_ref, yr_ref, yc_ref, ys_ref)):
        acc = acc + jnp.dot(y_ref[...], w_ref[i * W_GROUP:(i + 1) * W_GROUP, :], preferred_element_type=F32)
    o_ref[...] = acc


def _out_proj(x2, ys, w_out, tm=512):
    t, d = x2.shape
    yspec = pl.BlockSpec((tm, W_GROUP), lambda i: (i, 0))
    return pl.pallas_call(
        _out_proj_kernel,
        grid=(t // tm,),
        in_specs=[pl.BlockSpec((tm, d), lambda i: (i, 0)), yspec, yspec, yspec, yspec, _full(w_out.shape)],
        out_specs=pl.BlockSpec((tm, d), lambda i: (i, 0)),
        out_shape=jax.ShapeDtypeStruct((t, d), F32),
        compiler_params=_params("parallel"),
        name="out_proj",
    )(x2, *ys, w_out)


def _moe_kernel(x_ref, nw_ref, wr_ref, br_ref, wg_ref, wu_ref, wd_ref, fw_ref, o_ref, t_scr, cw_scr, acc_scr,
                *, final_norm):
    e = pl.program_id(1)
    tm = x_ref.shape[0]
    lane = lax.broadcasted_iota(I32, (tm, LANES), 1)

    @pl.when(e == 0)
    def _():
        x = x_ref[...]
        t = x * lax.rsqrt(jnp.mean(x * x, axis=-1, keepdims=True) + NORM_EPS) * nw_ref[...]
        t_scr[...] = t.astype(BF16)
        logits = _mm(t, wr_ref[...], precise=True) + br_ref[...]
        big = jnp.int32(LANES)
        is_g = lane < E_GROUPS
        gl = jnp.where(is_g, logits, -jnp.inf)
        gmax = jnp.max(gl, axis=1, keepdims=True)
        g_sel = jnp.min(jnp.where(jnp.logical_and(is_g, gl == gmax), lane, big), axis=1, keepdims=True)
        g_gate = 1.0 / jnp.sum(jnp.exp(gl - gmax), axis=1, keepdims=True)
        lo = E_GROUPS + g_sel * E_PER_GROUP
        in_g = jnp.logical_and(lane >= lo, lane < lo + E_PER_GROUP)
        el = jnp.where(in_g, logits, -jnp.inf)
        m1 = jnp.max(el, axis=1, keepdims=True)
        i1 = jnp.min(jnp.where(el == m1, lane, big), axis=1, keepdims=True)
        el2 = jnp.where(lane == i1, -jnp.inf, el)
        m2 = jnp.max(el2, axis=1, keepdims=True)
        i2 = jnp.min(jnp.where(el2 == m2, lane, big), axis=1, keepdims=True)
        e2 = jnp.exp(m2 - m1)
        w1 = 1.0 / (1.0 + e2)
        w2 = e2 / (1.0 + e2)
        cw_scr[...] = jnp.where(lane == i1, w1, jnp.where(lane == i2, w2, 0.0)) * g_gate
        acc_scr[...] = jnp.zeros_like(acc_scr)

    t = t_scr[...]
    cw_e = jnp.sum(jnp.where(lane == e + E_GROUPS, cw_scr[...], 0.0), axis=1, keepdims=True)
    hid = _silu(jnp.dot(t, wg_ref[0], preferred_element_type=F32)) * jnp.dot(t, wu_ref[0], preferred_element_type=F32)
    hid = (hid * cw_e).astype(BF16)
    acc_scr[...] += jnp.dot(hid, wd_ref[0], preferred_element_type=F32)

    @pl.when(e == pl.num_programs(1) - 1)
    def _():
        y = x_ref[...] + acc_scr[...]
        if final_norm:
            y = y * lax.rsqrt(jnp.mean(y * y, axis=-1, keepdims=True) + NORM_EPS) * fw_ref[...]
        o_ref[...] = y


def _moe(x2, norm_w, wr_g, br_g, wr_e, br_e, w_gate, w_up, w_down, final_w, final_norm, tm=1024):
    t, d = x2.shape
    wr = jnp.zeros((d, LANES), F32).at[:, 0:E_GROUPS].set(wr_g.astype(F32))
    wr = wr.at[:, E_GROUPS:E_GROUPS + N_EXPERTS].set(wr_e.astype(F32))
    br = jnp.zeros((1, LANES), F32).at[0, 0:E_GROUPS].set(br_g.astype(F32))
    br = br.at[0, E_GROUPS:E_GROUPS + N_EXPERTS].set(br_e.astype(F32))
    wg = w_gate.reshape(N_EXPERTS, d, D_EXPERT).astype(BF16)
    wu = w_up.reshape(N_EXPERTS, d, D_EXPERT).astype(BF16)
    wd = w_down.reshape(N_EXPERTS, D_EXPERT, d).astype(BF16)
    kern = functools.partial(_moe_kernel, final_norm=final_norm)
    return pl.pallas_call(
        kern,
        grid=(t // tm, N_EXPERTS),
        in_specs=[pl.BlockSpec((tm, d), lambda i, e: (i, 0)), _full((1, d)), _full((d, LANES)), _full((1, LANES)),
                  pl.BlockSpec((1, d, D_EXPERT), lambda i, e: (e, 0, 0)),
                  pl.BlockSpec((1, d, D_EXPERT), lambda i, e: (e, 0, 0)),
                  pl.BlockSpec((1, D_EXPERT, d), lambda i, e: (e, 0, 0)), _full((1, d))],
        out_specs=pl.BlockSpec((tm, d), lambda i, e: (i, 0)),
        out_shape=jax.ShapeDtypeStruct((t, d), F32),
        scratch_shapes=[pltpu.VMEM((tm, d), BF16), pltpu.VMEM((tm, LANES), F32), pltpu.VMEM((tm, d), F32)],
        compiler_params=_params("parallel", "arbitrary"),
        name="moe",
    )(x2, norm_w.astype(F32)[None, :], wr, br, wg, wu, wd, final_w.astype(F32)[None, :])


def _pad_in_proj(w):
    d = w.shape[0]
    m_in, r_in = 772, 896
    c_used = 680
    z = lambda k: jnp.zeros((d, k), w.dtype)
    wm = w[:, 0:m_in]
    wr = w[:, m_in:m_in + r_in]
    wc = w[:, m_in + r_in:m_in + r_in + c_used]
    ws = w[:, m_in + r_in + c_used:]
    return jnp.concatenate([wm, z(M_COLS - m_in), wr, wc, z(C_COLS - c_used), ws], axis=1).astype(BF16)


def kernel(x, norm1_w, w_in, m_conv_w, m_conv_b, m_dt_bias, m_a_log, m_d, m_norm_w, r_mu, r_w0, r_w2, r_a0, r_a2, r_g2, r_k_k, r_k_a, r_r_k, r_ln_w, r_ln_b, c_kv_norm_w, c_w_uk, c_w_uv, rel_bias, s_a_re, s_a_im, s_b_re, s_b_im, s_c_re, s_c_im, s_d, s_log_dt, s_glu_w, s_glu_b, w_out, norm2_w, moe_wr_group, moe_br_group, moe_wr_exp, moe_br_exp, moe_w_gate, moe_w_up, moe_w_down, final_norm_w):
    bsz, seq, d = x.shape
    depth = w_in.shape[0]
    x2 = x.astype(F32).reshape(bsz * seq, d)
    for i in range(depth):
        u_m, u_r, u_c, u_s = _in_proj(x2, norm1_w[i].astype(F32)[None, :], _pad_in_proj(w_in[i]))
        sh = lambda t: t.reshape(bsz, seq, t.shape[-1])
        y_m = _mamba(sh(u_m), m_conv_w[i], m_conv_b[i], m_dt_bias[i], m_a_log[i], m_d[i], m_norm_w[i])
        y_r = _rwkv(sh(u_r), r_mu[i], r_w0[i], r_w2[i], r_a0[i], r_a2[i], r_g2[i],
                    r_k_k[i], r_k_a[i], r_r_k[i], r_ln_w[i], r_ln_b[i])
        y_c = _dsa(sh(u_c), c_kv_norm_w[i], c_w_uk[i], c_w_uv[i], rel_bias)
        y_s = _s5(sh(u_s), s_a_re[i], s_a_im[i], s_b_re[i], s_b_im[i], s_c_re[i], s_c_im[i],
                  s_d[i], s_log_dt[i], s_glu_w[i], s_glu_b[i])
        ys = [t.reshape(bsz * seq, W_GROUP) for t in (y_m, y_r, y_c, y_s)]
        x2 = _out_proj(x2, ys, w_out[i].astype(BF16))
        x2 = _moe(x2, norm2_w[i], moe_wr_group[i], moe_br_group[i], moe_wr_exp[i], moe_br_exp[i],
                  moe_w_gate[i], moe_w_up[i], moe_w_down[i], final_norm_w, final_norm=(i == depth - 1))
    return x2.reshape(bsz, seq, d).astype(x.dtype)
```

```python
import functools
import math

import jax
import jax.numpy as jnp
from jax import lax
from jax.experimental import pallas as pl
from jax.experimental.pallas import tpu as pltpu

F32 = jnp.float32
BF16 = jnp.bfloat16
I32 = jnp.int32
HIGHEST = lax.Precision.HIGHEST

LANES = 128
VMEM_LIMIT = 56 * 1024 * 1024

NORM_EPS = 1e-6
W_GROUP = 256
HEAD = 64
N_HEADS = W_GROUP // HEAD
M_CONV_CH = 512
M_CONV = 4
M_COLS = 896
R_COLS = 896
C_COLS = 768
S_COLS = 256
SSD_CHUNK = 128
R_CHUNK = 64
R_BLOCK = 128
R_GN_EPS = 64e-5
I_HEADS = 8
I_DIM = 32
Q_BLOCK = 128
DSA_WIDE = 512
TOPK_MAX = 256
REL_BUCKETS = 32
REL_MAX_DIST = 128
S_GROUPS = 16
S_GROUP_CH = 16
S_STATE = 64
S_LANES = S_GROUPS * S_STATE
S_BLOCK = 256
E_GROUPS = 4
E_PER_GROUP = 8
N_EXPERTS = E_GROUPS * E_PER_GROUP
D_EXPERT = 256
INT_MIN = -2 ** 31
NEG = -1e30


def _mm(a, b, precise=False):
    if precise:
        return jnp.dot(a.astype(F32), b.astype(F32), precision=HIGHEST, preferred_element_type=F32)
    return jnp.dot(a.astype(BF16), b.astype(BF16), preferred_element_type=F32)


def _mm_nt(a, b, precise=False):
    dn = (((1,), (1,)), ((), ()))
    if precise:
        return lax.dot_general(a.astype(F32), b.astype(F32), dn, precision=HIGHEST, preferred_element_type=F32)
    return lax.dot_general(a.astype(BF16), b.astype(BF16), dn, preferred_element_type=F32)


def _mm_tn(a, b, precise=False):
    dn = (((0,), (0,)), ((), ()))
    if precise:
        return lax.dot_general(a.astype(F32), b.astype(F32), dn, precision=HIGHEST, preferred_element_type=F32)
    return lax.dot_general(a.astype(BF16), b.astype(BF16), dn, preferred_element_type=F32)


def _silu(x):
    return x * jax.nn.sigmoid(x)


def _softplus(x):
    return jnp.maximum(x, 0.0) + jnp.log(1.0 + jnp.exp(-jnp.abs(x)))


def _params(*sem):
    return pltpu.CompilerParams(dimension_semantics=sem, vmem_limit_bytes=VMEM_LIMIT)


def _full(shape):
    nd = len(shape)
    return pl.BlockSpec(shape, lambda *_: (0,) * nd)


def _in_proj_kernel(x_ref, nw_ref, w_ref, om_ref, or_ref, oc_ref, os_ref):
    x = x_ref[...]
    h = x * lax.rsqrt(jnp.mean(x * x, axis=-1, keepdims=True) + NORM_EPS) * nw_ref[...]
    h = h.astype(BF16)
    off = 0
    for o_ref, width in ((om_ref, M_COLS), (or_ref, R_COLS), (oc_ref, C_COLS), (os_ref, S_COLS)):
        o_ref[...] = jnp.dot(h, w_ref[:, off:off + width], preferred_element_type=F32)
        off += width


def _in_proj(x2, norm_w, w_pad, tm=512):
    t, d = x2.shape
    n = w_pad.shape[1]
    return pl.pallas_call(
        _in_proj_kernel,
        grid=(t // tm,),
        in_specs=[pl.BlockSpec((tm, d), lambda i: (i, 0)), _full((1, d)), _full((d, n))],
        out_specs=[pl.BlockSpec((tm, c), lambda i: (i, 0)) for c in (M_COLS, R_COLS, C_COLS, S_COLS)],
        out_shape=[jax.ShapeDtypeStruct((t, c), F32) for c in (M_COLS, R_COLS, C_COLS, S_COLS)],
        compiler_params=_params("parallel"),
        name="in_proj",
    )(x2, norm_w, w_pad)


def _mamba_kernel(u_ref, cw_ref, cb_ref, dtb_ref, alog_ref, dsk_ref, nw_ref, o_ref, xbuf, state):
    c = pl.program_id(1)
    n = SSD_CHUNK

    @pl.when(c == 0)
    def _():
        xbuf[0:8, :] = jnp.zeros((8, M_CONV_CH), F32)
        state[...] = jnp.zeros_like(state)

    u = u_ref[0]
    z = u[:, 0:W_GROUP]
    dtr = u[:, W_GROUP + M_CONV_CH:M_COLS]
    xbuf[8:8 + n, :] = u[:, W_GROUP:W_GROUP + M_CONV_CH]
    acc = jnp.broadcast_to(cb_ref[...], (n, M_CONV_CH))
    for j in range(M_CONV):
        acc = acc + cw_ref[j:j + 1, :] * xbuf[pl.ds(8 - (M_CONV - 1) + j, n), :]
    xbuf[0:8, :] = xbuf[n:n + 8, :]
    xc = _silu(acc)
    xs = xc[:, 0:W_GROUP]
    bm = xc[:, W_GROUP:W_GROUP + 2 * HEAD]
    cm = xc[:, W_GROUP + 2 * HEAD:]

    dt = _softplus(dtr + dtb_ref[...])
    a = dt * (-jnp.exp(alog_ref[...]))
    row = lax.broadcasted_iota(I32, (n, n), 0)
    col = lax.broadcasted_iota(I32, (n, n), 1)
    causal = row >= col
    a_cum = _mm(causal.astype(F32), a, precise=True)
    a_cum_t = a_cum.T
    a_last = a_cum[n - 1:n, :]

    ys = []
    for h in range(N_HEADS):
        g = h // 2
        b_h = bm[:, g * HEAD:(g + 1) * HEAD]
        c_h = cm[:, g * HEAD:(g + 1) * HEAD]
        x_h = xs[:, h * HEAD:(h + 1) * HEAD]
        xdt = x_h * dt[:, h:h + 1]
        ac = a_cum[:, h:h + 1]
        dec = jnp.exp(jnp.where(causal, ac - a_cum_t[h:h + 1, :], -jnp.inf))
        scores = _mm_nt(c_h, b_h) * dec
        y = _mm(scores, xdt)
        st = state[h]
        y = y + _mm_nt(c_h, st) * jnp.exp(ac)
        al = a_last[:, h:h + 1]
        state[h] = st * jnp.exp(al) + _mm_tn(xdt, b_h * jnp.exp(al - ac))
        ys.append(y + x_h * dsk_ref[:, h * HEAD:(h + 1) * HEAD])
    y = jnp.concatenate(ys, axis=1) * _silu(z)
    y = y * lax.rsqrt(jnp.mean(y * y, axis=-1, keepdims=True) + NORM_EPS) * nw_ref[...]
    o_ref[0] = y.astype(o_ref.dtype)


def _mamba(u_m, conv_w, conv_b, dt_bias, a_log, d_skip, norm_w):
    b, l, _ = u_m.shape
    pad = lambda v: jnp.pad(v.astype(F32), (0, LANES - v.shape[0]))[None, :]
    return pl.pallas_call(
        _mamba_kernel,
        grid=(b, l // SSD_CHUNK),
        in_specs=[pl.BlockSpec((1, SSD_CHUNK, M_COLS), lambda i, j: (i, j, 0)),
                  _full((M_CONV, M_CONV_CH)), _full((1, M_CONV_CH)), _full((1, LANES)), _full((1, LANES)),
                  _full((1, W_GROUP)), _full((1, W_GROUP))],
        out_specs=pl.BlockSpec((1, SSD_CHUNK, W_GROUP), lambda i, j: (i, j, 0)),
        out_shape=jax.ShapeDtypeStruct((b, l, W_GROUP), BF16),
        scratch_shapes=[pltpu.VMEM((SSD_CHUNK + 8, M_CONV_CH), F32), pltpu.VMEM((N_HEADS, HEAD, HEAD), F32)],
        compiler_params=_params("parallel", "arbitrary"),
        name="mamba_ssd",
    )(u_m, conv_w.astype(F32), conv_b.astype(F32)[None, :], pad(dt_bias), pad(a_log),
      jnp.repeat(d_skip.astype(F32), HEAD)[None, :], norm_w.astype(F32)[None, :])


def _rwkv_kernel(u_ref, mu_ref, wl_ref, w0_ref, a0_ref, kk_ref, ka_ref, rk_ref, lnw_ref, lnb_ref, seg_ref,
                 o_ref, ubuf, state):
    c = pl.program_id(1)
    n = R_BLOCK
    cs = R_CHUNK

    @pl.when(c == 0)
    def _():
        ubuf[0:8, :] = jnp.zeros((8, R_COLS), F32)
        state[...] = jnp.zeros_like(state)

    u = u_ref[0]
    ubuf[8:8 + n, :] = u
    u = u + mu_ref[...] * (ubuf[pl.ds(7, n), :] - u)
    ubuf[0:8, :] = ubuf[n:n + 8, :]
    r = u[:, 0:W_GROUP]
    k = u[:, W_GROUP:2 * W_GROUP]
    v = u[:, 2 * W_GROUP:3 * W_GROUP]
    lo = u[:, 3 * W_GROUP:]
    lane = lax.broadcasted_iota(I32, lo.shape, 1)
    lo = jnp.where(lane < 32, jnp.tanh(lo), jnp.where(lane < 64, lo, jax.nn.sigmoid(lo)))
    proj = _mm(lo, wl_ref[...], precise=True)
    w_log = -_softplus(-(w0_ref[...] + proj[:, 0:W_GROUP])) - 0.5
    ld = -jnp.exp(w_log)
    a_lr = jax.nn.sigmoid(a0_ref[...] + proj[:, W_GROUP:2 * W_GROUP])
    g = proj[:, 2 * W_GROUP:]
    seg = seg_ref[...]
    kk = k * kk_ref[...]
    kk = kk * lax.rsqrt(jnp.maximum(_mm(kk * kk, seg, precise=True), 1e-24))
    k = k * (1.0 + (a_lr - 1.0) * ka_ref[...])
    a_v = -kk
    b_v = kk * a_lr

    row = lax.broadcasted_iota(I32, (n, n), 0)
    col = lax.broadcasted_iota(I32, (n, n), 1)
    tri2 = ((row >= col) & ((row // cs) == (col // cs))).astype(F32)
    lw = _mm(tri2, ld, precise=True)
    w_inc = jnp.exp(lw)
    w_inv = jnp.exp(-lw)
    rt = r * w_inc
    at = a_v * jnp.exp(lw - ld)
    bt = b_v * w_inv
    kt = k * w_inv

    wg = W_GROUP
    r2 = lax.broadcasted_iota(I32, (wg, wg), 0)
    c2 = lax.broadcasted_iota(I32, (wg, wg), 1)
    lower = (r2 % cs) >= (c2 % cs)
    strict = (r2 % cs) > (c2 % cs)
    eye = (r2 == c2).astype(F32)
    lane_head = lax.broadcasted_iota(I32, (cs, wg), 1) // HEAD

    def blocks(xc, dtype=BF16):
        return jnp.concatenate([jnp.where(lane_head == h, xc, 0.0) for h in range(N_HEADS)], axis=0).astype(dtype)

    s_cur = state[...]
    outs = []
    for ci in range(n // cs):
        rows = slice(ci * cs, (ci + 1) * cs)
        lw_end = lw[ci * cs + cs - 1:ci * cs + cs, :]
        ratio = jnp.exp(lw_end - lw[rows, :])
        at_b, rt_b, bt_b, kt_b = blocks(at[rows, :]), blocks(rt[rows, :], F32), blocks(bt[rows, :]), blocks(kt[rows, :])
        v_b = blocks(v[rows, :])
        bh = blocks(b_v[rows, :] * ratio)
        kh = blocks(k[rows, :] * ratio)
        a_ab = jnp.where(strict, _mm_nt(at_b, bt_b), 0.0)
        a_ak = jnp.where(strict, _mm_nt(at_b, kt_b), 0.0)
        a_rb = jnp.where(lower, _mm_nt(rt_b, bt_b), 0.0)
        a_rk = jnp.where(lower, _mm_nt(rt_b, kt_b), 0.0)
        tinv = eye + a_ab
        pw = a_ab
        for _ in range(int(math.log2(cs)) - 1):
            pw = _mm(pw, pw)
            tinv = tinv + _mm(pw, tinv)
        ta = _mm(tinv, at_b)
        pv = _mm(tinv, _mm(a_ak, v_b))
        q_eff = rt_b + _mm(a_rb, ta)
        o_v = _mm(a_rb, pv) + _mm(a_rk, v_b)
        m_eff = eye * jnp.exp(lw_end) + _mm_tn(ta, bh)
        s_v = _mm_tn(pv, bh) + _mm_tn(v_b, kh)
        o_b = _mm_nt(q_eff, s_cur) + o_v
        outs.append(o_b[0:cs] + o_b[cs:2 * cs] + o_b[2 * cs:3 * cs] + o_b[3 * cs:])
        s_cur = _mm(s_cur, m_eff, precise=True) + s_v
    state[...] = s_cur
    o = jnp.concatenate(outs, axis=0)

    inv = 1.0 / HEAD
    mean = _mm(o, seg, precise=True) * inv
    d = o - mean
    var = _mm(d * d, seg, precise=True) * inv
    o = d * lax.rsqrt(var + R_GN_EPS) * lnw_ref[...] + lnb_ref[...]
    o = o + _mm(r * k * rk_ref[...], seg, precise=True) * v
    o_ref[0] = (o * g).astype(o_ref.dtype)


def _rwkv(u_r, mu, w0, w2, a0, a2, g2, k_k, k_a, r_k, ln_w, ln_b):
    b, l, _ = u_r.shape
    f = lambda t: t.astype(F32).reshape(1, -1)
    w_lora = jnp.zeros((LANES, 3 * W_GROUP), F32)
    w_lora = w_lora.at[0:32, 0:W_GROUP].set(w2.astype(F32))
    w_lora = w_lora.at[32:64, W_GROUP:2 * W_GROUP].set(a2.astype(F32))
    w_lora = w_lora.at[64:128, 2 * W_GROUP:].set(g2.astype(F32))
    hid = jnp.arange(W_GROUP) // HEAD
    seg = (hid[:, None] == hid[None, :]).astype(F32)
    vec = _full((1, W_GROUP))
    return pl.pallas_call(
        _rwkv_kernel,
        grid=(b, l // R_BLOCK),
        in_specs=[pl.BlockSpec((1, R_BLOCK, R_COLS), lambda i, j: (i, j, 0)),
                  _full((1, R_COLS)), _full((LANES, 3 * W_GROUP)),
                  vec, vec, vec, vec, vec, vec, vec, _full((W_GROUP, W_GROUP))],
        out_specs=pl.BlockSpec((1, R_BLOCK, W_GROUP), lambda i, j: (i, j, 0)),
        out_shape=jax.ShapeDtypeStruct((b, l, W_GROUP), BF16),
        scratch_shapes=[pltpu.VMEM((R_BLOCK + 8, R_COLS), F32), pltpu.VMEM((W_GROUP, W_GROUP), F32)],
        compiler_params=_params("parallel", "arbitrary"),
        name="rwkv7",
    )(u_r, f(mu), w_lora, f(w0), f(a0), f(k_k), f(k_a), f(r_k), f(ln_w), f(ln_b), seg)


def _dsa_kernel(u_ref, kvw_ref, wkv_ref, bias_ref, far_ref, o_ref,
                kt_scr, kv_scr, kit_scr, key_scr, q_scr, qi_scr, acc_scr, m_scr, l_scr, cut_scr, *, top_k):
    qb = pl.program_id(1)
    n = Q_BLOCK
    wide = DSA_WIDE
    seq = key_scr.shape[1] - wide

    @pl.when(qb == 0)
    def _():
        kt_scr[...] = jnp.zeros_like(kt_scr)
        kv_scr[...] = jnp.zeros_like(kv_scr)
        kit_scr[...] = jnp.zeros_like(kit_scr)
        key_scr[...] = jnp.full(key_scr.shape, INT_MIN, I32)

    u = u_ref[0]
    q = u[:, 0:W_GROUP]
    ckv = u[:, W_GROUP:W_GROUP + LANES]
    qi = u[:, W_GROUP + LANES:2 * W_GROUP + LANES]
    tail = u[:, 2 * W_GROUP + LANES:]
    ckv = ckv * lax.rsqrt(jnp.mean(ckv * ckv, axis=-1, keepdims=True) + NORM_EPS) * kvw_ref[...]
    kv = _mm(ckv, wkv_ref[...])
    start = pl.multiple_of(qb * n, n)
    end = pl.multiple_of(start + wide + n, n)
    kt_scr[:, pl.ds(end - n, n)] = kv.T[0:HEAD, :].astype(BF16)
    kv_scr[pl.ds(end - n, n), :] = kv.astype(BF16)
    kit_scr[:, pl.ds(end - n, n)] = tail.T[0:I_DIM, :].astype(BF16)
    for h in range(N_HEADS):
        q_scr[h * n:(h + 1) * n, :] = (q[:, h * HEAD:(h + 1) * HEAD] * (HEAD ** -0.5)).astype(BF16)
    for h in range(I_HEADS):
        qi_scr[h * n:(h + 1) * n, :] = qi[:, h * I_DIM:(h + 1) * I_DIM].astype(BF16)
    wi = tail[:, I_DIM:I_DIM + I_HEADS] * (I_HEADS ** -0.5 * I_DIM ** -0.5)

    q_pos = start + lax.broadcasted_iota(I32, (n, wide), 0)
    lane_w = lax.broadcasted_iota(I32, (n, wide), 1)
    n_wide = (start + n + wide - 1) // wide

    def tile_start(i):
        return pl.multiple_of(end - (i + 1) * wide, n)

    def idx_tile(i, carry):
        ks = tile_start(i)
        s_all = jnp.dot(qi_scr[...], kit_scr[:, pl.ds(ks, wide)], preferred_element_type=F32)
        s = jnp.zeros((n, wide), F32)
        for h in range(I_HEADS):
            s = s + jnp.maximum(s_all[h * n:(h + 1) * n, :], 0.0) * wi[:, h:h + 1]
        s = s + 0.0
        bits = pltpu.bitcast(s, I32)
        key = jnp.where(bits < 0, bits ^ jnp.int32(0x7FFFFFFF), bits)
        idx = ks - wide + lane_w
        key = jnp.where(jnp.logical_and(idx <= q_pos, idx >= 0), key, jnp.int32(INT_MIN))
        key_scr[:, pl.ds(ks, wide)] = key
        return carry

    lax.fori_loop(0, n_wide, idx_tile, 0)

    def count(pred):
        def body(i, acc):
            ks = tile_start(i)
            kk = key_scr[:, pl.ds(ks, wide)]
            hit = pred(kk, ks - wide + lane_w).astype(I32)
            for c in range(wide // n):
                acc = acc + hit[:, c * n:(c + 1) * n]
            return acc
        acc = lax.fori_loop(0, n_wide, body, jnp.zeros((n, n), I32))
        return jnp.sum(acc, axis=1, keepdims=True)

    c0 = count(lambda kk, idx: kk >= 0)
    thr = jnp.where(c0 >= top_k, jnp.int32(0), jnp.int32(INT_MIN))
    for bit in range(30, -1, -1):
        cand = thr | jnp.int32(1 << bit)
        cnt = count(lambda kk, idx: kk >= cand)
        thr = jnp.where(cnt >= top_k, cand, thr)
    need = top_k - count(lambda kk, idx: kk > thr)
    n_eq = count(lambda kk, idx: kk == thr)
    tied = jnp.logical_and(n_eq > need, thr != INT_MIN)
    cut_scr[...] = jnp.full((n, 1), seq, I32)

    @pl.when(jnp.max(tied.astype(I32)) > 0)
    def _():
        m = jnp.zeros((n, 1), I32)
        for bit in range(int(math.log2(seq)) - 1, -1, -1):
            cand = m | jnp.int32(1 << bit)
            cnt = count(lambda kk, idx: jnp.logical_and(kk == thr, idx < cand))
            m = jnp.where(cnt < need, cand, m)
        cut_scr[...] = jnp.where(tied, m, seq)

    cut = cut_scr[...]

    acc_scr[...] = jnp.zeros_like(acc_scr)
    m_scr[...] = jnp.full(m_scr.shape, NEG, F32)
    l_scr[...] = jnp.zeros_like(l_scr)

    def attn_tile(i, near):
        ks = tile_start(i)
        kk = key_scr[:, pl.ds(ks, wide)]
        sel = jnp.logical_or(kk > thr, jnp.logical_and(kk == thr, ks - wide + lane_w <= cut))
        sel = jnp.logical_and(sel, kk != INT_MIN)
        s_all = jnp.dot(q_scr[...], kt_scr[:, pl.ds(ks, wide)], preferred_element_type=F32)
        ps, alphas = [], []
        for h in range(N_HEADS):
            rows = slice(h * n, (h + 1) * n)
            s = s_all[rows, :] + (bias_ref[h] if near else far_ref[h])
            s = jnp.where(sel, s, NEG)
            m_old = m_scr[rows, :]
            m_new = jnp.maximum(m_old, jnp.max(s, axis=1, keepdims=True))
            alpha = jnp.exp(m_old - m_new)
            p = jnp.where(sel, jnp.exp(s - m_new), 0.0)
            l_scr[rows, :] = alpha * l_scr[rows, :] + jnp.sum(p, axis=1, keepdims=True)
            m_scr[rows, :] = m_new
            ps.append(p.astype(BF16))
            alphas.append(alpha)
        pv = jnp.dot(jnp.concatenate(ps, axis=0), kv_scr[pl.ds(ks, wide), :], preferred_element_type=F32)
        acc_scr[...] = jnp.concatenate(alphas, axis=0) * acc_scr[...] + pv

    def far_tile(i, carry):
        attn_tile(i, False)
        return carry

    attn_tile(0, True)
    lax.fori_loop(1, n_wide, far_tile, 0)
    o_ref[0] = jnp.concatenate(
        [acc_scr[h * n:(h + 1) * n, HEAD:] / l_scr[h * n:(h + 1) * n, :] for h in range(N_HEADS)],
        axis=1).astype(o_ref.dtype)


def _t5_bucket(dist):
    n = jnp.maximum(dist, 0)
    max_exact = REL_BUCKETS // 2
    log_ratio = jnp.log(jnp.maximum(n, 1).astype(F32) / max_exact) / math.log(REL_MAX_DIST / max_exact)
    large = jnp.minimum(max_exact + (log_ratio * (REL_BUCKETS - max_exact)).astype(I32), REL_BUCKETS - 1)
    return jnp.where(n < max_exact, n, large)


def _dsa(u_c, kv_norm_w, w_uk, w_uv, rel_bias):
    b, l, _ = u_c.shape
    top_k = min(TOPK_MAX, l // 4)
    assert DSA_WIDE - Q_BLOCK + 1 >= REL_MAX_DIST and l % DSA_WIDE == 0
    dist = (jnp.arange(Q_BLOCK, dtype=I32)[:, None] + (DSA_WIDE - Q_BLOCK)) - jnp.arange(DSA_WIDE, dtype=I32)[None, :]
    onehot = _t5_bucket(dist)[None] == jnp.arange(REL_BUCKETS, dtype=I32)[:, None, None]
    table = rel_bias.astype(F32)
    bias = jnp.stack([jnp.sum(jnp.where(onehot, table[:, h][:, None, None], 0.0), axis=0) for h in range(N_HEADS)])
    w_kv = jnp.concatenate([w_uk, w_uv], axis=1).astype(F32)
    kern = functools.partial(_dsa_kernel, top_k=top_k)
    lp = l + DSA_WIDE
    return pl.pallas_call(
        kern,
        grid=(b, l // Q_BLOCK),
        in_specs=[pl.BlockSpec((1, Q_BLOCK, C_COLS), lambda i, j: (i, j, 0)),
                  _full((1, LANES)), _full((LANES, LANES)), _full((N_HEADS, Q_BLOCK, DSA_WIDE)),
                  pl.BlockSpec(memory_space=pltpu.SMEM)],
        out_specs=pl.BlockSpec((1, Q_BLOCK, W_GROUP), lambda i, j: (i, j, 0)),
        out_shape=jax.ShapeDtypeStruct((b, l, W_GROUP), BF16),
        scratch_shapes=[pltpu.VMEM((HEAD, lp), BF16), pltpu.VMEM((lp, LANES), BF16), pltpu.VMEM((I_DIM, lp), BF16),
                        pltpu.VMEM((Q_BLOCK, lp), I32),
                        pltpu.VMEM((N_HEADS * Q_BLOCK, HEAD), BF16), pltpu.VMEM((I_HEADS * Q_BLOCK, I_DIM), BF16),
                        pltpu.VMEM((N_HEADS * Q_BLOCK, LANES), F32),
                        pltpu.VMEM((N_HEADS * Q_BLOCK, 1), F32), pltpu.VMEM((N_HEADS * Q_BLOCK, 1), F32),
                        pltpu.VMEM((Q_BLOCK, 1), I32)],
        compiler_params=_params("parallel", "arbitrary"),
        name="dsa",
    )(u_c, kv_norm_w.astype(F32)[None, :], w_kv, bias, table[REL_BUCKETS - 1])


def _s5_kernel(u_ref, are_ref, aim_ref, ldt_ref, bre_ref, bim_ref, cre_ref, cim_ref, dsk_ref, gw_ref, gb_ref,
               o_ref, pre, pim, bdr, bdi, hre, him, carry):
    first = jnp.logical_and(pl.program_id(0) == 0, pl.program_id(1) == 0)
    n = S_BLOCK
    pad = LANES

    @pl.when(first)
    def _():
        lam_re = jnp.minimum(are_ref[...], -1e-4)
        lam_im = aim_ref[...]
        dt = jnp.exp(ldt_ref[...])
        mag = jnp.exp(dt * lam_re)
        ab_re = mag * jnp.cos(dt * lam_im)
        ab_im = mag * jnp.sin(dt * lam_im)
        den = lam_re * lam_re + lam_im * lam_im
        f_re = ((ab_re - 1.0) * lam_re + ab_im * lam_im) / den
        f_im = (ab_im * lam_re - (ab_re - 1.0) * lam_im) / den
        b_re = bre_ref[...]
        b_im = bim_ref[...]
        bb_re = f_re * b_re - f_im * b_im
        bb_im = f_re * b_im + f_im * b_re
        rg = lax.broadcasted_iota(I32, (W_GROUP, S_LANES), 0) // S_GROUP_CH
        cg = lax.broadcasted_iota(I32, (W_GROUP, S_LANES), 1) // S_STATE
        same = rg == cg
        bdr[...] = jnp.where(same, jnp.concatenate([bb_re] * S_GROUPS, axis=0), 0.0)
        bdi[...] = jnp.where(same, jnp.concatenate([bb_im] * S_GROUPS, axis=0), 0.0)
        pre[0:1, :] = ab_re
        pim[0:1, :] = ab_im
        d = 1
        while d < n:
            sr = pre[d - 1:d, :]
            si = pim[d - 1:d, :]
            xr = pre[0:d, :]
            xi = pim[0:d, :]
            pre[d:2 * d, :] = xr * sr - xi * si
            pim[d:2 * d, :] = xr * si + xi * sr
            d *= 2
        hre[...] = jnp.zeros_like(hre)
        him[...] = jnp.zeros_like(him)

    @pl.when(pl.program_id(1) == 0)
    def _():
        carry[...] = jnp.zeros_like(carry)

    u = u_ref[0]
    hre[0, pad:pad + n, :] = _mm(u, bdr[...])
    him[0, pad:pad + n, :] = _mm(u, bdi[...])

    def col_block(cb, c_):
        cs = pl.ds(pl.multiple_of(cb * LANES, LANES), LANES)
        src = 0
        d = 1
        while d < n:
            ar = pre[d - 1:d, cs]
            ai = pim[d - 1:d, cs]
            xr = hre[src, pl.ds(pad - d, n), cs]
            xi = him[src, pl.ds(pad - d, n), cs]
            hre[1 - src, pad:pad + n, cs] = hre[src, pad:pad + n, cs] + ar * xr - ai * xi
            him[1 - src, pad:pad + n, cs] = him[src, pad:pad + n, cs] + ar * xi + ai * xr
            src = 1 - src
            d *= 2
        assert src == 0
        return c_

    lax.fori_loop(0, S_LANES // LANES, col_block, 0)

    c_re = carry[0:1, :]
    c_im = carry[1:2, :]
    p_re = pre[...]
    p_im = pim[...]
    h_re = hre[0, pad:pad + n, :] + p_re * c_re - p_im * c_im
    h_im = him[0, pad:pad + n, :] + p_re * c_im + p_im * c_re
    carry[0:1, :] = h_re[n - 1:n, :]
    carry[1:2, :] = h_im[n - 1:n, :]
    y = _mm(h_re, cre_ref[...]) - _mm(h_im, cim_ref[...]) + dsk_ref[...] * u
    y = jax.nn.gelu(y)
    gl = _mm(y, gw_ref[...]) + gb_ref[...]
    o_ref[0] = (gl[:, 0:W_GROUP] * jax.nn.sigmoid(gl[:, W_GROUP:])).astype(o_ref.dtype)


def _s5(u_s, a_re, a_im, b_re, b_im, c_re, c_im, d_skip, log_dt, glu_w, glu_b):
    b, l, _ = u_s.shape
    f = lambda t: t.astype(F32).reshape(1, -1)
    eye = jnp.eye(S_GROUPS, dtype=F32)
    bt = lambda t: jnp.transpose(t.astype(F32), (2, 0, 1)).reshape(S_GROUP_CH, S_LANES)
    cbd = lambda t: jnp.einsum('gpn,gh->gnhp', t.astype(F32), eye).reshape(S_LANES, W_GROUP)
    ldt = jnp.repeat(log_dt.astype(F32), S_STATE)[None, :]
    n_log = int(math.log2(S_BLOCK))
    assert n_log % 2 == 0
    return pl.pallas_call(
        _s5_kernel,
        grid=(b, l // S_BLOCK),
        in_specs=[pl.BlockSpec((1, S_BLOCK, S_COLS), lambda i, j: (i, j, 0)),
                  _full((1, S_LANES)), _full((1, S_LANES)), _full((1, S_LANES)),
                  _full((S_GROUP_CH, S_LANES)), _full((S_GROUP_CH, S_LANES)),
                  _full((S_LANES, W_GROUP)), _full((S_LANES, W_GROUP)),
                  _full((1, W_GROUP)), _full((W_GROUP, 2 * W_GROUP)), _full((1, 2 * W_GROUP))],
        out_specs=pl.BlockSpec((1, S_BLOCK, W_GROUP), lambda i, j: (i, j, 0)),
        out_shape=jax.ShapeDtypeStruct((b, l, W_GROUP), BF16),
        scratch_shapes=[pltpu.VMEM((S_BLOCK, S_LANES), F32), pltpu.VMEM((S_BLOCK, S_LANES), F32),
                        pltpu.VMEM((W_GROUP, S_LANES), F32), pltpu.VMEM((W_GROUP, S_LANES), F32),
                        pltpu.VMEM((2, S_BLOCK + LANES, S_LANES), F32), pltpu.VMEM((2, S_BLOCK + LANES, S_LANES), F32),
                        pltpu.VMEM((8, S_LANES), F32)],
        compiler_params=_params("arbitrary", "arbitrary"),
        name="s5",
    )(u_s, f(a_re), f(a_im), ldt, bt(b_re), bt(b_im), cbd(c_re), cbd(c_im), f(d_skip),
      glu_w.astype(F32), f(glu_b))


def _out_proj_kernel(x_ref, ym_ref, yr_ref, yc_ref, ys_ref, w_ref, o_ref):
    acc = x_ref[...]
    for i, y_ref in enumerate((ym_ref, yr_ref, yc_ref, ys_ref)):
        acc = acc + jnp.dot(y_ref[...], w_ref[i * W_GROUP:(i + 1) * W_GROUP, :], preferred_element_type=F32)
    o_ref[...] = acc


def _out_proj(x2, ys, w_out, tm=512):
    t, d = x2.shape
    yspec = pl.BlockSpec((tm, W_GROUP), lambda i: (i, 0))
    return pl.pallas_call(
        _out_proj_kernel,
        grid=(t // tm,),
        in_specs=[pl.BlockSpec((tm, d), lambda i: (i, 0)), yspec, yspec, yspec, yspec, _full(w_out.shape)],
        out_specs=pl.BlockSpec((tm, d), lambda i: (i, 0)),
        out_shape=jax.ShapeDtypeStruct((t, d), F32),
        compiler_params=_params("parallel"),
        name="out_proj",
    )(x2, *ys, w_out)


def _moe_kernel(x_ref, nw_ref, wr_ref, br_ref, wg_ref, wu_ref, wd_ref, fw_ref, o_ref, t_scr, cw_scr, acc_scr,
                *, final_norm):
    e = pl.program_id(1)
    tm = x_ref.shape[0]
    lane = lax.broadcasted_iota(I32, (tm, LANES), 1)

    @pl.when(e == 0)
    def _():
        x = x_ref[...]
        t = x * lax.rsqrt(jnp.mean(x * x, axis=-1, keepdims=True) + NORM_EPS) * nw_ref[...]
        t_scr[...] = t.astype(BF16)
        logits = _mm(t, wr_ref[...], precise=True) + br_ref[...]
        big = jnp.int32(LANES)
        is_g = lane < E_GROUPS
        gl = jnp.where(is_g, logits, -jnp.inf)
        gmax = jnp.max(gl, axis=1, keepdims=True)
        g_sel = jnp.min(jnp.where(jnp.logical_and(is_g, gl == gmax), lane, big), axis=1, keepdims=True)
        g_gate = 1.0 / jnp.sum(jnp.exp(gl - gmax), axis=1, keepdims=True)
        lo = E_GROUPS + g_sel * E_PER_GROUP
        in_g = jnp.logical_and(lane >= lo, lane < lo + E_PER_GROUP)
        el = jnp.where(in_g, logits, -jnp.inf)
        m1 = jnp.max(el, axis=1, keepdims=True)
        i1 = jnp.min(jnp.where(el == m1, lane, big), axis=1, keepdims=True)
        el2 = jnp.where(lane == i1, -jnp.inf, el)
        m2 = jnp.max(el2, axis=1, keepdims=True)
        i2 = jnp.min(jnp.where(el2 == m2, lane, big), axis=1, keepdims=True)
        e2 = jnp.exp(m2 - m1)
        w1 = 1.0 / (1.0 + e2)
        w2 = e2 / (1.0 + e2)
        cw_scr[...] = jnp.where(lane == i1, w1, jnp.where(lane == i2, w2, 0.0)) * g_gate
        acc_scr[...] = jnp.zeros_like(acc_scr)

    t = t_scr[...]
    cw_e = jnp.sum(jnp.where(lane == e + E_GROUPS, cw_scr[...], 0.0), axis=1, keepdims=True)
    hid = _silu(jnp.dot(t, wg_ref[0], preferred_element_type=F32)) * jnp.dot(t, wu_ref[0], preferred_element_type=F32)
    hid = (hid * cw_e).astype(BF16)
    acc_scr[...] += jnp.dot(hid, wd_ref[0], preferred_element_type=F32)

    @pl.when(e == pl.num_programs(1) - 1)
    def _():
        y = x_ref[...] + acc_scr[...]
        if final_norm:
            y = y * lax.rsqrt(jnp.mean(y * y, axis=-1, keepdims=True) + NORM_EPS) * fw_ref[...]
        o_ref[...] = y


def _moe(x2, norm_w, wr_g, br_g, wr_e, br_e, w_gate, w_up, w_down, final_w, final_norm, tm=1024):
    t, d = x2.shape
    wr = jnp.zeros((d, LANES), F32).at[:, 0:E_GROUPS].set(wr_g.astype(F32))
    wr = wr.at[:, E_GROUPS:E_GROUPS + N_EXPERTS].set(wr_e.astype(F32))
    br = jnp.zeros((1, LANES), F32).at[0, 0:E_GROUPS].set(br_g.astype(F32))
    br = br.at[0, E_GROUPS:E_GROUPS + N_EXPERTS].set(br_e.astype(F32))
    wg = w_gate.reshape(N_EXPERTS, d, D_EXPERT).astype(BF16)
    wu = w_up.reshape(N_EXPERTS, d, D_EXPERT).astype(BF16)
    wd = w_down.reshape(N_EXPERTS, D_EXPERT, d).astype(BF16)
    kern = functools.partial(_moe_kernel, final_norm=final_norm)
    return pl.pallas_call(
        kern,
        grid=(t // tm, N_EXPERTS),
        in_specs=[pl.BlockSpec((tm, d), lambda i, e: (i, 0)), _full((1, d)), _full((d, LANES)), _full((1, LANES)),
                  pl.BlockSpec((1, d, D_EXPERT), lambda i, e: (e, 0, 0)),
                  pl.BlockSpec((1, d, D_EXPERT), lambda i, e: (e, 0, 0)),
                  pl.BlockSpec((1, D_EXPERT, d), lambda i, e: (e, 0, 0)), _full((1, d))],
        out_specs=pl.BlockSpec((tm, d), lambda i, e: (i, 0)),
        out_shape=jax.ShapeDtypeStruct((t, d), F32),
        scratch_shapes=[pltpu.VMEM((tm, d), BF16), pltpu.VMEM((tm, LANES), F32), pltpu.VMEM((tm, d), F32)],
        compiler_params=_params("parallel", "arbitrary"),
        name="moe",
    )(x2, norm_w.astype(F32)[None, :], wr, br, wg, wu, wd, final_w.astype(F32)[None, :])


def _pad_in_proj(w):
    d = w.shape[0]
    m_in, r_in = 772, 896
    c_used = 680
    z = lambda k: jnp.zeros((d, k), w.dtype)
    wm = w[:, 0:m_in]
    wr = w[:, m_in:m_in + r_in]
    wc = w[:, m_in + r_in:m_in + r_in + c_used]
    ws = w[:, m_in + r_in + c_used:]
    return jnp.concatenate([wm, z(M_COLS - m_in), wr, wc, z(C_COLS - c_used), ws], axis=1).astype(BF16)


def kernel(x, norm1_w, w_in, m_conv_w, m_conv_b, m_dt_bias, m_a_log, m_d, m_norm_w, r_mu, r_w0, r_w2, r_a0, r_a2, r_g2, r_k_k, r_k_a, r_r_k, r_ln_w, r_ln_b, c_kv_norm_w, c_w_uk, c_w_uv, rel_bias, s_a_re, s_a_im, s_b_re, s_b_im, s_c_re, s_c_im, s_d, s_log_dt, s_glu_w, s_glu_b, w_out, norm2_w, moe_wr_group, moe_br_group, moe_wr_exp, moe_br_exp, moe_w_gate, moe_w_up, moe_w_down, final_norm_w):
    bsz, seq, d = x.shape
    depth = w_in.shape[0]
    x2 = x.astype(F32).reshape(bsz * seq, d)
    for i in range(depth):
        u_m, u_r, u_c, u_s = _in_proj(x2, norm1_w[i].astype(F32)[None, :], _pad_in_proj(w_in[i]))
        sh = lambda t: t.reshape(bsz, seq, t.shape[-1])
        y_m = _mamba(sh(u_m), m_conv_w[i], m_conv_b[i], m_dt_bias[i], m_a_log[i], m_d[i], m_norm_w[i])
        y_r = _rwkv(sh(u_r), r_mu[i], r_w0[i], r_w2[i], r_a0[i], r_a2[i], r_g2[i],
                    r_k_k[i], r_k_a[i], r_r_k[i], r_ln_w[i], r_ln_b[i])
        y_c = _dsa(sh(u_c), c_kv_norm_w[i], c_w_uk[i], c_w_uv[i], rel_bias)
        y_s = _s5(sh(u_s), s_a_re[i], s_a_im[i], s_b_re[i], s_b_im[i], s_c_re[i], s_c_im[i],
                  s_d[i], s_log_dt[i], s_glu_w[i], s_glu_b[i])
        ys = [t.reshape(bsz * seq, W_GROUP) for t in (y_m, y_r, y_c, y_s)]
        x2 = _out_proj(x2, ys, w_out[i].astype(BF16))
        x2 = _moe(x2, norm2_w[i], moe_wr_group[i], moe_br_group[i], moe_wr_exp[i], moe_br_exp[i],
                  moe_w_gate[i], moe_w_up[i], moe_w_down[i], final_norm_w, final_norm=(i == depth - 1))
    return x2.reshape(bsz, seq, d).astype(x.dtype)
```

```python
import functools
import math

import jax
import jax.numpy as jnp
from jax import lax
from jax.experimental import pallas as pl
from jax.experimental.pallas import tpu as pltpu

F32 = jnp.float32
BF16 = jnp.bfloat16
I32 = jnp.int32
HIGHEST = lax.Precision.HIGHEST

LANES = 128
VMEM_LIMIT = 56 * 1024 * 1024

NORM_EPS = 1e-6
W_GROUP = 256
HEAD = 64
N_HEADS = W_GROUP // HEAD
M_CONV_CH = 512
M_CONV = 4
M_COLS = 896
R_COLS = 896
C_COLS = 768
S_COLS = 256
SSD_CHUNK = 128
R_CHUNK = 64
R_BLOCK = 128
R_GN_EPS = 64e-5
I_HEADS = 8
I_DIM = 32
Q_BLOCK = 128
DSA_WIDE = 512
DSA_AUX = 4
TOPK_MAX = 256
REL_BUCKETS = 32
REL_MAX_DIST = 128
S_GROUPS = 16
S_GROUP_CH = 16
S_STATE = 64
S_LANES = S_GROUPS * S_STATE
S_BLOCK = 256
E_GROUPS = 4
E_PER_GROUP = 8
N_EXPERTS = E_GROUPS * E_PER_GROUP
D_EXPERT = 256
INT_MIN = -2 ** 31
NEG = -1e30


def _mm(a, b, precise=False):
    if precise:
        return jnp.dot(a.astype(F32), b.astype(F32), precision=HIGHEST, preferred_element_type=F32)
    return jnp.dot(a.astype(BF16), b.astype(BF16), preferred_element_type=F32)


def _mm_nt(a, b, precise=False):
    dn = (((1,), (1,)), ((), ()))
    if precise:
        return lax.dot_general(a.astype(F32), b.astype(F32), dn, precision=HIGHEST, preferred_element_type=F32)
    return lax.dot_general(a.astype(BF16), b.astype(BF16), dn, preferred_element_type=F32)


def _mm_tn(a, b, precise=False):
    dn = (((0,), (0,)), ((), ()))
    if precise:
        return lax.dot_general(a.astype(F32), b.astype(F32), dn, precision=HIGHEST, preferred_element_type=F32)
    return lax.dot_general(a.astype(BF16), b.astype(BF16), dn, preferred_element_type=F32)


def _silu(x):
    return x * jax.nn.sigmoid(x)


def _softplus(x):
    return jnp.maximum(x, 0.0) + jnp.log(1.0 + jnp.exp(-jnp.abs(x)))


def _params(*sem):
    return pltpu.CompilerParams(dimension_semantics=sem, vmem_limit_bytes=VMEM_LIMIT)


def _full(shape):
    nd = len(shape)
    return pl.BlockSpec(shape, lambda *_: (0,) * nd)


def _in_proj_kernel(x_ref, nw_ref, w_ref, om_ref, or_ref, oc_ref, os_ref):
    x = x_ref[...]
    h = x * lax.rsqrt(jnp.mean(x * x, axis=-1, keepdims=True) + NORM_EPS) * nw_ref[...]
    h = h.astype(BF16)
    off = 0
    for o_ref, width in ((om_ref, M_COLS), (or_ref, R_COLS), (oc_ref, C_COLS), (os_ref, S_COLS)):
        o_ref[...] = jnp.dot(h, w_ref[:, off:off + width], preferred_element_type=F32)
        off += width


def _in_proj(x2, norm_w, w_pad, tm=512):
    t, d = x2.shape
    n = w_pad.shape[1]
    return pl.pallas_call(
        _in_proj_kernel,
        grid=(t // tm,),
        in_specs=[pl.BlockSpec((tm, d), lambda i: (i, 0)), _full((1, d)), _full((d, n))],
        out_specs=[pl.BlockSpec((tm, c), lambda i: (i, 0)) for c in (M_COLS, R_COLS, C_COLS, S_COLS)],
        out_shape=[jax.ShapeDtypeStruct((t, c), F32) for c in (M_COLS, R_COLS, C_COLS, S_COLS)],
        compiler_params=_params("parallel"),
        name="in_proj",
    )(x2, norm_w, w_pad)


def _mamba_kernel(u_ref, cw_ref, cb_ref, dtb_ref, alog_ref, dsk_ref, nw_ref, o_ref, xbuf, state):
    c = pl.program_id(1)
    n = SSD_CHUNK

    @pl.when(c == 0)
    def _():
        xbuf[0:8, :] = jnp.zeros((8, M_CONV_CH), F32)
        state[...] = jnp.zeros_like(state)

    u = u_ref[0]
    z = u[:, 0:W_GROUP]
    dtr = u[:, W_GROUP + M_CONV_CH:M_COLS]
    xbuf[8:8 + n, :] = u[:, W_GROUP:W_GROUP + M_CONV_CH]
    acc = jnp.broadcast_to(cb_ref[...], (n, M_CONV_CH))
    for j in range(M_CONV):
        acc = acc + cw_ref[j:j + 1, :] * xbuf[pl.ds(8 - (M_CONV - 1) + j, n), :]
    xbuf[0:8, :] = xbuf[n:n + 8, :]
    xc = _silu(acc)
    xs = xc[:, 0:W_GROUP]
    bm = xc[:, W_GROUP:W_GROUP + 2 * HEAD]
    cm = xc[:, W_GROUP + 2 * HEAD:]

    dt = _softplus(dtr + dtb_ref[...])
    a = dt * (-jnp.exp(alog_ref[...]))
    row = lax.broadcasted_iota(I32, (n, n), 0)
    col = lax.broadcasted_iota(I32, (n, n), 1)
    causal = row >= col
    a_cum = _mm(causal.astype(F32), a, precise=True)
    a_cum_t = a_cum.T
    a_last = a_cum[n - 1:n, :]

    ys = []
    for h in range(N_HEADS):
        g = h // 2
        b_h = bm[:, g * HEAD:(g + 1) * HEAD]
        c_h = cm[:, g * HEAD:(g + 1) * HEAD]
        x_h = xs[:, h * HEAD:(h + 1) * HEAD]
        xdt = x_h * dt[:, h:h + 1]
        ac = a_cum[:, h:h + 1]
        dec = jnp.exp(jnp.where(causal, ac - a_cum_t[h:h + 1, :], -jnp.inf))
        scores = _mm_nt(c_h, b_h) * dec
        y = _mm(scores, xdt)
        st = state[h]
        y = y + _mm_nt(c_h, st) * jnp.exp(ac)
        al = a_last[:, h:h + 1]
        state[h] = st * jnp.exp(al) + _mm_tn(xdt, b_h * jnp.exp(al - ac))
        ys.append(y + x_h * dsk_ref[:, h * HEAD:(h + 1) * HEAD])
    y = jnp.concatenate(ys, axis=1) * _silu(z)
    y = y * lax.rsqrt(jnp.mean(y * y, axis=-1, keepdims=True) + NORM_EPS) * nw_ref[...]
    o_ref[0] = y.astype(o_ref.dtype)


def _mamba(u_m, conv_w, conv_b, dt_bias, a_log, d_skip, norm_w):
    b, l, _ = u_m.shape
    pad = lambda v: jnp.pad(v.astype(F32), (0, LANES - v.shape[0]))[None, :]
    return pl.pallas_call(
        _mamba_kernel,
        grid=(b, l // SSD_CHUNK),
        in_specs=[pl.BlockSpec((1, SSD_CHUNK, M_COLS), lambda i, j: (i, j, 0)),
                  _full((M_CONV, M_CONV_CH)), _full((1, M_CONV_CH)), _full((1, LANES)), _full((1, LANES)),
                  _full((1, W_GROUP)), _full((1, W_GROUP))],
        out_specs=pl.BlockSpec((1, SSD_CHUNK, W_GROUP), lambda i, j: (i, j, 0)),
        out_shape=jax.ShapeDtypeStruct((b, l, W_GROUP), BF16),
        scratch_shapes=[pltpu.VMEM((SSD_CHUNK + 8, M_CONV_CH), F32), pltpu.VMEM((N_HEADS, HEAD, HEAD), F32)],
        compiler_params=_params("parallel", "arbitrary"),
        name="mamba_ssd",
    )(u_m, conv_w.astype(F32), conv_b.astype(F32)[None, :], pad(dt_bias), pad(a_log),
      jnp.repeat(d_skip.astype(F32), HEAD)[None, :], norm_w.astype(F32)[None, :])


def _rwkv_kernel(u_ref, mu_ref, wl_ref, w0_ref, a0_ref, kk_ref, ka_ref, rk_ref, lnw_ref, lnb_ref, seg_ref,
                 o_ref, ubuf, state):
    c = pl.program_id(1)
    n = R_BLOCK
    cs = R_CHUNK

    @pl.when(c == 0)
    def _():
        ubuf[0:8, :] = jnp.zeros((8, R_COLS), F32)
        state[...] = jnp.zeros_like(state)

    u = u_ref[0]
    ubuf[8:8 + n, :] = u
    u = u + mu_ref[...] * (ubuf[pl.ds(7, n), :] - u)
    ubuf[0:8, :] = ubuf[n:n + 8, :]
    r = u[:, 0:W_GROUP]
    k = u[:, W_GROUP:2 * W_GROUP]
    v = u[:, 2 * W_GROUP:3 * W_GROUP]
    lo = u[:, 3 * W_GROUP:]
    lane = lax.broadcasted_iota(I32, lo.shape, 1)
    lo = jnp.where(lane < 32, jnp.tanh(lo), jnp.where(lane < 64, lo, jax.nn.sigmoid(lo)))
    proj = _mm(lo, wl_ref[...], precise=True)
    w_log = -_softplus(-(w0_ref[...] + proj[:, 0:W_GROUP])) - 0.5
    ld = -jnp.exp(w_log)
    a_lr = jax.nn.sigmoid(a0_ref[...] + proj[:, W_GROUP:2 * W_GROUP])
    g = proj[:, 2 * W_GROUP:]
    seg = seg_ref[...]
    kk = k * kk_ref[...]
    kk = kk * lax.rsqrt(jnp.maximum(_mm(kk * kk, seg, precise=True), 1e-24))
    k = k * (1.0 + (a_lr - 1.0) * ka_ref[...])
    a_v = -kk
    b_v = kk * a_lr

    row = lax.broadcasted_iota(I32, (n, n), 0)
    col = lax.broadcasted_iota(I32, (n, n), 1)
    tri2 = ((row >= col) & ((row // cs) == (col // cs))).astype(F32)
    lw = _mm(tri2, ld, precise=True)
    w_inc = jnp.exp(lw)
    w_inv = jnp.exp(-lw)
    rt = r * w_inc
    at = a_v * jnp.exp(lw - ld)
    bt = b_v * w_inv
    kt = k * w_inv

    wg = W_GROUP
    r2 = lax.broadcasted_iota(I32, (wg, wg), 0)
    c2 = lax.broadcasted_iota(I32, (wg, wg), 1)
    lower = (r2 % cs) >= (c2 % cs)
    strict = (r2 % cs) > (c2 % cs)
    eye = (r2 == c2).astype(F32)
    lane_head = lax.broadcasted_iota(I32, (cs, wg), 1) // HEAD

    def blocks(xc, dtype=BF16):
        return jnp.concatenate([jnp.where(lane_head == h, xc, 0.0) for h in range(N_HEADS)], axis=0).astype(dtype)

    s_cur = state[...]
    outs = []
    for ci in range(n // cs):
        rows = slice(ci * cs, (ci + 1) * cs)
        lw_end = lw[ci * cs + cs - 1:ci * cs + cs, :]
        ratio = jnp.exp(lw_end - lw[rows, :])
        at_b, rt_b, bt_b, kt_b = blocks(at[rows, :]), blocks(rt[rows, :], F32), blocks(bt[rows, :]), blocks(kt[rows, :])
        v_b = blocks(v[rows, :])
        bh = blocks(b_v[rows, :] * ratio)
        kh = blocks(k[rows, :] * ratio)
        a_ab = jnp.where(strict, _mm_nt(at_b, bt_b), 0.0)
        a_ak = jnp.where(strict, _mm_nt(at_b, kt_b), 0.0)
        a_rb = jnp.where(lower, _mm_nt(rt_b, bt_b), 0.0)
        a_rk = jnp.where(lower, _mm_nt(rt_b, kt_b), 0.0)
        tinv = eye + a_ab
        pw = a_ab
        for _ in range(int(math.log2(cs)) - 1):
            pw = _mm(pw, pw)
            tinv = tinv + _mm(pw, tinv)
        ta = _mm(tinv, at_b)
        pv = _mm(tinv, _mm(a_ak, v_b))
        q_eff = rt_b + _mm(a_rb, ta)
        o_v = _mm(a_rb, pv) + _mm(a_rk, v_b)
        m_eff = eye * jnp.exp(lw_end) + _mm_tn(ta, bh)
        s_v = _mm_tn(pv, bh) + _mm_tn(v_b, kh)
        o_b = _mm_nt(q_eff, s_cur) + o_v
        outs.append(o_b[0:cs] + o_b[cs:2 * cs] + o_b[2 * cs:3 * cs] + o_b[3 * cs:])
        s_cur = _mm(s_cur, m_eff, precise=True) + s_v
    state[...] = s_cur
    o = jnp.concatenate(outs, axis=0)

    inv = 1.0 / HEAD
    mean = _mm(o, seg, precise=True) * inv
    d = o - mean
    var = _mm(d * d, seg, precise=True) * inv
    o = d * lax.rsqrt(var + R_GN_EPS) * lnw_ref[...] + lnb_ref[...]
    o = o + _mm(r * k * rk_ref[...], seg, precise=True) * v
    o_ref[0] = (o * g).astype(o_ref.dtype)


def _rwkv(u_r, mu, w0, w2, a0, a2, g2, k_k, k_a, r_k, ln_w, ln_b):
    b, l, _ = u_r.shape
    f = lambda t: t.astype(F32).reshape(1, -1)
    w_lora = jnp.zeros((LANES, 3 * W_GROUP), F32)
    w_lora = w_lora.at[0:32, 0:W_GROUP].set(w2.astype(F32))
    w_lora = w_lora.at[32:64, W_GROUP:2 * W_GROUP].set(a2.astype(F32))
    w_lora = w_lora.at[64:128, 2 * W_GROUP:].set(g2.astype(F32))
    hid = jnp.arange(W_GROUP) // HEAD
    seg = (hid[:, None] == hid[None, :]).astype(F32)
    vec = _full((1, W_GROUP))
    return pl.pallas_call(
        _rwkv_kernel,
        grid=(b, l // R_BLOCK),
        in_specs=[pl.BlockSpec((1, R_BLOCK, R_COLS), lambda i, j: (i, j, 0)),
                  _full((1, R_COLS)), _full((LANES, 3 * W_GROUP)),
                  vec, vec, vec, vec, vec, vec, vec, _full((W_GROUP, W_GROUP))],
        out_specs=pl.BlockSpec((1, R_BLOCK, W_GROUP), lambda i, j: (i, j, 0)),
        out_shape=jax.ShapeDtypeStruct((b, l, W_GROUP), BF16),
        scratch_shapes=[pltpu.VMEM((R_BLOCK + 8, R_COLS), F32), pltpu.VMEM((W_GROUP, W_GROUP), F32)],
        compiler_params=_params("parallel", "arbitrary"),
        name="rwkv7",
    )(u_r, f(mu), w_lora, f(w0), f(a0), f(k_k), f(k_a), f(r_k), f(ln_w), f(ln_b), seg)


def _dsa_kernel(u_ref, kvw_ref, wkv_ref, bias_ref, far_ref, o_ref,
                kt_scr, kv_scr, tail_scr, key_scr, mask_scr, acc_scr, mx_scr, l_scr, cut_scr, *, top_k):
    qb = pl.program_id(1)
    n = Q_BLOCK
    wide = DSA_WIDE
    seq = key_scr.shape[0] - wide

    @pl.when(qb == 0)
    def _():
        kt_scr[...] = jnp.zeros_like(kt_scr)
        kv_scr[...] = jnp.zeros_like(kv_scr)
        tail_scr[...] = jnp.zeros_like(tail_scr)
        key_scr[...] = jnp.full(key_scr.shape, INT_MIN, I32)
        mask_scr[...] = jnp.full(mask_scr.shape, NEG, F32)

    u = u_ref[0]
    q = u[:, 0:W_GROUP]
    ckv = u[:, W_GROUP:W_GROUP + LANES]
    qi = u[:, W_GROUP + LANES:2 * W_GROUP + LANES]
    tail = u[:, 2 * W_GROUP + LANES:]
    ckv = ckv * lax.rsqrt(jnp.mean(ckv * ckv, axis=-1, keepdims=True) + NORM_EPS) * kvw_ref[...]
    kv = _mm(ckv, wkv_ref[...])
    start = pl.multiple_of(qb * n, n)
    end = pl.multiple_of(start + wide + n, n)
    sub = lax.broadcasted_iota(I32, (HEAD, n), 0)
    kt_scr[:, pl.ds(end - n, n)] = jnp.concatenate(
        [kv.T[0:HEAD, :], jnp.where(sub < DSA_AUX, 1.0, 0.0)], axis=0).astype(BF16)
    kv_scr[pl.ds(end - n, n), :] = kv.astype(BF16)
    tail_scr[pl.ds(end - n, n), :] = tail.astype(BF16)
    qi_t = qi.T
    w_qi = jnp.concatenate(
        [jnp.concatenate([qi_t[h * I_DIM:(h + 1) * I_DIM, :] for h in range(I_HEADS)], axis=1),
         jnp.zeros((LANES - I_DIM, I_HEADS * n), F32)], axis=0).astype(BF16)
    wi_t = tail.T[I_DIM:I_DIM + I_HEADS, :] * (I_HEADS ** -0.5 * I_DIM ** -0.5)

    q_pos = start + lax.broadcasted_iota(I32, (wide, n), 1)
    row_w = lax.broadcasted_iota(I32, (wide, n), 0)
    n_wide = (start + n + wide - 1) // wide

    def tile_start(i):
        return pl.multiple_of(end - (i + 1) * wide, n)

    def idx_tile(i, carry):
        ks = tile_start(i)
        s_all = jnp.dot(tail_scr[pl.ds(ks, wide), :], w_qi, preferred_element_type=F32)
        s = jnp.zeros((wide, n), F32)
        for h in range(I_HEADS):
            s = s + jnp.maximum(s_all[:, h * n:(h + 1) * n], 0.0) * wi_t[h:h + 1, :]
        s = s + 0.0
        bits = pltpu.bitcast(s, I32)
        key = jnp.where(bits < 0, bits ^ jnp.int32(0x7FFFFFFF), bits)
        idx = ks - wide + row_w
        key = jnp.where(jnp.logical_and(idx <= q_pos, idx >= 0), key, jnp.int32(INT_MIN))
        key_scr[pl.ds(ks, wide), :] = key
        return carry

    lax.fori_loop(0, n_wide, idx_tile, 0)

    part = 64

    def count(pred):
        def body(i, acc):
            ks = tile_start(i)
            kk = key_scr[pl.ds(ks, wide), :]
            hit = pred(kk, ks - wide + row_w).astype(I32)
            for c in range(wide // part):
                acc = acc + hit[c * part:(c + 1) * part, :]
            return acc
        acc = lax.fori_loop(0, n_wide, body, jnp.zeros((part, n), I32))
        return jnp.sum(acc, axis=0, keepdims=True)

    c0 = count(lambda kk, idx: kk >= 0)
    thr = jnp.where(c0 >= top_k, jnp.int32(0), jnp.int32(INT_MIN))
    for bit in range(30, -1, -1):
        cand = thr | jnp.int32(1 << bit)
        cnt = count(lambda kk, idx: kk >= cand)
        thr = jnp.where(cnt >= top_k, cand, thr)
    need = top_k - count(lambda kk, idx: kk > thr)
    n_eq = count(lambda kk, idx: kk == thr)
    tied = jnp.logical_and(n_eq > need, thr != INT_MIN)
    cut_scr[0:1, :] = jnp.full((1, n), seq, I32)

    @pl.when(jnp.max(tied.astype(I32)) > 0)
    def _():
        m = jnp.zeros((1, n), I32)
        for bit in range(int(math.log2(seq)) - 1, -1, -1):
            cand = m | jnp.int32(1 << bit)
            cnt = count(lambda kk, idx: jnp.logical_and(kk == thr, idx < cand))
            m = jnp.where(cnt < need, cand, m)
        cut_scr[0:1, :] = jnp.where(tied, m, seq)

    cut = cut_scr[0:1, :]

    def mask_tile(i, carry):
        ks = tile_start(i)
        kk = key_scr[pl.ds(ks, wide), :]
        sel = jnp.logical_or(kk > thr, jnp.logical_and(kk == thr, ks - wide + row_w <= cut))
        sel = jnp.logical_and(sel, kk != INT_MIN)
        m_t = jnp.where(sel, 0.0, NEG)
        for c in range(wide // n):
            mask_scr[:, pl.ds(ks + c * n, n)] = m_t[c * n:(c + 1) * n, :].T
        return carry

    lax.fori_loop(0, n_wide, mask_tile, 0)

    lane = lax.broadcasted_iota(I32, (n, LANES), 1)

    def split(x):
        hi = x.astype(BF16).astype(F32)
        return hi, x - hi

    def q_operand(shift, far):
        rows = []
        for h in range(N_HEADS):
            op = jnp.concatenate([q[:, h * HEAD:(h + 1) * HEAD] * (HEAD ** -0.5), jnp.zeros((n, HEAD), F32)], axis=1)
            if shift is not None:
                hi, lo = split(shift[h])
                op = op + jnp.where(lane == HEAD, hi, 0.0) + jnp.where(lane == HEAD + 1, lo, 0.0)
            if far:
                hi, lo = split(jnp.full((n, 1), far_ref[h], F32))
                op = op + jnp.where(lane == HEAD + 2, hi, 0.0) + jnp.where(lane == HEAD + 3, lo, 0.0)
            rows.append(op)
        return jnp.concatenate(rows, axis=0).astype(BF16)

    def logits(i, q_op, near):
        ks = tile_start(i)
        s_all = jnp.dot(q_op, kt_scr[:, pl.ds(ks, wide)], preferred_element_type=F32)
        msk = mask_scr[:, pl.ds(ks, wide)]
        out = []
        for h in range(N_HEADS):
            s = s_all[h * n:(h + 1) * n, :] + msk
            out.append(s + bias_ref[h] if near else s)
        return ks, out

    def max_tile(i, q_op, near):
        _, ss = logits(i, q_op, near)
        for h in range(N_HEADS):
            m = mx_scr[h * n:(h + 1) * n, :]
            for c in range(wide // n):
                m = jnp.maximum(m, ss[h][:, c * n:(c + 1) * n])
            mx_scr[h * n:(h + 1) * n, :] = m

    def sum_tile(i, q_op, near):
        ks, ss = logits(i, q_op, near)
        ps = []
        for h in range(N_HEADS):
            p = jnp.exp(ss[h])
            part_l = l_scr[h * n:(h + 1) * n, :]
            for c in range(wide // n):
                part_l = part_l + p[:, c * n:(c + 1) * n]
            l_scr[h * n:(h + 1) * n, :] = part_l
            ps.append(p.astype(BF16))
        acc_scr[...] += jnp.dot(jnp.concatenate(ps, axis=0), kv_scr[pl.ds(ks, wide), :], preferred_element_type=F32)

    def run(tile_fn, shift):
        tile_fn(0, q_operand(shift, False), True)
        q_far = q_operand(shift, True)

        def body(i, carry):
            tile_fn(i, q_far, False)
            return carry
        lax.fori_loop(1, n_wide, body, 0)

    mx_scr[...] = jnp.full(mx_scr.shape, NEG, F32)
    run(max_tile, None)
    neg_max = [-jnp.max(mx_scr[h * n:(h + 1) * n, :], axis=1, keepdims=True) for h in range(N_HEADS)]
    acc_scr[...] = jnp.zeros_like(acc_scr)
    l_scr[...] = jnp.zeros_like(l_scr)
    run(sum_tile, neg_max)
    o_ref[0] = jnp.concatenate(
        [acc_scr[h * n:(h + 1) * n, HEAD:] / jnp.sum(l_scr[h * n:(h + 1) * n, :], axis=1, keepdims=True)
         for h in range(N_HEADS)], axis=1).astype(o_ref.dtype)


def _t5_bucket(dist):
    n = jnp.maximum(dist, 0)
    max_exact = REL_BUCKETS // 2
    log_ratio = jnp.log(jnp.maximum(n, 1).astype(F32) / max_exact) / math.log(REL_MAX_DIST / max_exact)
    large = jnp.minimum(max_exact + (log_ratio * (REL_BUCKETS - max_exact)).astype(I32), REL_BUCKETS - 1)
    return jnp.where(n < max_exact, n, large)


def _dsa(u_c, kv_norm_w, w_uk, w_uv, rel_bias):
    b, l, _ = u_c.shape
    top_k = min(TOPK_MAX, l // 4)
    assert DSA_WIDE - Q_BLOCK + 1 >= REL_MAX_DIST and l % DSA_WIDE == 0
    dist = (jnp.arange(Q_BLOCK, dtype=I32)[:, None] + (DSA_WIDE - Q_BLOCK)) - jnp.arange(DSA_WIDE, dtype=I32)[None, :]
    onehot = _t5_bucket(dist)[None] == jnp.arange(REL_BUCKETS, dtype=I32)[:, None, None]
    table = rel_bias.astype(F32)
    bias = jnp.stack([jnp.sum(jnp.where(onehot, table[:, h][:, None, None], 0.0), axis=0) for h in range(N_HEADS)])
    w_kv = jnp.concatenate([w_uk, w_uv], axis=1).astype(F32)
    kern = functools.partial(_dsa_kernel, top_k=top_k)
    lp = l + DSA_WIDE
    return pl.pallas_call(
        kern,
        grid=(b, l // Q_BLOCK),
        in_specs=[pl.BlockSpec((1, Q_BLOCK, C_COLS), lambda i, j: (i, j, 0)),
                  _full((1, LANES)), _full((LANES, LANES)), _full((N_HEADS, Q_BLOCK, DSA_WIDE)),
                  pl.BlockSpec(memory_space=pltpu.SMEM)],
        out_specs=pl.BlockSpec((1, Q_BLOCK, W_GROUP), lambda i, j: (i, j, 0)),
        out_shape=jax.ShapeDtypeStruct((b, l, W_GROUP), BF16),
        scratch_shapes=[pltpu.VMEM((LANES, lp), BF16), pltpu.VMEM((lp, LANES), BF16), pltpu.VMEM((lp, LANES), BF16),
                        pltpu.VMEM((lp, Q_BLOCK), I32), pltpu.VMEM((Q_BLOCK, lp), F32),
                        pltpu.VMEM((N_HEADS * Q_BLOCK, LANES), F32), pltpu.VMEM((N_HEADS * Q_BLOCK, LANES), F32),
                        pltpu.VMEM((N_HEADS * Q_BLOCK, LANES), F32), pltpu.VMEM((8, Q_BLOCK), I32)],
        compiler_params=_params("parallel", "arbitrary"),
        name="dsa",
    )(u_c, kv_norm_w.astype(F32)[None, :], w_kv, bias, table[REL_BUCKETS - 1])


def _s5_kernel(u_ref, are_ref, aim_ref, ldt_ref, bre_ref, bim_ref, cre_ref, cim_ref, dsk_ref, gw_ref, gb_ref,
               o_ref, pre, pim, bdr, bdi, hre, him, carry):
    first = jnp.logical_and(pl.program_id(0) == 0, pl.program_id(1) == 0)
    n = S_BLOCK
    pad = LANES

    @pl.when(first)
    def _():
        lam_re = jnp.minimum(are_ref[...], -1e-4)
        lam_im = aim_ref[...]
        dt = jnp.exp(ldt_ref[...])
        mag = jnp.exp(dt * lam_re)
        ab_re = mag * jnp.cos(dt * lam_im)
        ab_im = mag * jnp.sin(dt * lam_im)
        den = lam_re * lam_re + lam_im * lam_im
        f_re = ((ab_re - 1.0) * lam_re + ab_im * lam_im) / den
        f_im = (ab_im * lam_re - (ab_re - 1.0) * lam_im) / den
        b_re = bre_ref[...]
        b_im = bim_ref[...]
        bb_re = f_re * b_re - f_im * b_im
        bb_im = f_re * b_im + f_im * b_re
        rg = lax.broadcasted_iota(I32, (W_GROUP, S_LANES), 0) // S_GROUP_CH
        cg = lax.broadcasted_iota(I32, (W_GROUP, S_LANES), 1) // S_STATE
        same = rg == cg
        bdr[...] = jnp.where(same, jnp.concatenate([bb_re] * S_GROUPS, axis=0), 0.0)
        bdi[...] = jnp.where(same, jnp.concatenate([bb_im] * S_GROUPS, axis=0), 0.0)
        pre[0:1, :] = ab_re
        pim[0:1, :] = ab_im
        d = 1
        while d < n:
            sr = pre[d - 1:d, :]
            si = pim[d - 1:d, :]
            xr = pre[0:d, :]
            xi = pim[0:d, :]
            pre[d:2 * d, :] = xr * sr - xi * si
            pim[d:2 * d, :] = xr * si + xi * sr
            d *= 2
        hre[...] = jnp.zeros_like(hre)
        him[...] = jnp.zeros_like(him)

    @pl.when(pl.program_id(1) == 0)
    def _():
        carry[...] = jnp.zeros_like(carry)

    u = u_ref[0]
    hre[0, pad:pad + n, :] = _mm(u, bdr[...])
    him[0, pad:pad + n, :] = _mm(u, bdi[...])

    def col_block(cb, c_):
        cs = pl.ds(pl.multiple_of(cb * LANES, LANES), LANES)
        src = 0
        d = 1
        while d < n:
            ar = pre[d - 1:d, cs]
            ai = pim[d - 1:d, cs]
            xr = hre[src, pl.ds(pad - d, n), cs]
            xi = him[src, pl.ds(pad - d, n), cs]
            hre[1 - src, pad:pad + n, cs] = hre[src, pad:pad + n, cs] + ar * xr - ai * xi
            him[1 - src, pad:pad + n, cs] = him[src, pad:pad + n, cs] + ar * xi + ai * xr
            src = 1 - src
            d *= 2
        assert src == 0
        return c_

    lax.fori_loop(0, S_LANES // LANES, col_block, 0)

    c_re = carry[0:1, :]
    c_im = carry[1:2, :]
    p_re = pre[...]
    p_im = pim[...]
    h_re = hre[0, pad:pad + n, :] + p_re * c_re - p_im * c_im
    h_im = him[0, pad:pad + n, :] + p_re * c_im + p_im * c_re
    carry[0:1, :] = h_re[n - 1:n, :]
    carry[1:2, :] = h_im[n - 1:n, :]
    y = _mm(h_re, cre_ref[...]) - _mm(h_im, cim_ref[...]) + dsk_ref[...] * u
    y = jax.nn.gelu(y)
    gl = _mm(y, gw_ref[...]) + gb_ref[...]
    o_ref[0] = (gl[:, 0:W_GROUP] * jax.nn.sigmoid(gl[:, W_GROUP:])).astype(o_ref.dtype)


def _s5(u_s, a_re, a_im, b_re, b_im, c_re, c_im, d_skip, log_dt, glu_w, glu_b):
    b, l, _ = u_s.shape
    f = lambda t: t.astype(F32).reshape(1, -1)
    eye = jnp.eye(S_GROUPS, dtype=F32)
    bt = lambda t: jnp.transpose(t.astype(F32), (2, 0, 1)).reshape(S_GROUP_CH, S_LANES)
    cbd = lambda t: jnp.einsum('gpn,gh->gnhp', t.astype(F32), eye).reshape(S_LANES, W_GROUP)
    ldt = jnp.repeat(log_dt.astype(F32), S_STATE)[None, :]
    n_log = int(math.log2(S_BLOCK))
    assert n_log % 2 == 0
    return pl.pallas_call(
        _s5_kernel,
        grid=(b, l // S_BLOCK),
        in_specs=[pl.BlockSpec((1, S_BLOCK, S_COLS), lambda i, j: (i, j, 0)),
                  _full((1, S_LANES)), _full((1, S_LANES)), _full((1, S_LANES)),
                  _full((S_GROUP_CH, S_LANES)), _full((S_GROUP_CH, S_LANES)),
                  _full((S_LANES, W_GROUP)), _full((S_LANES, W_GROUP)),
                  _full((1, W_GROUP)), _full((W_GROUP, 2 * W_GROUP)), _full((1, 2 * W_GROUP))],
        out_specs=pl.BlockSpec((1, S_BLOCK, W_GROUP), lambda i, j: (i, j, 0)),
        out_shape=jax.ShapeDtypeStruct((b, l, W_GROUP), BF16),
        scratch_shapes=[pltpu.VMEM((S_BLOCK, S_LANES), F32), pltpu.VMEM((S_BLOCK, S_LANES), F32),
                        pltpu.VMEM((W_GROUP, S_LANES), F32), pltpu.VMEM((W_GROUP, S_LANES), F32),
                        pltpu.VMEM((2, S_BLOCK + LANES, S_LANES), F32), pltpu.VMEM((2, S_BLOCK + LANES, S_LANES), F32),
                        pltpu.VMEM((8, S_LANES), F32)],
        compiler_params=_params("arbitrary", "arbitrary"),
        name="s5",
    )(u_s, f(a_re), f(a_im), ldt, bt(b_re), bt(b_im), cbd(c_re), cbd(c_im), f(d_skip),
      glu_w.astype(F32), f(glu_b))


def _out_proj_kernel(x_ref, ym_ref, yr_ref, yc_ref, ys_ref, w_ref, o_ref):
    acc = x_ref[...]
    for i, y_ref in enumerate((ym_ref, yr_ref, yc_ref, ys_ref)):
        acc = acc + jnp.dot(y_ref[...], w_ref[i * W_GROUP:(i + 1) * W_GROUP, :], preferred_element_type=F32)
    o_ref[...] = acc


def _out_proj(x2, ys, w_out, tm=512):
    t, d = x2.shape
    yspec = pl.BlockSpec((tm, W_GROUP), lambda i: (i, 0))
    return pl.pallas_call(
        _out_proj_kernel,
        grid=(t // tm,),
        in_specs=[pl.BlockSpec((tm, d), lambda i: (i, 0)), yspec, yspec, yspec, yspec, _full(w_out.shape)],
        out_specs=pl.BlockSpec((tm, d), lambda i: (i, 0)),
        out_shape=jax.ShapeDtypeStruct((t, d), F32),
        compiler_params=_params("parallel"),
        name="out_proj",
    )(x2, *ys, w_out)


def _moe_kernel(x_ref, nw_ref, wr_ref, br_ref, wg_ref, wu_ref, wd_ref, fw_ref, o_ref, t_scr, cw_scr, acc_scr,
                *, final_norm):
    e = pl.program_id(1)
    tm = x_ref.shape[0]
    lane = lax.broadcasted_iota(I32, (tm, LANES), 1)

    @pl.when(e == 0)
    def _():
        x = x_ref[...]
        t = x * lax.rsqrt(jnp.mean(x * x, axis=-1, keepdims=True) + NORM_EPS) * nw_ref[...]
        t_scr[...] = t.astype(BF16)
        logits = _mm(t, wr_ref[...], precise=True) + br_ref[...]
        big = jnp.int32(LANES)
        is_g = lane < E_GROUPS
        gl = jnp.where(is_g, logits, -jnp.inf)
        gmax = jnp.max(gl, axis=1, keepdims=True)
        g_sel = jnp.min(jnp.where(jnp.logical_and(is_g, gl == gmax), lane, big), axis=1, keepdims=True)
        g_gate = 1.0 / jnp.sum(jnp.exp(gl - gmax), axis=1, keepdims=True)
        lo = E_GROUPS + g_sel * E_PER_GROUP
        in_g = jnp.logical_and(lane >= lo, lane < lo + E_PER_GROUP)
        el = jnp.where(in_g, logits, -jnp.inf)
        m1 = jnp.max(el, axis=1, keepdims=True)
        i1 = jnp.min(jnp.where(el == m1, lane, big), axis=1, keepdims=True)
        el2 = jnp.where(lane == i1, -jnp.inf, el)
        m2 = jnp.max(el2, axis=1, keepdims=True)
        i2 = jnp.min(jnp.where(el2 == m2, lane, big), axis=1, keepdims=True)
        e2 = jnp.exp(m2 - m1)
        w1 = 1.0 / (1.0 + e2)
        w2 = e2 / (1.0 + e2)
        cw_scr[...] = jnp.where(lane == i1, w1, jnp.where(lane == i2, w2, 0.0)) * g_gate
        acc_scr[...] = jnp.zeros_like(acc_scr)

    t = t_scr[...]
    cw_e = jnp.sum(jnp.where(lane == e + E_GROUPS, cw_scr[...], 0.0), axis=1, keepdims=True)
    hid = _silu(jnp.dot(t, wg_ref[0], preferred_element_type=F32)) * jnp.dot(t, wu_ref[0], preferred_element_type=F32)
    hid = (hid * cw_e).astype(BF16)
    acc_scr[...] += jnp.dot(hid, wd_ref[0], preferred_element_type=F32)

    @pl.when(e == pl.num_programs(1) - 1)
    def _():
        y = x_ref[...] + acc_scr[...]
        if final_norm:
            y = y * lax.rsqrt(jnp.mean(y * y, axis=-1, keepdims=True) + NORM_EPS) * fw_ref[...]
        o_ref[...] = y


def _moe(x2, norm_w, wr_g, br_g, wr_e, br_e, w_gate, w_up, w_down, final_w, final_norm, tm=1024):
    t, d = x2.shape
    wr = jnp.zeros((d, LANES), F32).at[:, 0:E_GROUPS].set(wr_g.astype(F32))
    wr = wr.at[:, E_GROUPS:E_GROUPS + N_EXPERTS].set(wr_e.astype(F32))
    br = jnp.zeros((1, LANES), F32).at[0, 0:E_GROUPS].set(br_g.astype(F32))
    br = br.at[0, E_GROUPS:E_GROUPS + N_EXPERTS].set(br_e.astype(F32))
    wg = w_gate.reshape(N_EXPERTS, d, D_EXPERT).astype(BF16)
    wu = w_up.reshape(N_EXPERTS, d, D_EXPERT).astype(BF16)
    wd = w_down.reshape(N_EXPERTS, D_EXPERT, d).astype(BF16)
    kern = functools.partial(_moe_kernel, final_norm=final_norm)
    return pl.pallas_call(
        kern,
        grid=(t // tm, N_EXPERTS),
        in_specs=[pl.BlockSpec((tm, d), lambda i, e: (i, 0)), _full((1, d)), _full((d, LANES)), _full((1, LANES)),
                  pl.BlockSpec((1, d, D_EXPERT), lambda i, e: (e, 0, 0)),
                  pl.BlockSpec((1, d, D_EXPERT), lambda i, e: (e, 0, 0)),
                  pl.BlockSpec((1, D_EXPERT, d), lambda i, e: (e, 0, 0)), _full((1, d))],
        out_specs=pl.BlockSpec((tm, d), lambda i, e: (i, 0)),
        out_shape=jax.ShapeDtypeStruct((t, d), F32),
        scratch_shapes=[pltpu.VMEM((tm, d), BF16), pltpu.VMEM((tm, LANES), F32), pltpu.VMEM((tm, d), F32)],
        compiler_params=_params("parallel", "arbitrary"),
        name="moe",
    )(x2, norm_w.astype(F32)[None, :], wr, br, wg, wu, wd, final_w.astype(F32)[None, :])


def _pad_in_proj(w):
    d = w.shape[0]
    m_in, r_in = 772, 896
    c_used = 680
    z = lambda k: jnp.zeros((d, k), w.dtype)
    wm = w[:, 0:m_in]
    wr = w[:, m_in:m_in + r_in]
    wc = w[:, m_in + r_in:m_in + r_in + c_used]
    ws = w[:, m_in + r_in + c_used:]
    return jnp.concatenate([wm, z(M_COLS - m_in), wr, wc, z(C_COLS - c_used), ws], axis=1).astype(BF16)


def kernel(x, norm1_w, w_in, m_conv_w, m_conv_b, m_dt_bias, m_a_log, m_d, m_norm_w, r_mu, r_w0, r_w2, r_a0, r_a2, r_g2, r_k_k, r_k_a, r_r_k, r_ln_w, r_ln_b, c_kv_norm_w, c_w_uk, c_w_uv, rel_bias, s_a_re, s_a_im, s_b_re, s_b_im, s_c_re, s_c_im, s_d, s_log_dt, s_glu_w, s_glu_b, w_out, norm2_w, moe_wr_group, moe_br_group, moe_wr_exp, moe_br_exp, moe_w_gate, moe_w_up, moe_w_down, final_norm_w):
    bsz, seq, d = x.shape
    depth = w_in.shape[0]
    x2 = x.astype(F32).reshape(bsz * seq, d)
    for i in range(depth):
        u_m, u_r, u_c, u_s = _in_proj(x2, norm1_w[i].astype(F32)[None, :], _pad_in_proj(w_in[i]))
        sh = lambda t: t.reshape(bsz, seq, t.shape[-1])
        y_m = _mamba(sh(u_m), m_conv_w[i], m_conv_b[i], m_dt_bias[i], m_a_log[i], m_d[i], m_norm_w[i])
        y_r = _rwkv(sh(u_r), r_mu[i], r_w0[i], r_w2[i], r_a0[i], r_a2[i], r_g2[i],
                    r_k_k[i], r_k_a[i], r_r_k[i], r_ln_w[i], r_ln_b[i])
        y_c = _dsa(sh(u_c), c_kv_norm_w[i], c_w_uk[i], c_w_uv[i], rel_bias)
        y_s = _s5(sh(u_s), s_a_re[i], s_a_im[i], s_b_re[i], s_b_im[i], s_c_re[i], s_c_im[i],
                  s_d[i], s_log_dt[i], s_glu_w[i], s_glu_b[i])
        ys = [t.reshape(bsz * seq, W_GROUP) for t in (y_m, y_r, y_c, y_s)]
        x2 = _out_proj(x2, ys, w_out[i].astype(BF16))
        x2 = _moe(x2, norm2_w[i], moe_wr_group[i], moe_br_group[i], moe_wr_exp[i], moe_br_exp[i],
                  moe_w_gate[i], moe_w_up[i], moe_w_down[i], final_norm_w, final_norm=(i == depth - 1))
    return x2.reshape(bsz, seq, d).astype(x.dtype)
```

```python
import functools
import math

import jax
import jax.numpy as jnp
from jax import lax
from jax.experimental import pallas as pl
from jax.experimental.pallas import tpu as pltpu

F32 = jnp.float32
BF16 = jnp.bfloat16
I32 = jnp.int32
HIGHEST = lax.Precision.HIGHEST

LANES = 128
VMEM_LIMIT = 56 * 1024 * 1024

NORM_EPS = 1e-6
W_GROUP = 256
HEAD = 64
N_HEADS = W_GROUP // HEAD
M_CONV_CH = 512
M_CONV = 4
M_COLS = 896
R_COLS = 896
C_COLS = 768
S_COLS = 256
SSD_CHUNK = 128
R_CHUNK = 64
R_BLOCK = 128
R_GN_EPS = 64e-5
I_HEADS = 8
I_DIM = 32
Q_BLOCK = 128
DSA_WIDE = 512
DSA_AUX = 4
TOPK_MAX = 256
REL_BUCKETS = 32
REL_MAX_DIST = 128
S_GROUPS = 16
S_GROUP_CH = 16
S_STATE = 64
S_LANES = S_GROUPS * S_STATE
S_BLOCK = 256
E_GROUPS = 4
E_PER_GROUP = 8
N_EXPERTS = E_GROUPS * E_PER_GROUP
D_EXPERT = 256
MOE_CAP = 320
INT_MIN = -2 ** 31
NEG = -1e30


def _mm(a, b, precise=False):
    if precise:
        return jnp.dot(a.astype(F32), b.astype(F32), precision=HIGHEST, preferred_element_type=F32)
    return jnp.dot(a.astype(BF16), b.astype(BF16), preferred_element_type=F32)


def _mm_nt(a, b, precise=False):
    dn = (((1,), (1,)), ((), ()))
    if precise:
        return lax.dot_general(a.astype(F32), b.astype(F32), dn, precision=HIGHEST, preferred_element_type=F32)
    return lax.dot_general(a.astype(BF16), b.astype(BF16), dn, preferred_element_type=F32)


def _mm_tn(a, b, precise=False):
    dn = (((0,), (0,)), ((), ()))
    if precise:
        return lax.dot_general(a.astype(F32), b.astype(F32), dn, precision=HIGHEST, preferred_element_type=F32)
    return lax.dot_general(a.astype(BF16), b.astype(BF16), dn, preferred_element_type=F32)


def _mm_split(a, b, split_rhs=False):
    x = b if split_rhs else a
    hi = x.astype(BF16)
    lo = (x - hi.astype(F32)).astype(BF16)
    if split_rhs:
        a = a.astype(BF16)
        return jnp.dot(a, hi, preferred_element_type=F32) + jnp.dot(a, lo, preferred_element_type=F32)
    b = b.astype(BF16)
    return jnp.dot(hi, b, preferred_element_type=F32) + jnp.dot(lo, b, preferred_element_type=F32)


def _silu(x):
    return x * jax.nn.sigmoid(x)


def _softplus(x):
    return jnp.maximum(x, 0.0) + jnp.log(1.0 + jnp.exp(-jnp.abs(x)))


def _params(*sem):
    return pltpu.CompilerParams(dimension_semantics=sem, vmem_limit_bytes=VMEM_LIMIT)


def _full(shape):
    nd = len(shape)
    return pl.BlockSpec(shape, lambda *_: (0,) * nd)


def _in_proj_kernel(x_ref, nw_ref, w_ref, om_ref, or_ref, oc_ref, os_ref):
    x = x_ref[...]
    h = x * lax.rsqrt(jnp.mean(x * x, axis=-1, keepdims=True) + NORM_EPS) * nw_ref[...]
    h = h.astype(BF16)
    off = 0
    for o_ref, width in ((om_ref, M_COLS), (or_ref, R_COLS), (oc_ref, C_COLS), (os_ref, S_COLS)):
        o_ref[...] = jnp.dot(h, w_ref[:, off:off + width], preferred_element_type=F32)
        off += width


def _in_proj(x2, norm_w, w_pad, tm=512):
    t, d = x2.shape
    n = w_pad.shape[1]
    return pl.pallas_call(
        _in_proj_kernel,
        grid=(t // tm,),
        in_specs=[pl.BlockSpec((tm, d), lambda i: (i, 0)), _full((1, d)), _full((d, n))],
        out_specs=[pl.BlockSpec((tm, c), lambda i: (i, 0)) for c in (M_COLS, R_COLS, C_COLS, S_COLS)],
        out_shape=[jax.ShapeDtypeStruct((t, c), F32) for c in (M_COLS, R_COLS, C_COLS, S_COLS)],
        compiler_params=_params("parallel"),
        name="in_proj",
    )(x2, norm_w, w_pad)


def _mamba_kernel(u_ref, cw_ref, cb_ref, dtb_ref, alog_ref, dsk_ref, nw_ref, o_ref, xbuf, state):
    c = pl.program_id(1)
    n = SSD_CHUNK

    @pl.when(c == 0)
    def _():
        xbuf[0:8, :] = jnp.zeros((8, M_CONV_CH), F32)
        state[...] = jnp.zeros_like(state)

    u = u_ref[0]
    z = u[:, 0:W_GROUP]
    dtr = u[:, W_GROUP + M_CONV_CH:M_COLS]
    xbuf[8:8 + n, :] = u[:, W_GROUP:W_GROUP + M_CONV_CH]
    acc = jnp.broadcast_to(cb_ref[...], (n, M_CONV_CH))
    for j in range(M_CONV):
        acc = acc + cw_ref[j:j + 1, :] * xbuf[pl.ds(8 - (M_CONV - 1) + j, n), :]
    xbuf[0:8, :] = xbuf[n:n + 8, :]
    xc = _silu(acc)
    xs = xc[:, 0:W_GROUP]
    bm = xc[:, W_GROUP:W_GROUP + 2 * HEAD]
    cm = xc[:, W_GROUP + 2 * HEAD:]

    dt = _softplus(dtr + dtb_ref[...])
    a = dt * (-jnp.exp(alog_ref[...]))
    row = lax.broadcasted_iota(I32, (n, n), 0)
    col = lax.broadcasted_iota(I32, (n, n), 1)
    causal = row >= col
    a_cum = _mm(causal.astype(F32), a, precise=True)
    a_cum_t = a_cum.T
    a_last = a_cum[n - 1:n, :]

    ys = []
    for h in range(N_HEADS):
        g = h // 2
        b_h = bm[:, g * HEAD:(g + 1) * HEAD]
        c_h = cm[:, g * HEAD:(g + 1) * HEAD]
        x_h = xs[:, h * HEAD:(h + 1) * HEAD]
        xdt = x_h * dt[:, h:h + 1]
        ac = a_cum[:, h:h + 1]
        dec = jnp.exp(jnp.where(causal, ac - a_cum_t[h:h + 1, :], -jnp.inf))
        scores = _mm_nt(c_h, b_h) * dec
        y = _mm(scores, xdt)
        st = state[h]
        y = y + _mm_nt(c_h, st) * jnp.exp(ac)
        al = a_last[:, h:h + 1]
        state[h] = st * jnp.exp(al) + _mm_tn(xdt, b_h * jnp.exp(al - ac))
        ys.append(y + x_h * dsk_ref[:, h * HEAD:(h + 1) * HEAD])
    y = jnp.concatenate(ys, axis=1) * _silu(z)
    y = y * lax.rsqrt(jnp.mean(y * y, axis=-1, keepdims=True) + NORM_EPS) * nw_ref[...]
    o_ref[0] = y.astype(o_ref.dtype)


def _mamba(u_m, conv_w, conv_b, dt_bias, a_log, d_skip, norm_w):
    b, l, _ = u_m.shape
    pad = lambda v: jnp.pad(v.astype(F32), (0, LANES - v.shape[0]))[None, :]
    return pl.pallas_call(
        _mamba_kernel,
        grid=(b, l // SSD_CHUNK),
        in_specs=[pl.BlockSpec((1, SSD_CHUNK, M_COLS), lambda i, j: (i, j, 0)),
                  _full((M_CONV, M_CONV_CH)), _full((1, M_CONV_CH)), _full((1, LANES)), _full((1, LANES)),
                  _full((1, W_GROUP)), _full((1, W_GROUP))],
        out_specs=pl.BlockSpec((1, SSD_CHUNK, W_GROUP), lambda i, j: (i, j, 0)),
        out_shape=jax.ShapeDtypeStruct((b, l, W_GROUP), BF16),
        scratch_shapes=[pltpu.VMEM((SSD_CHUNK + 8, M_CONV_CH), F32), pltpu.VMEM((N_HEADS, HEAD, HEAD), F32)],
        compiler_params=_params("parallel", "arbitrary"),
        name="mamba_ssd",
    )(u_m, conv_w.astype(F32), conv_b.astype(F32)[None, :], pad(dt_bias), pad(a_log),
      jnp.repeat(d_skip.astype(F32), HEAD)[None, :], norm_w.astype(F32)[None, :])


def _rwkv_kernel(u_ref, mu_ref, wl_ref, w0_ref, a0_ref, kk_ref, ka_ref, rk_ref, lnw_ref, lnb_ref, seg_ref,
                 o_ref, ubuf, state):
    c = pl.program_id(1)
    n = R_BLOCK
    cs = R_CHUNK

    @pl.when(c == 0)
    def _():
        ubuf[0:8, :] = jnp.zeros((8, R_COLS), F32)
        state[...] = jnp.zeros_like(state)

    u = u_ref[0]
    ubuf[8:8 + n, :] = u
    u = u + mu_ref[...] * (ubuf[pl.ds(7, n), :] - u)
    ubuf[0:8, :] = ubuf[n:n + 8, :]
    r = u[:, 0:W_GROUP]
    k = u[:, W_GROUP:2 * W_GROUP]
    v = u[:, 2 * W_GROUP:3 * W_GROUP]
    lo = u[:, 3 * W_GROUP:]
    lane = lax.broadcasted_iota(I32, lo.shape, 1)
    lo = jnp.where(lane < 32, jnp.tanh(lo), jnp.where(lane < 64, lo, jax.nn.sigmoid(lo)))
    proj = _mm(lo, wl_ref[...])
    w_log = -_softplus(-(w0_ref[...] + proj[:, 0:W_GROUP])) - 0.5
    ld = -jnp.exp(w_log)
    a_lr = jax.nn.sigmoid(a0_ref[...] + proj[:, W_GROUP:2 * W_GROUP])
    g = proj[:, 2 * W_GROUP:]
    seg = seg_ref[...]
    kk = k * kk_ref[...]
    kk = kk * lax.rsqrt(jnp.maximum(_mm_split(kk * kk, seg), 1e-24))
    k = k * (1.0 + (a_lr - 1.0) * ka_ref[...])
    a_v = -kk
    b_v = kk * a_lr

    row = lax.broadcasted_iota(I32, (n, n), 0)
    col = lax.broadcasted_iota(I32, (n, n), 1)
    tri2 = ((row >= col) & ((row // cs) == (col // cs))).astype(F32)
    lw = _mm_split(tri2, ld, split_rhs=True)
    w_inc = jnp.exp(lw)
    w_inv = jnp.exp(-lw)
    rt = r * w_inc
    at = a_v * jnp.exp(lw - ld)
    bt = b_v * w_inv
    kt = k * w_inv

    wg = W_GROUP
    r2 = lax.broadcasted_iota(I32, (wg, wg), 0)
    c2 = lax.broadcasted_iota(I32, (wg, wg), 1)
    lower = (r2 % cs) >= (c2 % cs)
    strict = (r2 % cs) > (c2 % cs)
    eye = (r2 == c2).astype(F32)
    lane_head = lax.broadcasted_iota(I32, (cs, wg), 1) // HEAD

    def blocks(xc, dtype=BF16):
        return jnp.concatenate([jnp.where(lane_head == h, xc, 0.0) for h in range(N_HEADS)], axis=0).astype(dtype)

    s_cur = state[...]
    outs = []
    for ci in range(n // cs):
        rows = slice(ci * cs, (ci + 1) * cs)
        lw_end = lw[ci * cs + cs - 1:ci * cs + cs, :]
        ratio = jnp.exp(lw_end - lw[rows, :])
        at_b, rt_b, bt_b, kt_b = blocks(at[rows, :]), blocks(rt[rows, :], F32), blocks(bt[rows, :]), blocks(kt[rows, :])
        v_b = blocks(v[rows, :])
        bh = blocks(b_v[rows, :] * ratio)
        kh = blocks(k[rows, :] * ratio)
        a_ab = jnp.where(strict, _mm_nt(at_b, bt_b), 0.0)
        a_ak = jnp.where(strict, _mm_nt(at_b, kt_b), 0.0)
        a_rb = jnp.where(lower, _mm_nt(rt_b, bt_b), 0.0)
        a_rk = jnp.where(lower, _mm_nt(rt_b, kt_b), 0.0)
        tinv = eye + a_ab
        pw = a_ab
        for _ in range(int(math.log2(cs)) - 1):
            pw = _mm(pw, pw)
            tinv = tinv + _mm(pw, tinv)
        ta = _mm(tinv, at_b)
        pv = _mm(tinv, _mm(a_ak, v_b))
        q_eff = rt_b + _mm(a_rb, ta)
        o_v = _mm(a_rb, pv) + _mm(a_rk, v_b)
        s_v = _mm_tn(pv, bh) + _mm_tn(v_b, kh)
        o_b = _mm_nt(q_eff, s_cur) + o_v
        outs.append(o_b[0:cs] + o_b[cs:2 * cs] + o_b[2 * cs:3 * cs] + o_b[3 * cs:])
        s_cur = s_cur * jnp.exp(lw_end) + _mm_split(s_cur, _mm_tn(ta, bh)) + s_v
    state[...] = s_cur
    o = jnp.concatenate(outs, axis=0)

    inv = 1.0 / HEAD
    mean = _mm_split(o, seg) * inv
    d = o - mean
    var = _mm_split(d * d, seg) * inv
    o = d * lax.rsqrt(var + R_GN_EPS) * lnw_ref[...] + lnb_ref[...]
    o = o + _mm_split(r * k * rk_ref[...], seg) * v
    o_ref[0] = (o * g).astype(o_ref.dtype)


def _rwkv(u_r, mu, w0, w2, a0, a2, g2, k_k, k_a, r_k, ln_w, ln_b):
    b, l, _ = u_r.shape
    f = lambda t: t.astype(F32).reshape(1, -1)
    w_lora = jnp.zeros((LANES, 3 * W_GROUP), F32)
    w_lora = w_lora.at[0:32, 0:W_GROUP].set(w2.astype(F32))
    w_lora = w_lora.at[32:64, W_GROUP:2 * W_GROUP].set(a2.astype(F32))
    w_lora = w_lora.at[64:128, 2 * W_GROUP:].set(g2.astype(F32))
    hid = jnp.arange(W_GROUP) // HEAD
    seg = (hid[:, None] == hid[None, :]).astype(F32)
    vec = _full((1, W_GROUP))
    return pl.pallas_call(
        _rwkv_kernel,
        grid=(b, l // R_BLOCK),
        in_specs=[pl.BlockSpec((1, R_BLOCK, R_COLS), lambda i, j: (i, j, 0)),
                  _full((1, R_COLS)), _full((LANES, 3 * W_GROUP)),
                  vec, vec, vec, vec, vec, vec, vec, _full((W_GROUP, W_GROUP))],
        out_specs=pl.BlockSpec((1, R_BLOCK, W_GROUP), lambda i, j: (i, j, 0)),
        out_shape=jax.ShapeDtypeStruct((b, l, W_GROUP), BF16),
        scratch_shapes=[pltpu.VMEM((R_BLOCK + 8, R_COLS), F32), pltpu.VMEM((W_GROUP, W_GROUP), F32)],
        compiler_params=_params("parallel", "arbitrary"),
        name="rwkv7",
    )(u_r, f(mu), w_lora, f(w0), f(a0), f(k_k), f(k_a), f(r_k), f(ln_w), f(ln_b), seg)


def _dsa_kernel(u_ref, kvw_ref, wkv_ref, bias_ref, far_ref, o_ref,
                kt_scr, kv_scr, tail_scr, key_scr, mask_scr, acc_scr, mx_scr, l_scr, cut_scr, *, top_k):
    qb = pl.program_id(1)
    n = Q_BLOCK
    wide = DSA_WIDE
    seq = key_scr.shape[0] - wide

    @pl.when(qb == 0)
    def _():
        kt_scr[...] = jnp.zeros_like(kt_scr)
        kv_scr[...] = jnp.zeros_like(kv_scr)
        tail_scr[...] = jnp.zeros_like(tail_scr)
        key_scr[...] = jnp.full(key_scr.shape, INT_MIN, I32)
        mask_scr[...] = jnp.full(mask_scr.shape, NEG, F32)

    u = u_ref[0]
    q = u[:, 0:W_GROUP]
    ckv = u[:, W_GROUP:W_GROUP + LANES]
    qi = u[:, W_GROUP + LANES:2 * W_GROUP + LANES]
    tail = u[:, 2 * W_GROUP + LANES:]
    ckv = ckv * lax.rsqrt(jnp.mean(ckv * ckv, axis=-1, keepdims=True) + NORM_EPS) * kvw_ref[...]
    kv = _mm(ckv, wkv_ref[...])
    start = pl.multiple_of(qb * n, n)
    end = pl.multiple_of(start + wide + n, n)
    sub = lax.broadcasted_iota(I32, (HEAD, n), 0)
    kt_scr[:, pl.ds(end - n, n)] = jnp.concatenate(
        [kv.T[0:HEAD, :], jnp.where(sub < DSA_AUX, 1.0, 0.0)], axis=0).astype(BF16)
    kv_scr[pl.ds(end - n, n), :] = kv.astype(BF16)
    tail_scr[pl.ds(end - n, n), :] = tail.astype(BF16)
    qi_t = qi.T
    w_qi = jnp.concatenate(
        [jnp.concatenate([qi_t[h * I_DIM:(h + 1) * I_DIM, :] for h in range(I_HEADS)], axis=1),
         jnp.zeros((LANES - I_DIM, I_HEADS * n), F32)], axis=0).astype(BF16)
    wi_t = tail.T[I_DIM:I_DIM + I_HEADS, :] * (I_HEADS ** -0.5 * I_DIM ** -0.5)

    q_pos = start + lax.broadcasted_iota(I32, (wide, n), 1)
    row_w = lax.broadcasted_iota(I32, (wide, n), 0)
    n_wide = (start + n + wide - 1) // wide

    def tile_start(i):
        return pl.multiple_of(end - (i + 1) * wide, n)

    def idx_tile(i, carry):
        ks = tile_start(i)
        s_all = jnp.dot(tail_scr[pl.ds(ks, wide), :], w_qi, preferred_element_type=F32)
        s = jnp.zeros((wide, n), F32)
        for h in range(I_HEADS):
            s = s + jnp.maximum(s_all[:, h * n:(h + 1) * n], 0.0) * wi_t[h:h + 1, :]
        s = s + 0.0
        bits = pltpu.bitcast(s, I32)
        key = jnp.where(bits < 0, bits ^ jnp.int32(0x7FFFFFFF), bits)
        idx = ks - wide + row_w
        key = jnp.where(jnp.logical_and(idx <= q_pos, idx >= 0), key, jnp.int32(INT_MIN))
        key_scr[pl.ds(ks, wide), :] = key
        return carry

    lax.fori_loop(0, n_wide, idx_tile, 0)

    part = 64

    def count(pred):
        def body(i, acc):
            ks = tile_start(i)
            kk = key_scr[pl.ds(ks, wide), :]
            hit = pred(kk, ks - wide + row_w).astype(I32)
            for c in range(wide // part):
                acc = acc + hit[c * part:(c + 1) * part, :]
            return acc
        acc = lax.fori_loop(0, n_wide, body, jnp.zeros((part, n), I32))
        return jnp.sum(acc, axis=0, keepdims=True)

    c0 = count(lambda kk, idx: kk >= 0)
    thr = jnp.where(c0 >= top_k, jnp.int32(0), jnp.int32(INT_MIN))
    n_ge = c0
    for bit in range(30, -1, -1):
        cand = thr | jnp.int32(1 << bit)
        cnt = count(lambda kk, idx: kk >= cand)
        keep = cnt >= top_k
        thr = jnp.where(keep, cand, thr)
        n_ge = jnp.where(keep, cnt, n_ge)
    tied = jnp.logical_and(n_ge > top_k, thr != INT_MIN)
    cut_scr[0:1, :] = jnp.full((1, n), seq, I32)

    @pl.when(jnp.max(tied.astype(I32)) > 0)
    def _():
        need = top_k - count(lambda kk, idx: kk > thr)
        m = jnp.zeros((1, n), I32)
        for bit in range(int(math.log2(seq)) - 1, -1, -1):
            cand = m | jnp.int32(1 << bit)
            cnt = count(lambda kk, idx: jnp.logical_and(kk == thr, idx < cand))
            m = jnp.where(cnt < need, cand, m)
        cut_scr[0:1, :] = jnp.where(tied, m, seq)

    cut = cut_scr[0:1, :]

    def mask_tile(i, carry):
        ks = tile_start(i)
        kk = key_scr[pl.ds(ks, wide), :]
        sel = jnp.logical_or(kk > thr, jnp.logical_and(kk == thr, ks - wide + row_w <= cut))
        sel = jnp.logical_and(sel, kk != INT_MIN)
        m_t = jnp.where(sel, 0.0, NEG)
        for c in range(wide // n):
            mask_scr[:, pl.ds(ks + c * n, n)] = m_t[c * n:(c + 1) * n, :].T
        return carry

    lax.fori_loop(0, n_wide, mask_tile, 0)

    lane = lax.broadcasted_iota(I32, (n, LANES), 1)

    def split(x):
        hi = x.astype(BF16).astype(F32)
        return hi, x - hi

    def q_operand(shift, far):
        rows = []
        for h in range(N_HEADS):
            op = jnp.concatenate([q[:, h * HEAD:(h + 1) * HEAD] * (HEAD ** -0.5), jnp.zeros((n, HEAD), F32)], axis=1)
            if shift is not None:
                hi, lo = split(shift[h])
                op = op + jnp.where(lane == HEAD, hi, 0.0) + jnp.where(lane == HEAD + 1, lo, 0.0)
            if far:
                hi, lo = split(jnp.full((n, 1), far_ref[h], F32))
                op = op + jnp.where(lane == HEAD + 2, hi, 0.0) + jnp.where(lane == HEAD + 3, lo, 0.0)
            rows.append(op)
        return jnp.concatenate(rows, axis=0).astype(BF16)

    def logits(i, q_op, near):
        ks = tile_start(i)
        s_all = jnp.dot(q_op, kt_scr[:, pl.ds(ks, wide)], preferred_element_type=F32)
        msk = mask_scr[:, pl.ds(ks, wide)]
        out = []
        for h in range(N_HEADS):
            s = s_all[h * n:(h + 1) * n, :] + msk
            out.append(s + bias_ref[h] if near else s)
        return ks, out

    def max_tile(i, q_op, near):
        _, ss = logits(i, q_op, near)
        for h in range(N_HEADS):
            m = mx_scr[h * n:(h + 1) * n, :]
            for c in range(wide // n):
                m = jnp.maximum(m, ss[h][:, c * n:(c + 1) * n])
            mx_scr[h * n:(h + 1) * n, :] = m

    def sum_tile(i, q_op, near):
        ks, ss = logits(i, q_op, near)
        ps = []
        for h in range(N_HEADS):
            p = jnp.exp(ss[h])
            part_l = l_scr[h * n:(h + 1) * n, :]
            for c in range(wide // n):
                part_l = part_l + p[:, c * n:(c + 1) * n]
            l_scr[h * n:(h + 1) * n, :] = part_l
            ps.append(p.astype(BF16))
        acc_scr[...] += jnp.dot(jnp.concatenate(ps, axis=0), kv_scr[pl.ds(ks, wide), :], preferred_element_type=F32)

    def run(tile_fn, shift):
        tile_fn(0, q_operand(shift, False), True)
        q_far = q_operand(shift, True)

        def body(i, carry):
            tile_fn(i, q_far, False)
            return carry
        lax.fori_loop(1, n_wide, body, 0)

    mx_scr[...] = jnp.full(mx_scr.shape, NEG, F32)
    run(max_tile, None)
    neg_max = [-jnp.max(mx_scr[h * n:(h + 1) * n, :], axis=1, keepdims=True) for h in range(N_HEADS)]
    acc_scr[...] = jnp.zeros_like(acc_scr)
    l_scr[...] = jnp.zeros_like(l_scr)
    run(sum_tile, neg_max)
    o_ref[0] = jnp.concatenate(
        [acc_scr[h * n:(h + 1) * n, HEAD:] / jnp.sum(l_scr[h * n:(h + 1) * n, :], axis=1, keepdims=True)
         for h in range(N_HEADS)], axis=1).astype(o_ref.dtype)


def _t5_bucket(dist):
    n = jnp.maximum(dist, 0)
    max_exact = REL_BUCKETS // 2
    log_ratio = jnp.log(jnp.maximum(n, 1).astype(F32) / max_exact) / math.log(REL_MAX_DIST / max_exact)
    large = jnp.minimum(max_exact + (log_ratio * (REL_BUCKETS - max_exact)).astype(I32), REL_BUCKETS - 1)
    return jnp.where(n < max_exact, n, large)


def _dsa(u_c, kv_norm_w, w_uk, w_uv, rel_bias):
    b, l, _ = u_c.shape
    top_k = min(TOPK_MAX, l // 4)
    assert DSA_WIDE - Q_BLOCK + 1 >= REL_MAX_DIST and l % DSA_WIDE == 0
    dist = (jnp.arange(Q_BLOCK, dtype=I32)[:, None] + (DSA_WIDE - Q_BLOCK)) - jnp.arange(DSA_WIDE, dtype=I32)[None, :]
    onehot = _t5_bucket(dist)[None] == jnp.arange(REL_BUCKETS, dtype=I32)[:, None, None]
    table = rel_bias.astype(F32)
    bias = jnp.stack([jnp.sum(jnp.where(onehot, table[:, h][:, None, None], 0.0), axis=0) for h in range(N_HEADS)])
    w_kv = jnp.concatenate([w_uk, w_uv], axis=1).astype(F32)
    kern = functools.partial(_dsa_kernel, top_k=top_k)
    lp = l + DSA_WIDE
    return pl.pallas_call(
        kern,
        grid=(b, l // Q_BLOCK),
        in_specs=[pl.BlockSpec((1, Q_BLOCK, C_COLS), lambda i, j: (i, j, 0)),
                  _full((1, LANES)), _full((LANES, LANES)), _full((N_HEADS, Q_BLOCK, DSA_WIDE)),
                  pl.BlockSpec(memory_space=pltpu.SMEM)],
        out_specs=pl.BlockSpec((1, Q_BLOCK, W_GROUP), lambda i, j: (i, j, 0)),
        out_shape=jax.ShapeDtypeStruct((b, l, W_GROUP), BF16),
        scratch_shapes=[pltpu.VMEM((LANES, lp), BF16), pltpu.VMEM((lp, LANES), BF16), pltpu.VMEM((lp, LANES), BF16),
                        pltpu.VMEM((lp, Q_BLOCK), I32), pltpu.VMEM((Q_BLOCK, lp), F32),
                        pltpu.VMEM((N_HEADS * Q_BLOCK, LANES), F32), pltpu.VMEM((N_HEADS * Q_BLOCK, LANES), F32),
                        pltpu.VMEM((N_HEADS * Q_BLOCK, LANES), F32), pltpu.VMEM((8, Q_BLOCK), I32)],
        compiler_params=_params("parallel", "arbitrary"),
        name="dsa",
    )(u_c, kv_norm_w.astype(F32)[None, :], w_kv, bias, table[REL_BUCKETS - 1])


def _s5_kernel(u_ref, are_ref, aim_ref, ldt_ref, bre_ref, bim_ref, cre_ref, cim_ref, dsk_ref, gw_ref, gb_ref,
               o_ref, pre, pim, bdr, bdi, hre, him, carry):
    first = jnp.logical_and(pl.program_id(0) == 0, pl.program_id(1) == 0)
    n = S_BLOCK
    pad = LANES

    @pl.when(first)
    def _():
        lam_re = jnp.minimum(are_ref[...], -1e-4)
        lam_im = aim_ref[...]
        dt = jnp.exp(ldt_ref[...])
        mag = jnp.exp(dt * lam_re)
        ab_re = mag * jnp.cos(dt * lam_im)
        ab_im = mag * jnp.sin(dt * lam_im)
        den = lam_re * lam_re + lam_im * lam_im
        f_re = ((ab_re - 1.0) * lam_re + ab_im * lam_im) / den
        f_im = (ab_im * lam_re - (ab_re - 1.0) * lam_im) / den
        b_re = bre_ref[...]
        b_im = bim_ref[...]
        bb_re = f_re * b_re - f_im * b_im
        bb_im = f_re * b_im + f_im * b_re
        rg = lax.broadcasted_iota(I32, (W_GROUP, S_LANES), 0) // S_GROUP_CH
        cg = lax.broadcasted_iota(I32, (W_GROUP, S_LANES), 1) // S_STATE
        same = rg == cg
        bdr[...] = jnp.where(same, jnp.concatenate([bb_re] * S_GROUPS, axis=0), 0.0)
        bdi[...] = jnp.where(same, jnp.concatenate([bb_im] * S_GROUPS, axis=0), 0.0)
        pre[0:1, :] = ab_re
        pim[0:1, :] = ab_im
        d = 1
        while d < n:
            sr = pre[d - 1:d, :]
            si = pim[d - 1:d, :]
            xr = pre[0:d, :]
            xi = pim[0:d, :]
            pre[d:2 * d, :] = xr * sr - xi * si
            pim[d:2 * d, :] = xr * si + xi * sr
            d *= 2
        hre[...] = jnp.zeros_like(hre)
        him[...] = jnp.zeros_like(him)

    @pl.when(pl.program_id(1) == 0)
    def _():
        carry[...] = jnp.zeros_like(carry)

    u = u_ref[0]
    hre[0, pad:pad + n, :] = _mm(u, bdr[...])
    him[0, pad:pad + n, :] = _mm(u, bdi[...])

    def col_block(cb, c_):
        cs = pl.ds(pl.multiple_of(cb * LANES, LANES), LANES)
        src = 0
        d = 1
        while d < n:
            ar = pre[d - 1:d, cs]
            ai = pim[d - 1:d, cs]
            xr = hre[src, pl.ds(pad - d, n), cs]
            xi = him[src, pl.ds(pad - d, n), cs]
            hre[1 - src, pad:pad + n, cs] = hre[src, pad:pad + n, cs] + ar * xr - ai * xi
            him[1 - src, pad:pad + n, cs] = him[src, pad:pad + n, cs] + ar * xi + ai * xr
            src = 1 - src
            d *= 2
        assert src == 0
        return c_

    lax.fori_loop(0, S_LANES // LANES, col_block, 0)

    c_re = carry[0:1, :]
    c_im = carry[1:2, :]
    p_re = pre[...]
    p_im = pim[...]
    h_re = hre[0, pad:pad + n, :] + p_re * c_re - p_im * c_im
    h_im = him[0, pad:pad + n, :] + p_re * c_im + p_im * c_re
    carry[0:1, :] = h_re[n - 1:n, :]
    carry[1:2, :] = h_im[n - 1:n, :]
    y = _mm(h_re, cre_ref[...]) - _mm(h_im, cim_ref[...]) + dsk_ref[...] * u
    y = jax.nn.gelu(y)
    gl = _mm(y, gw_ref[...]) + gb_ref[...]
    o_ref[0] = (gl[:, 0:W_GROUP] * jax.nn.sigmoid(gl[:, W_GROUP:])).astype(o_ref.dtype)


def _s5(u_s, a_re, a_im, b_re, b_im, c_re, c_im, d_skip, log_dt, glu_w, glu_b):
    b, l, _ = u_s.shape
    f = lambda t: t.astype(F32).reshape(1, -1)
    eye = jnp.eye(S_GROUPS, dtype=F32)
    bt = lambda t: jnp.transpose(t.astype(F32), (2, 0, 1)).reshape(S_GROUP_CH, S_LANES)
    cbd = lambda t: jnp.einsum('gpn,gh->gnhp', t.astype(F32), eye).reshape(S_LANES, W_GROUP)
    ldt = jnp.repeat(log_dt.astype(F32), S_STATE)[None, :]
    n_log = int(math.log2(S_BLOCK))
    assert n_log % 2 == 0
    return pl.pallas_call(
        _s5_kernel,
        grid=(b, l // S_BLOCK),
        in_specs=[pl.BlockSpec((1, S_BLOCK, S_COLS), lambda i, j: (i, j, 0)),
                  _full((1, S_LANES)), _full((1, S_LANES)), _full((1, S_LANES)),
                  _full((S_GROUP_CH, S_LANES)), _full((S_GROUP_CH, S_LANES)),
                  _full((S_LANES, W_GROUP)), _full((S_LANES, W_GROUP)),
                  _full((1, W_GROUP)), _full((W_GROUP, 2 * W_GROUP)), _full((1, 2 * W_GROUP))],
        out_specs=pl.BlockSpec((1, S_BLOCK, W_GROUP), lambda i, j: (i, j, 0)),
        out_shape=jax.ShapeDtypeStruct((b, l, W_GROUP), BF16),
        scratch_shapes=[pltpu.VMEM((S_BLOCK, S_LANES), F32), pltpu.VMEM((S_BLOCK, S_LANES), F32),
                        pltpu.VMEM((W_GROUP, S_LANES), F32), pltpu.VMEM((W_GROUP, S_LANES), F32),
                        pltpu.VMEM((2, S_BLOCK + LANES, S_LANES), F32), pltpu.VMEM((2, S_BLOCK + LANES, S_LANES), F32),
                        pltpu.VMEM((8, S_LANES), F32)],
        compiler_params=_params("arbitrary", "arbitrary"),
        name="s5",
    )(u_s, f(a_re), f(a_im), ldt, bt(b_re), bt(b_im), cbd(c_re), cbd(c_im), f(d_skip),
      glu_w.astype(F32), f(glu_b))


def _out_proj_kernel(x_ref, ym_ref, yr_ref, yc_ref, ys_ref, w_ref, o_ref):
    acc = x_ref[...]
    for i, y_ref in enumerate((ym_ref, yr_ref, yc_ref, ys_ref)):
        acc = acc + jnp.dot(y_ref[...], w_ref[i * W_GROUP:(i + 1) * W_GROUP, :], preferred_element_type=F32)
    o_ref[...] = acc


def _out_proj(x2, ys, w_out, tm=512):
    t, d = x2.shape
    yspec = pl.BlockSpec((tm, W_GROUP), lambda i: (i, 0))
    return pl.pallas_call(
        _out_proj_kernel,
        grid=(t // tm,),
        in_specs=[pl.BlockSpec((tm, d), lambda i: (i, 0)), yspec, yspec, yspec, yspec, _full(w_out.shape)],
        out_specs=pl.BlockSpec((tm, d), lambda i: (i, 0)),
        out_shape=jax.ShapeDtypeStruct((t, d), F32),
        compiler_params=_params("parallel"),
        name="out_proj",
    )(x2, *ys, w_out)


def _moe_kernel(x_ref, nw_ref, wr_ref, br_ref, wg_ref, wu_ref, wd_ref, fw_ref, o_ref,
                t_scr, cw_scr, rk_scr, rkt_scr, cnt_scr, acc_scr, *, final_norm):
    g = pl.program_id(1)
    tm = x_ref.shape[0]
    cap = MOE_CAP
    lane = lax.broadcasted_iota(I32, (tm, LANES), 1)

    @pl.when(g == 0)
    def _():
        x = x_ref[...]
        t = x * lax.rsqrt(jnp.mean(x * x, axis=-1, keepdims=True) + NORM_EPS) * nw_ref[...]
        t_scr[...] = t.astype(BF16)
        logits = _mm(t, wr_ref[...], precise=True) + br_ref[...]
        big = jnp.int32(LANES)
        is_g = lane < E_GROUPS
        gl = jnp.where(is_g, logits, -jnp.inf)
        gmax = jnp.max(gl, axis=1, keepdims=True)
        g_sel = jnp.min(jnp.where(jnp.logical_and(is_g, gl == gmax), lane, big), axis=1, keepdims=True)
        g_gate = 1.0 / jnp.sum(jnp.exp(gl - gmax), axis=1, keepdims=True)
        lo = E_GROUPS + g_sel * E_PER_GROUP
        in_g = jnp.logical_and(lane >= lo, lane < lo + E_PER_GROUP)
        el = jnp.where(in_g, logits, -jnp.inf)
        m1 = jnp.max(el, axis=1, keepdims=True)
        i1 = jnp.min(jnp.where(el == m1, lane, big), axis=1, keepdims=True)
        el2 = jnp.where(lane == i1, -jnp.inf, el)
        m2 = jnp.max(el2, axis=1, keepdims=True)
        i2 = jnp.min(jnp.where(el2 == m2, lane, big), axis=1, keepdims=True)
        e2 = jnp.exp(m2 - m1)
        w1 = 1.0 / (1.0 + e2)
        w2 = e2 / (1.0 + e2)
        cw = jnp.where(lane == i1, w1, jnp.where(lane == i2, w2, 0.0)) * g_gate
        cw_hi = cw.astype(BF16)
        cw_scr[...] = jnp.concatenate([cw_hi, (cw - cw_hi.astype(F32)).astype(BF16)], axis=1)
        member = lane == g_sel
        row = lax.broadcasted_iota(I32, (tm, tm), 0)
        col = lax.broadcasted_iota(I32, (tm, tm), 1)
        before = jnp.where(row > col, 1.0, 0.0).astype(BF16)
        rank = jnp.dot(before, jnp.where(member, 1.0, 0.0).astype(BF16), preferred_element_type=F32)
        rk = jnp.where(member, rank, -1.0)
        rk_scr[...] = rk
        rkt_scr[...] = rk.T[0:8, :]
        cnt_scr[0:1, :] = jnp.sum(jnp.where(member, 1.0, 0.0), axis=0, keepdims=True)
        acc_scr[...] = jnp.zeros_like(acc_scr)

    lane1 = lax.broadcasted_iota(I32, (1, LANES), 1)
    n_tok = jnp.sum(jnp.where(lane1 == g, cnt_scr[0:1, :], 0.0)).astype(I32)
    rk_row = rkt_scr[pl.ds(g, 1), :]
    rk_col = jnp.sum(jnp.where(lane == g, rk_scr[...], 0.0), axis=1, keepdims=True)
    slot_r = lax.broadcasted_iota(I32, (cap, tm), 0).astype(F32)
    slot_c = lax.broadcasted_iota(I32, (tm, cap), 1).astype(F32)
    lane_c = lax.broadcasted_iota(I32, (cap, LANES), 1)

    def one_pass(blk, carry):
        base = (blk * cap).astype(F32)
        gather = jnp.where(rk_row - base == slot_r, 1.0, 0.0).astype(BF16)
        scatter = jnp.where(rk_col - base == slot_c, 1.0, 0.0).astype(BF16)
        xg = jnp.dot(gather, t_scr[...], preferred_element_type=F32).astype(BF16)
        cwg = jnp.dot(gather, cw_scr[...], preferred_element_type=F32)
        cwg = cwg[:, 0:LANES] + cwg[:, LANES:]
        hid = _silu(jnp.dot(xg, wg_ref[0], preferred_element_type=F32)) * jnp.dot(xg, wu_ref[0], preferred_element_type=F32)
        parts = []
        for e in range(E_PER_GROUP):
            w_e = jnp.sum(jnp.where(lane_c == E_GROUPS + g * E_PER_GROUP + e, cwg, 0.0), axis=1, keepdims=True)
            parts.append((hid[:, e * D_EXPERT:(e + 1) * D_EXPERT] * w_e).astype(BF16))
        y = jnp.dot(jnp.concatenate(parts, axis=1), wd_ref[0], preferred_element_type=F32)
        acc_scr[...] += jnp.dot(scatter, y.astype(BF16), preferred_element_type=F32)
        return carry

    lax.fori_loop(0, (n_tok + cap - 1) // cap, one_pass, 0)

    @pl.when(g == pl.num_programs(1) - 1)
    def _():
        y = x_ref[...] + acc_scr[...]
        if final_norm:
            y = y * lax.rsqrt(jnp.mean(y * y, axis=-1, keepdims=True) + NORM_EPS) * fw_ref[...]
        o_ref[...] = y


def _moe(x2, norm_w, wr_g, br_g, wr_e, br_e, w_gate, w_up, w_down, final_w, final_norm, tm=1024):
    t, d = x2.shape
    wr = jnp.zeros((d, LANES), F32).at[:, 0:E_GROUPS].set(wr_g.astype(F32))
    wr = wr.at[:, E_GROUPS:E_GROUPS + N_EXPERTS].set(wr_e.astype(F32))
    br = jnp.zeros((1, LANES), F32).at[0, 0:E_GROUPS].set(br_g.astype(F32))
    br = br.at[0, E_GROUPS:E_GROUPS + N_EXPERTS].set(br_e.astype(F32))
    wide = E_PER_GROUP * D_EXPERT
    wg = jnp.transpose(w_gate, (0, 2, 1, 3)).reshape(E_GROUPS, d, wide).astype(BF16)
    wu = jnp.transpose(w_up, (0, 2, 1, 3)).reshape(E_GROUPS, d, wide).astype(BF16)
    wd = w_down.reshape(E_GROUPS, wide, d).astype(BF16)
    kern = functools.partial(_moe_kernel, final_norm=final_norm)
    return pl.pallas_call(
        kern,
        grid=(t // tm, E_GROUPS),
        in_specs=[pl.BlockSpec((tm, d), lambda i, g: (i, 0)), _full((1, d)), _full((d, LANES)), _full((1, LANES)),
                  pl.BlockSpec((1, d, wide), lambda i, g: (g, 0, 0)),
                  pl.BlockSpec((1, d, wide), lambda i, g: (g, 0, 0)),
                  pl.BlockSpec((1, wide, d), lambda i, g: (g, 0, 0)), _full((1, d))],
        out_specs=pl.BlockSpec((tm, d), lambda i, g: (i, 0)),
        out_shape=jax.ShapeDtypeStruct((t, d), F32),
        scratch_shapes=[pltpu.VMEM((tm, d), BF16), pltpu.VMEM((tm, 2 * LANES), BF16), pltpu.VMEM((tm, LANES), F32),
                        pltpu.VMEM((8, tm), F32), pltpu.VMEM((8, LANES), F32), pltpu.VMEM((tm, d), F32)],
        compiler_params=_params("parallel", "arbitrary"),
        name="moe",
    )(x2, norm_w.astype(F32)[None, :], wr, br, wg, wu, wd, final_w.astype(F32)[None, :])


def _pad_in_proj(w):
    d = w.shape[0]
    m_in, r_in = 772, 896
    c_used = 680
    z = lambda k: jnp.zeros((d, k), w.dtype)
    wm = w[:, 0:m_in]
    wr = w[:, m_in:m_in + r_in]
    wc = w[:, m_in + r_in:m_in + r_in + c_used]
    ws = w[:, m_in + r_in + c_used:]
    return jnp.concatenate([wm, z(M_COLS - m_in), wr, wc, z(C_COLS - c_used), ws], axis=1).astype(BF16)


def kernel(x, norm1_w, w_in, m_conv_w, m_conv_b, m_dt_bias, m_a_log, m_d, m_norm_w, r_mu, r_w0, r_w2, r_a0, r_a2, r_g2, r_k_k, r_k_a, r_r_k, r_ln_w, r_ln_b, c_kv_norm_w, c_w_uk, c_w_uv, rel_bias, s_a_re, s_a_im, s_b_re, s_b_im, s_c_re, s_c_im, s_d, s_log_dt, s_glu_w, s_glu_b, w_out, norm2_w, moe_wr_group, moe_br_group, moe_wr_exp, moe_br_exp, moe_w_gate, moe_w_up, moe_w_down, final_norm_w):
    bsz, seq, d = x.shape
    depth = w_in.shape[0]
    x2 = x.astype(F32).reshape(bsz * seq, d)
    for i in range(depth):
        u_m, u_r, u_c, u_s = _in_proj(x2, norm1_w[i].astype(F32)[None, :], _pad_in_proj(w_in[i]))
        sh = lambda t: t.reshape(bsz, seq, t.shape[-1])
        y_m = _mamba(sh(u_m), m_conv_w[i], m_conv_b[i], m_dt_bias[i], m_a_log[i], m_d[i], m_norm_w[i])
        y_r = _rwkv(sh(u_r), r_mu[i], r_w0[i], r_w2[i], r_a0[i], r_a2[i], r_g2[i],
                    r_k_k[i], r_k_a[i], r_r_k[i], r_ln_w[i], r_ln_b[i])
        y_c = _dsa(sh(u_c), c_kv_norm_w[i], c_w_uk[i], c_w_uv[i], rel_bias)
        y_s = _s5(sh(u_s), s_a_re[i], s_a_im[i], s_b_re[i], s_b_im[i], s_c_re[i], s_c_im[i],
                  s_d[i], s_log_dt[i], s_glu_w[i], s_glu_b[i])
        ys = [t.reshape(bsz * seq, W_GROUP) for t in (y_m, y_r, y_c, y_s)]
        x2 = _out_proj(x2, ys, w_out[i].astype(BF16))
        x2 = _moe(x2, norm2_w[i], moe_wr_group[i], moe_br_group[i], moe_wr_exp[i], moe_br_exp[i],
                  moe_w_gate[i], moe_w_up[i], moe_w_down[i], final_norm_w, final_norm=(i == depth - 1))
    return x2.reshape(bsz, seq, d).astype(x.dtype)
```

```python
import functools
import math

import jax
import jax.numpy as jnp
from jax import lax
from jax.experimental import pallas as pl
from jax.experimental.pallas import tpu as pltpu

F32 = jnp.float32
BF16 = jnp.bfloat16
I32 = jnp.int32
HIGHEST = lax.Precision.HIGHEST

LANES = 128
VMEM_LIMIT = 56 * 1024 * 1024

NORM_EPS = 1e-6
W_GROUP = 256
HEAD = 64
N_HEADS = W_GROUP // HEAD
M_CONV_CH = 512
M_CONV = 4
M_COLS = 896
R_COLS = 896
C_COLS = 768
S_COLS = 256
SSD_CHUNK = 128
R_CHUNK = 64
R_BLOCK = 256
R_GN_EPS = 64e-5
I_HEADS = 8
I_DIM = 32
Q_BLOCK = 128
DSA_WIDE = 512
BIT_ROWS = 256
DSA_AUX = 4
TOPK_MAX = 256
REL_BUCKETS = 32
REL_MAX_DIST = 128
S_GROUPS = 16
S_GROUP_CH = 16
S_STATE = 64
S_LANES = S_GROUPS * S_STATE
S_BLOCK = 256
S_WIN = 8
E_GROUPS = 4
E_PER_GROUP = 8
N_EXPERTS = E_GROUPS * E_PER_GROUP
D_EXPERT = 256
MOE_CAP = 288
INT_MIN = -2 ** 31
NEG = -1e30


def _mm(a, b, precise=False):
    if precise:
        return jnp.dot(a.astype(F32), b.astype(F32), precision=HIGHEST, preferred_element_type=F32)
    return jnp.dot(a.astype(BF16), b.astype(BF16), preferred_element_type=F32)


def _mm_nt(a, b, precise=False):
    dn = (((1,), (1,)), ((), ()))
    if precise:
        return lax.dot_general(a.astype(F32), b.astype(F32), dn, precision=HIGHEST, preferred_element_type=F32)
    return lax.dot_general(a.astype(BF16), b.astype(BF16), dn, preferred_element_type=F32)


def _mm_tn(a, b, precise=False):
    dn = (((0,), (0,)), ((), ()))
    if precise:
        return lax.dot_general(a.astype(F32), b.astype(F32), dn, precision=HIGHEST, preferred_element_type=F32)
    return lax.dot_general(a.astype(BF16), b.astype(BF16), dn, preferred_element_type=F32)


def _mm_split(a, b, split_rhs=False):
    x = b if split_rhs else a
    hi = x.astype(BF16)
    lo = (x - hi.astype(F32)).astype(BF16)
    if split_rhs:
        a = a.astype(BF16)
        return jnp.dot(a, hi, preferred_element_type=F32) + jnp.dot(a, lo, preferred_element_type=F32)
    b = b.astype(BF16)
    return jnp.dot(hi, b, preferred_element_type=F32) + jnp.dot(lo, b, preferred_element_type=F32)


def _i32(v):
    return jnp.int32(v - (1 << 32) if v >= (1 << 31) else v)


def _bit_transpose(words):
    a = list(words)
    j, mask = 16, 0x0000FFFF
    while j:
        k = 0
        while k < 32:
            t = (a[k] ^ lax.shift_right_logical(a[k + j], jnp.int32(j))) & _i32(mask)
            a[k] = a[k] ^ t
            a[k + j] = a[k + j] ^ lax.shift_left(t, jnp.int32(j))
            k = (k + j + 1) & ~j
        j >>= 1
        mask = (mask ^ (mask << j)) & 0xFFFFFFFF
    return a


def _silu(x):
    return x * jax.nn.sigmoid(x)


def _softplus(x):
    return jnp.maximum(x, 0.0) + jnp.log(1.0 + jnp.exp(-jnp.abs(x)))


def _params(*sem):
    return pltpu.CompilerParams(dimension_semantics=sem, vmem_limit_bytes=VMEM_LIMIT)


def _full(shape):
    nd = len(shape)
    return pl.BlockSpec(shape, lambda *_: (0,) * nd)


def _in_proj_kernel(x_ref, nw_ref, w_ref, om_ref, or_ref, oc_ref, os_ref):
    x = x_ref[...]
    h = x * lax.rsqrt(jnp.mean(x * x, axis=-1, keepdims=True) + NORM_EPS) * nw_ref[...]
    h = h.astype(BF16)
    off = 0
    for o_ref, width in ((om_ref, M_COLS), (or_ref, R_COLS), (oc_ref, C_COLS), (os_ref, S_COLS)):
        o_ref[...] = jnp.dot(h, w_ref[:, off:off + width], preferred_element_type=F32)
        off += width


def _in_proj(x2, norm_w, w_pad, tm=512):
    t, d = x2.shape
    n = w_pad.shape[1]
    return pl.pallas_call(
        _in_proj_kernel,
        grid=(t // tm,),
        in_specs=[pl.BlockSpec((tm, d), lambda i: (i, 0)), _full((1, d)), _full((d, n))],
        out_specs=[pl.BlockSpec((tm, c), lambda i: (i, 0)) for c in (M_COLS, R_COLS, C_COLS, S_COLS)],
        out_shape=[jax.ShapeDtypeStruct((t, c), F32) for c in (M_COLS, R_COLS, C_COLS, S_COLS)],
        compiler_params=_params("parallel"),
        name="in_proj",
    )(x2, norm_w, w_pad)


def _mamba_kernel(u_ref, cw_ref, cb_ref, dtb_ref, alog_ref, dsk_ref, nw_ref, o_ref, xbuf, state):
    c = pl.program_id(1)
    n = SSD_CHUNK

    @pl.when(c == 0)
    def _():
        xbuf[0:8, :] = jnp.zeros((8, M_CONV_CH), F32)
        state[...] = jnp.zeros_like(state)

    u = u_ref[0]
    z = u[:, 0:W_GROUP]
    dtr = u[:, W_GROUP + M_CONV_CH:M_COLS]
    xbuf[8:8 + n, :] = u[:, W_GROUP:W_GROUP + M_CONV_CH]
    acc = jnp.broadcast_to(cb_ref[...], (n, M_CONV_CH))
    for j in range(M_CONV):
        acc = acc + cw_ref[j:j + 1, :] * xbuf[pl.ds(8 - (M_CONV - 1) + j, n), :]
    xbuf[0:8, :] = xbuf[n:n + 8, :]
    xc = _silu(acc)
    xs = xc[:, 0:W_GROUP]
    bm = xc[:, W_GROUP:W_GROUP + 2 * HEAD]
    cm = xc[:, W_GROUP + 2 * HEAD:]

    dt = _softplus(dtr + dtb_ref[...])
    a = dt * (-jnp.exp(alog_ref[...]))
    row = lax.broadcasted_iota(I32, (n, n), 0)
    col = lax.broadcasted_iota(I32, (n, n), 1)
    causal = row >= col
    a_cum = _mm(causal.astype(F32), a, precise=True)
    a_cum_t = a_cum.T
    a_last = a_cum[n - 1:n, :]

    ys = []
    for h in range(N_HEADS):
        g = h // 2
        b_h = bm[:, g * HEAD:(g + 1) * HEAD]
        c_h = cm[:, g * HEAD:(g + 1) * HEAD]
        x_h = xs[:, h * HEAD:(h + 1) * HEAD]
        xdt = x_h * dt[:, h:h + 1]
        ac = a_cum[:, h:h + 1]
        dec = jnp.exp(jnp.where(causal, ac - a_cum_t[h:h + 1, :], -jnp.inf))
        scores = _mm_nt(c_h, b_h) * dec
        y = _mm(scores, xdt)
        st = state[h]
        y = y + _mm_nt(c_h, st) * jnp.exp(ac)
        al = a_last[:, h:h + 1]
        state[h] = st * jnp.exp(al) + _mm_tn(xdt, b_h * jnp.exp(al - ac))
        ys.append(y + x_h * dsk_ref[:, h * HEAD:(h + 1) * HEAD])
    y = jnp.concatenate(ys, axis=1) * _silu(z)
    y = y * lax.rsqrt(jnp.mean(y * y, axis=-1, keepdims=True) + NORM_EPS) * nw_ref[...]
    o_ref[0] = y.astype(o_ref.dtype)


def _mamba(u_m, conv_w, conv_b, dt_bias, a_log, d_skip, norm_w):
    b, l, _ = u_m.shape
    pad = lambda v: jnp.pad(v.astype(F32), (0, LANES - v.shape[0]))[None, :]
    return pl.pallas_call(
        _mamba_kernel,
        grid=(b, l // SSD_CHUNK),
        in_specs=[pl.BlockSpec((1, SSD_CHUNK, M_COLS), lambda i, j: (i, j, 0)),
                  _full((M_CONV, M_CONV_CH)), _full((1, M_CONV_CH)), _full((1, LANES)), _full((1, LANES)),
                  _full((1, W_GROUP)), _full((1, W_GROUP))],
        out_specs=pl.BlockSpec((1, SSD_CHUNK, W_GROUP), lambda i, j: (i, j, 0)),
        out_shape=jax.ShapeDtypeStruct((b, l, W_GROUP), BF16),
        scratch_shapes=[pltpu.VMEM((SSD_CHUNK + 8, M_CONV_CH), F32), pltpu.VMEM((N_HEADS, HEAD, HEAD), F32)],
        compiler_params=_params("parallel", "arbitrary"),
        name="mamba_ssd",
    )(u_m, conv_w.astype(F32), conv_b.astype(F32)[None, :], pad(dt_bias), pad(a_log),
      jnp.repeat(d_skip.astype(F32), HEAD)[None, :], norm_w.astype(F32)[None, :])


def _rwkv_kernel(u_ref, mu_ref, wl_ref, w0_ref, a0_ref, kk_ref, ka_ref, rk_ref, lnw_ref, lnb_ref, seg_ref,
                 o_ref, ubuf, state):
    c = pl.program_id(1)
    n = R_BLOCK
    cs = R_CHUNK

    @pl.when(c == 0)
    def _():
        ubuf[0:8, :] = jnp.zeros((8, R_COLS), F32)
        state[...] = jnp.zeros_like(state)

    u = u_ref[0]
    ubuf[8:8 + n, :] = u
    u = u + mu_ref[...] * (ubuf[pl.ds(7, n), :] - u)
    ubuf[0:8, :] = ubuf[n:n + 8, :]
    r = u[:, 0:W_GROUP]
    k = u[:, W_GROUP:2 * W_GROUP]
    v = u[:, 2 * W_GROUP:3 * W_GROUP]
    lo = u[:, 3 * W_GROUP:]
    lane = lax.broadcasted_iota(I32, lo.shape, 1)
    lo = jnp.where(lane < 32, jnp.tanh(lo), jnp.where(lane < 64, lo, jax.nn.sigmoid(lo)))
    proj = _mm(lo, wl_ref[...])
    w_log = -_softplus(-(w0_ref[...] + proj[:, 0:W_GROUP])) - 0.5
    ld = -jnp.exp(w_log)
    a_lr = jax.nn.sigmoid(a0_ref[...] + proj[:, W_GROUP:2 * W_GROUP])
    g = proj[:, 2 * W_GROUP:]
    seg = seg_ref[...]
    kk = k * kk_ref[...]
    kk = kk * lax.rsqrt(jnp.maximum(_mm_split(kk * kk, seg), 1e-24))
    k = k * (1.0 + (a_lr - 1.0) * ka_ref[...])
    a_v = -kk
    b_v = kk * a_lr

    row = lax.broadcasted_iota(I32, (n, n), 0)
    col = lax.broadcasted_iota(I32, (n, n), 1)
    tri2 = ((row >= col) & ((row // cs) == (col // cs))).astype(F32)
    lw = _mm_split(tri2, ld, split_rhs=True)
    w_inc = jnp.exp(lw)
    w_inv = jnp.exp(-lw)
    rt = r * w_inc
    at = a_v * jnp.exp(lw - ld)
    bt = b_v * w_inv
    kt = k * w_inv

    wg = W_GROUP
    r2 = lax.broadcasted_iota(I32, (wg, wg), 0)
    c2 = lax.broadcasted_iota(I32, (wg, wg), 1)
    lower = (r2 % cs) >= (c2 % cs)
    strict = (r2 % cs) > (c2 % cs)
    eye = (r2 == c2).astype(F32)
    lane_head = lax.broadcasted_iota(I32, (cs, wg), 1) // HEAD

    def blocks(xc, dtype=BF16):
        return jnp.concatenate([jnp.where(lane_head == h, xc, 0.0) for h in range(N_HEADS)], axis=0).astype(dtype)

    s_cur = state[...]
    outs = []
    for ci in range(n // cs):
        rows = slice(ci * cs, (ci + 1) * cs)
        lw_end = lw[ci * cs + cs - 1:ci * cs + cs, :]
        ratio = jnp.exp(lw_end - lw[rows, :])
        at_b, rt_b, bt_b, kt_b = blocks(at[rows, :]), blocks(rt[rows, :], F32), blocks(bt[rows, :]), blocks(kt[rows, :])
        v_b = blocks(v[rows, :])
        bh = blocks(b_v[rows, :] * ratio)
        kh = blocks(k[rows, :] * ratio)
        a_ab = jnp.where(strict, _mm_nt(at_b, bt_b), 0.0)
        a_ak = jnp.where(strict, _mm_nt(at_b, kt_b), 0.0)
        a_rb = jnp.where(lower, _mm_nt(rt_b, bt_b), 0.0)
        a_rk = jnp.where(lower, _mm_nt(rt_b, kt_b), 0.0)
        tinv = eye + a_ab
        pw = a_ab
        for _ in range(int(math.log2(cs)) - 1):
            pw = _mm(pw, pw)
            tinv = tinv + _mm(pw, tinv)
        ta = _mm(tinv, at_b)
        pv = _mm(tinv, _mm(a_ak, v_b))
        q_eff = rt_b + _mm(a_rb, ta)
        o_v = _mm(a_rb, pv) + _mm(a_rk, v_b)
        s_v = _mm_tn(pv, bh) + _mm_tn(v_b, kh)
        o_b = _mm_nt(q_eff, s_cur) + o_v
        outs.append(o_b[0:cs] + o_b[cs:2 * cs] + o_b[2 * cs:3 * cs] + o_b[3 * cs:])
        s_cur = s_cur * jnp.exp(lw_end) + _mm_split(s_cur, _mm_tn(ta, bh)) + s_v
    state[...] = s_cur
    o = jnp.concatenate(outs, axis=0)

    inv = 1.0 / HEAD
    mean = _mm_split(o, seg) * inv
    d = o - mean
    var = _mm_split(d * d, seg) * inv
    o = d * lax.rsqrt(var + R_GN_EPS) * lnw_ref[...] + lnb_ref[...]
    o = o + _mm_split(r * k * rk_ref[...], seg) * v
    o_ref[0] = (o * g).astype(o_ref.dtype)


def _rwkv(u_r, mu, w0, w2, a0, a2, g2, k_k, k_a, r_k, ln_w, ln_b):
    b, l, _ = u_r.shape
    f = lambda t: t.astype(F32).reshape(1, -1)
    w_lora = jnp.zeros((LANES, 3 * W_GROUP), F32)
    w_lora = w_lora.at[0:32, 0:W_GROUP].set(w2.astype(F32))
    w_lora = w_lora.at[32:64, W_GROUP:2 * W_GROUP].set(a2.astype(F32))
    w_lora = w_lora.at[64:128, 2 * W_GROUP:].set(g2.astype(F32))
    hid = jnp.arange(W_GROUP) // HEAD
    seg = (hid[:, None] == hid[None, :]).astype(F32)
    vec = _full((1, W_GROUP))
    return pl.pallas_call(
        _rwkv_kernel,
        grid=(b, l // R_BLOCK),
        in_specs=[pl.BlockSpec((1, R_BLOCK, R_COLS), lambda i, j: (i, j, 0)),
                  _full((1, R_COLS)), _full((LANES, 3 * W_GROUP)),
                  vec, vec, vec, vec, vec, vec, vec, _full((W_GROUP, W_GROUP))],
        out_specs=pl.BlockSpec((1, R_BLOCK, W_GROUP), lambda i, j: (i, j, 0)),
        out_shape=jax.ShapeDtypeStruct((b, l, W_GROUP), BF16),
        scratch_shapes=[pltpu.VMEM((R_BLOCK + 8, R_COLS), F32), pltpu.VMEM((W_GROUP, W_GROUP), F32)],
        compiler_params=_params("parallel", "arbitrary"),
        name="rwkv7",
    )(u_r, f(mu), w_lora, f(w0), f(a0), f(k_k), f(k_a), f(r_k), f(ln_w), f(ln_b), seg)


def _dsa_kernel(u_ref, kvw_ref, wkv_ref, bias_ref, far_ref, o_ref,
                kt_scr, kv_scr, tail_scr, key_scr, plane_scr, mask_scr, acc_scr, mx_scr, l_scr, cut_scr, *, top_k):
    qb = pl.program_id(1)
    n = Q_BLOCK
    wide = DSA_WIDE
    seq = key_scr.shape[0] - wide

    @pl.when(qb == 0)
    def _():
        kt_scr[...] = jnp.zeros_like(kt_scr)
        kv_scr[...] = jnp.zeros_like(kv_scr)
        tail_scr[...] = jnp.zeros_like(tail_scr)
        key_scr[...] = jnp.full(key_scr.shape, INT_MIN, I32)
        plane_scr[...] = jnp.zeros_like(plane_scr)
        mask_scr[...] = jnp.full(mask_scr.shape, NEG, F32)

    u = u_ref[0]
    q = u[:, 0:W_GROUP]
    ckv = u[:, W_GROUP:W_GROUP + LANES]
    qi = u[:, W_GROUP + LANES:2 * W_GROUP + LANES]
    tail = u[:, 2 * W_GROUP + LANES:]
    ckv = ckv * lax.rsqrt(jnp.mean(ckv * ckv, axis=-1, keepdims=True) + NORM_EPS) * kvw_ref[...]
    kv = _mm(ckv, wkv_ref[...])
    start = pl.multiple_of(qb * n, n)
    end = pl.multiple_of(start + wide + n, n)
    sub = lax.broadcasted_iota(I32, (HEAD, n), 0)
    kt_scr[:, pl.ds(end - n, n)] = jnp.concatenate(
        [kv.T[0:HEAD, :], jnp.where(sub < DSA_AUX, 1.0, 0.0)], axis=0).astype(BF16)
    kv_scr[pl.ds(end - n, n), :] = kv.astype(BF16)
    tail_scr[pl.ds(end - n, n), :] = tail.astype(BF16)
    qi_t = qi.T
    w_qi = jnp.concatenate(
        [jnp.concatenate([qi_t[h * I_DIM:(h + 1) * I_DIM, :] for h in range(I_HEADS)], axis=1),
         jnp.zeros((LANES - I_DIM, I_HEADS * n), F32)], axis=0).astype(BF16)
    wi_t = tail.T[I_DIM:I_DIM + I_HEADS, :] * (I_HEADS ** -0.5 * I_DIM ** -0.5)

    q_pos = start + lax.broadcasted_iota(I32, (wide, n), 1)
    row_w = lax.broadcasted_iota(I32, (wide, n), 0)
    n_wide = (start + n + wide - 1) // wide

    def tile_start(i):
        return pl.multiple_of(end - (i + 1) * wide, n)

    def idx_tile(i, carry):
        ks = tile_start(i)
        s_all = jnp.dot(tail_scr[pl.ds(ks, wide), :], w_qi, preferred_element_type=F32)
        s = jnp.zeros((wide, n), F32)
        for h in range(I_HEADS):
            s = s + jnp.maximum(s_all[:, h * n:(h + 1) * n], 0.0) * wi_t[h:h + 1, :]
        s = s + 0.0
        bits = pltpu.bitcast(s, I32)
        key = jnp.where(bits < 0, bits ^ jnp.int32(0x7FFFFFFF), bits)
        idx = ks - wide + row_w
        key = jnp.where(jnp.logical_and(idx <= q_pos, idx >= 0), key, jnp.int32(INT_MIN))
        key_scr[pl.ds(ks, wide), :] = key
        ukey = key ^ jnp.int32(INT_MIN)
        for blk in range(wide // BIT_ROWS):
            words = _bit_transpose([ukey[blk * BIT_ROWS + 8 * k:blk * BIT_ROWS + 8 * k + 8, :] for k in range(32)])
            row0 = pl.multiple_of((i * (wide // BIT_ROWS) + blk) * 8, 8)
            for b in range(32):
                plane_scr[b, pl.ds(row0, 8), :] = words[31 - b]
        return carry

    lax.fori_loop(0, n_wide, idx_tile, 0)

    part = 64

    def count(pred):
        def body(i, acc):
            ks = tile_start(i)
            kk = key_scr[pl.ds(ks, wide), :]
            hit = pred(kk, ks - wide + row_w).astype(I32)
            for c in range(wide // part):
                acc = acc + hit[c * part:(c + 1) * part, :]
            return acc
        acc = lax.fori_loop(0, n_wide, body, jnp.zeros((part, n), I32))
        return jnp.sum(acc, axis=0, keepdims=True)

    blk_id = lax.broadcasted_iota(I32, (plane_scr.shape[1], n), 0) // 8
    live = jnp.where(blk_id < n_wide * (wide // BIT_ROWS), jnp.int32(-1), jnp.int32(0))
    above = jnp.zeros((1, n), I32)
    thr = jnp.zeros((1, n), I32)
    for b in range(31, -1, -1):
        ones = live & plane_scr[b]
        c1 = jnp.sum(lax.population_count(ones), axis=0, keepdims=True)
        take = above + c1 >= top_k
        thr = jnp.where(take, thr | _i32(1 << b), thr)
        above = jnp.where(take, above, above + c1)
        live = jnp.where(take, ones, live ^ ones)
    thr = thr ^ jnp.int32(INT_MIN)
    n_ge = above + jnp.sum(lax.population_count(live), axis=0, keepdims=True)
    tied = jnp.logical_and(n_ge > top_k, thr != INT_MIN)
    cut_scr[0:1, :] = jnp.full((1, n), seq, I32)

    @pl.when(jnp.max(tied.astype(I32)) > 0)
    def _():
        need = top_k - above
        m = jnp.zeros((1, n), I32)
        for bit in range(int(math.log2(seq)) - 1, -1, -1):
            cand = m | jnp.int32(1 << bit)
            cnt = count(lambda kk, idx: jnp.logical_and(kk == thr, idx < cand))
            m = jnp.where(cnt < need, cand, m)
        cut_scr[0:1, :] = jnp.where(tied, m, seq)

    cut = cut_scr[0:1, :]

    def mask_tile(i, carry):
        ks = tile_start(i)
        kk = key_scr[pl.ds(ks, wide), :]
        sel = jnp.logical_or(kk > thr, jnp.logical_and(kk == thr, ks - wide + row_w <= cut))
        sel = jnp.logical_and(sel, kk != INT_MIN)
        m_t = jnp.where(sel, 0.0, NEG)
        for c in range(wide // n):
            mask_scr[:, pl.ds(ks + c * n, n)] = m_t[c * n:(c + 1) * n, :].T
        return carry

    lax.fori_loop(0, n_wide, mask_tile, 0)

    lane = lax.broadcasted_iota(I32, (n, LANES), 1)

    def split(x):
        hi = x.astype(BF16).astype(F32)
        return hi, x - hi

    def q_operand(shift, far):
        rows = []
        for h in range(N_HEADS):
            op = jnp.concatenate([q[:, h * HEAD:(h + 1) * HEAD] * (HEAD ** -0.5), jnp.zeros((n, HEAD), F32)], axis=1)
            if shift is not None:
                hi, lo = split(shift[h])
                op = op + jnp.where(lane == HEAD, hi, 0.0) + jnp.where(lane == HEAD + 1, lo, 0.0)
            if far:
                hi, lo = split(jnp.full((n, 1), far_ref[h], F32))
                op = op + jnp.where(lane == HEAD + 2, hi, 0.0) + jnp.where(lane == HEAD + 3, lo, 0.0)
            rows.append(op)
        return jnp.concatenate(rows, axis=0).astype(BF16)

    def logits(i, q_op, near):
        ks = tile_start(i)
        s_all = jnp.dot(q_op, kt_scr[:, pl.ds(ks, wide)], preferred_element_type=F32)
        msk = mask_scr[:, pl.ds(ks, wide)]
        out = []
        for h in range(N_HEADS):
            s = s_all[h * n:(h + 1) * n, :] + msk
            out.append(s + bias_ref[h] if near else s)
        return ks, out

    def max_tile(i, q_op, near):
        _, ss = logits(i, q_op, near)
        for h in range(N_HEADS):
            m = mx_scr[h * n:(h + 1) * n, :]
            for c in range(wide // n):
                m = jnp.maximum(m, ss[h][:, c * n:(c + 1) * n])
            mx_scr[h * n:(h + 1) * n, :] = m

    def sum_tile(i, q_op, near):
        ks, ss = logits(i, q_op, near)
        ps = []
        for h in range(N_HEADS):
            p = jnp.exp(ss[h])
            part_l = l_scr[h * n:(h + 1) * n, :]
            for c in range(wide // n):
                part_l = part_l + p[:, c * n:(c + 1) * n]
            l_scr[h * n:(h + 1) * n, :] = part_l
            ps.append(p.astype(BF16))
        acc_scr[...] += jnp.dot(jnp.concatenate(ps, axis=0), kv_scr[pl.ds(ks, wide), :], preferred_element_type=F32)

    def run(tile_fn, shift):
        tile_fn(0, q_operand(shift, False), True)
        q_far = q_operand(shift, True)

        def body(i, carry):
            tile_fn(i, q_far, False)
            return carry
        lax.fori_loop(1, n_wide, body, 0)

    mx_scr[...] = jnp.full(mx_scr.shape, NEG, F32)
    run(max_tile, None)
    neg_max = [-jnp.max(mx_scr[h * n:(h + 1) * n, :], axis=1, keepdims=True) for h in range(N_HEADS)]
    acc_scr[...] = jnp.zeros_like(acc_scr)
    l_scr[...] = jnp.zeros_like(l_scr)
    run(sum_tile, neg_max)
    o_ref[0] = jnp.concatenate(
        [acc_scr[h * n:(h + 1) * n, HEAD:] / jnp.sum(l_scr[h * n:(h + 1) * n, :], axis=1, keepdims=True)
         for h in range(N_HEADS)], axis=1).astype(o_ref.dtype)


def _t5_bucket(dist):
    n = jnp.maximum(dist, 0)
    max_exact = REL_BUCKETS // 2
    log_ratio = jnp.log(jnp.maximum(n, 1).astype(F32) / max_exact) / math.log(REL_MAX_DIST / max_exact)
    large = jnp.minimum(max_exact + (log_ratio * (REL_BUCKETS - max_exact)).astype(I32), REL_BUCKETS - 1)
    return jnp.where(n < max_exact, n, large)


def _dsa(u_c, kv_norm_w, w_uk, w_uv, rel_bias):
    b, l, _ = u_c.shape
    top_k = min(TOPK_MAX, l // 4)
    assert DSA_WIDE - Q_BLOCK + 1 >= REL_MAX_DIST and l % DSA_WIDE == 0
    dist = (jnp.arange(Q_BLOCK, dtype=I32)[:, None] + (DSA_WIDE - Q_BLOCK)) - jnp.arange(DSA_WIDE, dtype=I32)[None, :]
    onehot = _t5_bucket(dist)[None] == jnp.arange(REL_BUCKETS, dtype=I32)[:, None, None]
    table = rel_bias.astype(F32)
    bias = jnp.stack([jnp.sum(jnp.where(onehot, table[:, h][:, None, None], 0.0), axis=0) for h in range(N_HEADS)])
    w_kv = jnp.concatenate([w_uk, w_uv], axis=1).astype(F32)
    kern = functools.partial(_dsa_kernel, top_k=top_k)
    lp = l + DSA_WIDE
    return pl.pallas_call(
        kern,
        grid=(b, l // Q_BLOCK),
        in_specs=[pl.BlockSpec((1, Q_BLOCK, C_COLS), lambda i, j: (i, j, 0)),
                  _full((1, LANES)), _full((LANES, LANES)), _full((N_HEADS, Q_BLOCK, DSA_WIDE)),
                  pl.BlockSpec(memory_space=pltpu.SMEM)],
        out_specs=pl.BlockSpec((1, Q_BLOCK, W_GROUP), lambda i, j: (i, j, 0)),
        out_shape=jax.ShapeDtypeStruct((b, l, W_GROUP), BF16),
        scratch_shapes=[pltpu.VMEM((LANES, lp), BF16), pltpu.VMEM((lp, LANES), BF16), pltpu.VMEM((lp, LANES), BF16),
                        pltpu.VMEM((lp, Q_BLOCK), I32), pltpu.VMEM((32, lp // BIT_ROWS * 8, Q_BLOCK), I32),
                        pltpu.VMEM((Q_BLOCK, lp), F32),
                        pltpu.VMEM((N_HEADS * Q_BLOCK, LANES), F32), pltpu.VMEM((N_HEADS * Q_BLOCK, LANES), F32),
                        pltpu.VMEM((N_HEADS * Q_BLOCK, LANES), F32), pltpu.VMEM((8, Q_BLOCK), I32)],
        compiler_params=_params("parallel", "arbitrary"),
        name="dsa",
    )(u_c, kv_norm_w.astype(F32)[None, :], w_kv, bias, table[REL_BUCKETS - 1])


def _s5_kernel(u_ref, are_ref, aim_ref, ldt_ref, bre_ref, bim_ref, cre_ref, cim_ref, dsk_ref, gw_ref, gb_ref,
               o_ref, pre, pim, bdr, bdi, hre, him, cre_s, cim_s, tre_s, tim_s):
    first = jnp.logical_and(pl.program_id(0) == 0, pl.program_id(1) == 0)
    n = S_BLOCK
    pad = LANES

    @pl.when(first)
    def _():
        lam_re = jnp.minimum(are_ref[...], -1e-4)
        lam_im = aim_ref[...]
        dt = jnp.exp(ldt_ref[...])
        mag = jnp.exp(dt * lam_re)
        ab_re = mag * jnp.cos(dt * lam_im)
        ab_im = mag * jnp.sin(dt * lam_im)
        den = lam_re * lam_re + lam_im * lam_im
        f_re = ((ab_re - 1.0) * lam_re + ab_im * lam_im) / den
        f_im = (ab_im * lam_re - (ab_re - 1.0) * lam_im) / den
        b_re = bre_ref[...]
        b_im = bim_ref[...]
        bb_re = f_re * b_re - f_im * b_im
        bb_im = f_re * b_im + f_im * b_re
        rg = lax.broadcasted_iota(I32, (W_GROUP, S_LANES), 0) // S_GROUP_CH
        cg = lax.broadcasted_iota(I32, (W_GROUP, S_LANES), 1) // S_STATE
        same = rg == cg
        bdr[...] = jnp.where(same, jnp.concatenate([bb_re] * S_GROUPS, axis=0), 0.0)
        bdi[...] = jnp.where(same, jnp.concatenate([bb_im] * S_GROUPS, axis=0), 0.0)
        pre[0:1, :] = ab_re
        pim[0:1, :] = ab_im
        d = 1
        while d < S_WIN:
            sr = pre[d - 1:d, :]
            si = pim[d - 1:d, :]
            xr = pre[0:d, :]
            xi = pim[0:d, :]
            pre[d:2 * d, :] = xr * sr - xi * si
            pim[d:2 * d, :] = xr * si + xi * sr
            d *= 2
        hre[...] = jnp.zeros_like(hre)
        him[...] = jnp.zeros_like(him)

    @pl.when(pl.program_id(1) == 0)
    def _():
        cre_s[...] = jnp.zeros_like(cre_s)
        cim_s[...] = jnp.zeros_like(cim_s)
        tre_s[...] = jnp.zeros_like(tre_s)
        tim_s[...] = jnp.zeros_like(tim_s)

    u = u_ref[0]
    bu_re = _mm(u, bdr[...])
    bu_im = _mm(u, bdi[...])
    lo = pad - S_WIN
    hre[0, lo:pad, :] = tre_s[...]
    him[0, lo:pad, :] = tim_s[...]
    hre[0, pad:pad + n, :] = bu_re
    him[0, pad:pad + n, :] = bu_im
    tre_s[...] = bu_re[n - S_WIN:, :]
    tim_s[...] = bu_im[n - S_WIN:, :]

    def col_block(cb, c_):
        cs = pl.ds(pl.multiple_of(cb * LANES, LANES), LANES)
        src = 0
        d = 1
        while d < S_WIN:
            ar = pre[d - 1:d, cs]
            ai = pim[d - 1:d, cs]
            xr = hre[src, pl.ds(lo - d, n + S_WIN), cs]
            xi = him[src, pl.ds(lo - d, n + S_WIN), cs]
            hre[1 - src, lo:pad + n, cs] = hre[src, lo:pad + n, cs] + ar * xr - ai * xi
            him[1 - src, lo:pad + n, cs] = him[src, lo:pad + n, cs] + ar * xi + ai * xr
            src = 1 - src
            d *= 2
        ar = jnp.broadcast_to(pre[S_WIN - 1:S_WIN, cs], (S_WIN, LANES))
        ai = jnp.broadcast_to(pim[S_WIN - 1:S_WIN, cs], (S_WIN, LANES))
        pr = cre_s[:, cs]
        pi = cim_s[:, cs]
        for v in range(n // S_WIN):
            rows = slice(pad + v * S_WIN, pad + (v + 1) * S_WIN)
            nr = hre[src, rows, cs] + ar * pr - ai * pi
            ni = him[src, rows, cs] + ar * pi + ai * pr
            hre[1 - src, rows, cs] = nr
            him[1 - src, rows, cs] = ni
            pr, pi = nr, ni
        cre_s[:, cs] = pr
        cim_s[:, cs] = pi
        assert src == 1
        return c_

    lax.fori_loop(0, S_LANES // LANES, col_block, 0)

    h_re = hre[0, pad:pad + n, :]
    h_im = him[0, pad:pad + n, :]
    y = _mm(h_re, cre_ref[...]) - _mm(h_im, cim_ref[...]) + dsk_ref[...] * u
    y = jax.nn.gelu(y)
    gl = _mm(y, gw_ref[...]) + gb_ref[...]
    o_ref[0] = (gl[:, 0:W_GROUP] * jax.nn.sigmoid(gl[:, W_GROUP:])).astype(o_ref.dtype)


def _s5(u_s, a_re, a_im, b_re, b_im, c_re, c_im, d_skip, log_dt, glu_w, glu_b):
    b, l, _ = u_s.shape
    f = lambda t: t.astype(F32).reshape(1, -1)
    eye = jnp.eye(S_GROUPS, dtype=F32)
    bt = lambda t: jnp.transpose(t.astype(F32), (2, 0, 1)).reshape(S_GROUP_CH, S_LANES)
    cbd = lambda t: jnp.einsum('gpn,gh->gnhp', t.astype(F32), eye).reshape(S_LANES, W_GROUP)
    ldt = jnp.repeat(log_dt.astype(F32), S_STATE)[None, :]
    return pl.pallas_call(
        _s5_kernel,
        grid=(b, l // S_BLOCK),
        in_specs=[pl.BlockSpec((1, S_BLOCK, S_COLS), lambda i, j: (i, j, 0)),
                  _full((1, S_LANES)), _full((1, S_LANES)), _full((1, S_LANES)),
                  _full((S_GROUP_CH, S_LANES)), _full((S_GROUP_CH, S_LANES)),
                  _full((S_LANES, W_GROUP)), _full((S_LANES, W_GROUP)),
                  _full((1, W_GROUP)), _full((W_GROUP, 2 * W_GROUP)), _full((1, 2 * W_GROUP))],
        out_specs=pl.BlockSpec((1, S_BLOCK, W_GROUP), lambda i, j: (i, j, 0)),
        out_shape=jax.ShapeDtypeStruct((b, l, W_GROUP), BF16),
        scratch_shapes=[pltpu.VMEM((S_WIN, S_LANES), F32), pltpu.VMEM((S_WIN, S_LANES), F32),
                        pltpu.VMEM((W_GROUP, S_LANES), F32), pltpu.VMEM((W_GROUP, S_LANES), F32),
                        pltpu.VMEM((2, S_BLOCK + LANES, S_LANES), F32), pltpu.VMEM((2, S_BLOCK + LANES, S_LANES), F32),
                        pltpu.VMEM((S_WIN, S_LANES), F32), pltpu.VMEM((S_WIN, S_LANES), F32),
                        pltpu.VMEM((S_WIN, S_LANES), F32), pltpu.VMEM((S_WIN, S_LANES), F32)],
        compiler_params=_params("arbitrary", "arbitrary"),
        name="s5",
    )(u_s, f(a_re), f(a_im), ldt, bt(b_re), bt(b_im), cbd(c_re), cbd(c_im), f(d_skip),
      glu_w.astype(F32), f(glu_b))


def _out_proj_kernel(x_ref, ym_ref, yr_ref, yc_ref, ys_ref, w_ref, o_ref):
    acc = x_ref[...]
    for i, y_ref in enumerate((ym_ref, yr_ref, yc_ref, ys_ref)):
        acc = acc + jnp.dot(y_ref[...], w_ref[i * W_GROUP:(i + 1) * W_GROUP, :], preferred_element_type=F32)
    o_ref[...] = acc


def _out_proj(x2, ys, w_out, tm=512):
    t, d = x2.shape
    yspec = pl.BlockSpec((tm, W_GROUP), lambda i: (i, 0))
    return pl.pallas_call(
        _out_proj_kernel,
        grid=(t // tm,),
        in_specs=[pl.BlockSpec((tm, d), lambda i: (i, 0)), yspec, yspec, yspec, yspec, _full(w_out.shape)],
        out_specs=pl.BlockSpec((tm, d), lambda i: (i, 0)),
        out_shape=jax.ShapeDtypeStruct((t, d), F32),
        compiler_params=_params("parallel"),
        name="out_proj",
    )(x2, *ys, w_out)


def _moe_kernel(x_ref, nw_ref, wr_ref, br_ref, wg_ref, wu_ref, wd_ref, fw_ref, o_ref,
                t_scr, cw_scr, rk_scr, rkt_scr, cnt_scr, acc_scr, *, final_norm):
    g = pl.program_id(1)
    tm = x_ref.shape[0]
    cap = MOE_CAP
    lane = lax.broadcasted_iota(I32, (tm, LANES), 1)

    @pl.when(g == 0)
    def _():
        x = x_ref[...]
        t = x * lax.rsqrt(jnp.mean(x * x, axis=-1, keepdims=True) + NORM_EPS) * nw_ref[...]
        t_scr[...] = t.astype(BF16)
        logits = _mm(t, wr_ref[...], precise=True) + br_ref[...]
        big = jnp.int32(LANES)
        is_g = lane < E_GROUPS
        gl = jnp.where(is_g, logits, -jnp.inf)
        gmax = jnp.max(gl, axis=1, keepdims=True)
        g_sel = jnp.min(jnp.where(jnp.logical_and(is_g, gl == gmax), lane, big), axis=1, keepdims=True)
        g_gate = 1.0 / jnp.sum(jnp.exp(gl - gmax), axis=1, keepdims=True)
        lo = E_GROUPS + g_sel * E_PER_GROUP
        in_g = jnp.logical_and(lane >= lo, lane < lo + E_PER_GROUP)
        el = jnp.where(in_g, logits, -jnp.inf)
        m1 = jnp.max(el, axis=1, keepdims=True)
        i1 = jnp.min(jnp.where(el == m1, lane, big), axis=1, keepdims=True)
        el2 = jnp.where(lane == i1, -jnp.inf, el)
        m2 = jnp.max(el2, axis=1, keepdims=True)
        i2 = jnp.min(jnp.where(el2 == m2, lane, big), axis=1, keepdims=True)
        e2 = jnp.exp(m2 - m1)
        w1 = 1.0 / (1.0 + e2)
        w2 = e2 / (1.0 + e2)
        cw = jnp.where(lane == i1, w1, jnp.where(lane == i2, w2, 0.0)) * g_gate
        cw_hi = cw.astype(BF16)
        cw_scr[...] = jnp.concatenate([cw_hi, (cw - cw_hi.astype(F32)).astype(BF16)], axis=1)
        member = lane == g_sel
        row = lax.broadcasted_iota(I32, (tm, tm), 0)
        col = lax.broadcasted_iota(I32, (tm, tm), 1)
        before = jnp.where(row > col, 1.0, 0.0).astype(BF16)
        rank = jnp.dot(before, jnp.where(member, 1.0, 0.0).astype(BF16), preferred_element_type=F32)
        rk = jnp.where(member, rank, -1.0)
        rk_scr[...] = rk
        rkt_scr[...] = rk.T[0:8, :]
        cnt_scr[0:1, :] = jnp.sum(jnp.where(member, 1.0, 0.0), axis=0, keepdims=True)
        acc_scr[...] = jnp.zeros_like(acc_scr)

    lane1 = lax.broadcasted_iota(I32, (1, LANES), 1)
    n_tok = jnp.sum(jnp.where(lane1 == g, cnt_scr[0:1, :], 0.0)).astype(I32)
    rk_row = rkt_scr[pl.ds(g, 1), :]
    rk_col = jnp.sum(jnp.where(lane == g, rk_scr[...], 0.0), axis=1, keepdims=True)
    slot_r = lax.broadcasted_iota(I32, (cap, tm), 0).astype(F32)
    slot_c = lax.broadcasted_iota(I32, (tm, cap), 1).astype(F32)
    lane_c = lax.broadcasted_iota(I32, (cap, LANES), 1)

    def one_pass(blk, carry):
        base = (blk * cap).astype(F32)
        gather = jnp.where(rk_row - base == slot_r, 1.0, 0.0).astype(BF16)
        scatter = jnp.where(rk_col - base == slot_c, 1.0, 0.0).astype(BF16)
        xg = jnp.dot(gather, t_scr[...], preferred_element_type=F32).astype(BF16)
        cwg = jnp.dot(gather, cw_scr[...], preferred_element_type=F32)
        cwg = cwg[:, 0:LANES] + cwg[:, LANES:]
        parts = []
        for e in range(E_PER_GROUP):
            hid = _silu(jnp.dot(xg, wg_ref[0, e], preferred_element_type=F32)) * jnp.dot(xg, wu_ref[0, e], preferred_element_type=F32)
            w_e = jnp.sum(jnp.where(lane_c == E_GROUPS + g * E_PER_GROUP + e, cwg, 0.0), axis=1, keepdims=True)
            parts.append((hid * w_e).astype(BF16))
        y = jnp.dot(jnp.concatenate(parts, axis=1), wd_ref[0], preferred_element_type=F32)
        acc_scr[...] += jnp.dot(scatter, y.astype(BF16), preferred_element_type=F32)
        return carry

    lax.fori_loop(0, (n_tok + cap - 1) // cap, one_pass, 0)

    @pl.when(g == pl.num_programs(1) - 1)
    def _():
        y = x_ref[...] + acc_scr[...]
        if final_norm:
            y = y * lax.rsqrt(jnp.mean(y * y, axis=-1, keepdims=True) + NORM_EPS) * fw_ref[...]
        o_ref[...] = y


def _moe(x2, norm_w, wr_g, br_g, wr_e, br_e, w_gate, w_up, w_down, final_w, final_norm, tm=1024):
    t, d = x2.shape
    wr = jnp.zeros((d, LANES), F32).at[:, 0:E_GROUPS].set(wr_g.astype(F32))
    wr = wr.at[:, E_GROUPS:E_GROUPS + N_EXPERTS].set(wr_e.astype(F32))
    br = jnp.zeros((1, LANES), F32).at[0, 0:E_GROUPS].set(br_g.astype(F32))
    br = br.at[0, E_GROUPS:E_GROUPS + N_EXPERTS].set(br_e.astype(F32))
    wide = E_PER_GROUP * D_EXPERT
    wg = w_gate.astype(BF16)
    wu = w_up.astype(BF16)
    wd = w_down.reshape(E_GROUPS, wide, d).astype(BF16)
    kern = functools.partial(_moe_kernel, final_norm=final_norm)
    return pl.pallas_call(
        kern,
        grid=(t // tm, E_GROUPS),
        in_specs=[pl.BlockSpec((tm, d), lambda i, g: (i, 0)), _full((1, d)), _full((d, LANES)), _full((1, LANES)),
                  pl.BlockSpec((1, E_PER_GROUP, d, D_EXPERT), lambda i, g: (g, 0, 0, 0)),
                  pl.BlockSpec((1, E_PER_GROUP, d, D_EXPERT), lambda i, g: (g, 0, 0, 0)),
                  pl.BlockSpec((1, wide, d), lambda i, g: (g, 0, 0)), _full((1, d))],
        out_specs=pl.BlockSpec((tm, d), lambda i, g: (i, 0)),
        out_shape=jax.ShapeDtypeStruct((t, d), F32),
        scratch_shapes=[pltpu.VMEM((tm, d), BF16), pltpu.VMEM((tm, 2 * LANES), BF16), pltpu.VMEM((tm, LANES), F32),
                        pltpu.VMEM((8, tm), F32), pltpu.VMEM((8, LANES), F32), pltpu.VMEM((tm, d), F32)],
        compiler_params=_params("parallel", "arbitrary"),
        name="moe",
    )(x2, norm_w.astype(F32)[None, :], wr, br, wg, wu, wd, final_w.astype(F32)[None, :])


def _pad_in_proj(w):
    d = w.shape[0]
    m_in, r_in = 772, 896
    c_used = 680
    z = lambda k: jnp.zeros((d, k), w.dtype)
    wm = w[:, 0:m_in]
    wr = w[:, m_in:m_in + r_in]
    wc = w[:, m_in + r_in:m_in + r_in + c_used]
    ws = w[:, m_in + r_in + c_used:]
    return jnp.concatenate([wm, z(M_COLS - m_in), wr, wc, z(C_COLS - c_used), ws], axis=1).astype(BF16)


def kernel(x, norm1_w, w_in, m_conv_w, m_conv_b, m_dt_bias, m_a_log, m_d, m_norm_w, r_mu, r_w0, r_w2, r_a0, r_a2, r_g2, r_k_k, r_k_a, r_r_k, r_ln_w, r_ln_b, c_kv_norm_w, c_w_uk, c_w_uv, rel_bias, s_a_re, s_a_im, s_b_re, s_b_im, s_c_re, s_c_im, s_d, s_log_dt, s_glu_w, s_glu_b, w_out, norm2_w, moe_wr_group, moe_br_group, moe_wr_exp, moe_br_exp, moe_w_gate, moe_w_up, moe_w_down, final_norm_w):
    bsz, seq, d = x.shape
    depth = w_in.shape[0]
    x2 = x.astype(F32).reshape(bsz * seq, d)
    for i in range(depth):
        u_m, u_r, u_c, u_s = _in_proj(x2, norm1_w[i].astype(F32)[None, :], _pad_in_proj(w_in[i]))
        sh = lambda t: t.reshape(bsz, seq, t.shape[-1])
        y_m = _mamba(sh(u_m), m_conv_w[i], m_conv_b[i], m_dt_bias[i], m_a_log[i], m_d[i], m_norm_w[i])
        y_r = _rwkv(sh(u_r), r_mu[i], r_w0[i], r_w2[i], r_a0[i], r_a2[i], r_g2[i],
                    r_k_k[i], r_k_a[i], r_r_k[i], r_ln_w[i], r_ln_b[i])
        y_c = _dsa(sh(u_c), c_kv_norm_w[i], c_w_uk[i], c_w_uv[i], rel_bias)
        y_s = _s5(sh(u_s), s_a_re[i], s_a_im[i], s_b_re[i], s_b_im[i], s_c_re[i], s_c_im[i],
                  s_d[i], s_log_dt[i], s_glu_w[i], s_glu_b[i])
        ys = [t.reshape(bsz * seq, W_GROUP) for t in (y_m, y_r, y_c, y_s)]
        x2 = _out_proj(x2, ys, w_out[i].astype(BF16))
        x2 = _moe(x2, norm2_w[i], moe_wr_group[i], moe_br_group[i], moe_wr_exp[i], moe_br_exp[i],
                  moe_w_gate[i], moe_w_up[i], moe_w_down[i], final_norm_w, final_norm=(i == depth - 1))
    return x2.reshape(bsz, seq, d).astype(x.dtype)
```

```python
import functools
import math

import jax
import jax.numpy as jnp
from jax import lax
from jax.experimental import pallas as pl
from jax.experimental.pallas import tpu as pltpu

F32 = jnp.float32
BF16 = jnp.bfloat16
I32 = jnp.int32
HIGHEST = lax.Precision.HIGHEST

LANES = 128
VMEM_LIMIT = 56 * 1024 * 1024

NORM_EPS = 1e-6
W_GROUP = 256
HEAD = 64
N_HEADS = W_GROUP // HEAD
M_CONV_CH = 512
M_CONV = 4
M_COLS = 896
R_COLS = 896
C_COLS = 768
S_COLS = 256
SSD_CHUNK = 128
R_CHUNK = 64
R_BLOCK = 256
R_GN_EPS = 64e-5
I_HEADS = 8
I_DIM = 32
Q_BLOCK = 128
DSA_WIDE = 512
BIT_ROWS = 256
DSA_AUX = 4
TOPK_MAX = 256
REL_BUCKETS = 32
REL_MAX_DIST = 128
S_GROUPS = 16
S_GROUP_CH = 16
S_STATE = 64
S_LANES = S_GROUPS * S_STATE
S_BLOCK = 256
S_WIN = 8
E_GROUPS = 4
E_PER_GROUP = 8
N_EXPERTS = E_GROUPS * E_PER_GROUP
D_EXPERT = 256
MOE_CAP = 288
INT_MIN = -2 ** 31
NEG = -1e30


def _mm(a, b, precise=False):
    if precise:
        return jnp.dot(a.astype(F32), b.astype(F32), precision=HIGHEST, preferred_element_type=F32)
    return jnp.dot(a.astype(BF16), b.astype(BF16), preferred_element_type=F32)


def _mm_nt(a, b, precise=False):
    dn = (((1,), (1,)), ((), ()))
    if precise:
        return lax.dot_general(a.astype(F32), b.astype(F32), dn, precision=HIGHEST, preferred_element_type=F32)
    return lax.dot_general(a.astype(BF16), b.astype(BF16), dn, preferred_element_type=F32)


def _mm_tn(a, b, precise=False):
    dn = (((0,), (0,)), ((), ()))
    if precise:
        return lax.dot_general(a.astype(F32), b.astype(F32), dn, precision=HIGHEST, preferred_element_type=F32)
    return lax.dot_general(a.astype(BF16), b.astype(BF16), dn, preferred_element_type=F32)


def _mm_split(a, b, split_rhs=False):
    x = b if split_rhs else a
    hi = x.astype(BF16)
    lo = (x - hi.astype(F32)).astype(BF16)
    if split_rhs:
        a = a.astype(BF16)
        return jnp.dot(a, hi, preferred_element_type=F32) + jnp.dot(a, lo, preferred_element_type=F32)
    b = b.astype(BF16)
    return jnp.dot(hi, b, preferred_element_type=F32) + jnp.dot(lo, b, preferred_element_type=F32)


def _i32(v):
    return jnp.int32(v - (1 << 32) if v >= (1 << 31) else v)


def _bit_transpose(words):
    a = list(words)
    j, mask = 16, 0x0000FFFF
    while j:
        k = 0
        while k < 32:
            t = (a[k] ^ lax.shift_right_logical(a[k + j], jnp.int32(j))) & _i32(mask)
            a[k] = a[k] ^ t
            a[k + j] = a[k + j] ^ lax.shift_left(t, jnp.int32(j))
            k = (k + j + 1) & ~j
        j >>= 1
        mask = (mask ^ (mask << j)) & 0xFFFFFFFF
    return a


def _silu(x):
    return x * jax.nn.sigmoid(x)


def _softplus(x):
    return jnp.maximum(x, 0.0) + jnp.log(1.0 + jnp.exp(-jnp.abs(x)))


def _params(*sem):
    return pltpu.CompilerParams(dimension_semantics=sem, vmem_limit_bytes=VMEM_LIMIT)


def _full(shape):
    nd = len(shape)
    return pl.BlockSpec(shape, lambda *_: (0,) * nd)


def _in_proj_kernel(x_ref, nw_ref, w_ref, om_ref, or_ref, oc_ref, os_ref):
    x = x_ref[...]
    h = x * lax.rsqrt(jnp.mean(x * x, axis=-1, keepdims=True) + NORM_EPS) * nw_ref[...]
    h = h.astype(BF16)
    off = 0
    for o_ref, width in ((om_ref, M_COLS), (or_ref, R_COLS), (oc_ref, C_COLS), (os_ref, S_COLS)):
        o_ref[...] = jnp.dot(h, w_ref[:, off:off + width], preferred_element_type=F32)
        off += width


def _in_proj(x2, norm_w, w_pad, tm=512):
    t, d = x2.shape
    n = w_pad.shape[1]
    return pl.pallas_call(
        _in_proj_kernel,
        grid=(t // tm,),
        in_specs=[pl.BlockSpec((tm, d), lambda i: (i, 0)), _full((1, d)), _full((d, n))],
        out_specs=[pl.BlockSpec((tm, c), lambda i: (i, 0)) for c in (M_COLS, R_COLS, C_COLS, S_COLS)],
        out_shape=[jax.ShapeDtypeStruct((t, c), F32) for c in (M_COLS, R_COLS, C_COLS, S_COLS)],
        compiler_params=_params("parallel"),
        name="in_proj",
    )(x2, norm_w, w_pad)


def _mamba_kernel(u_ref, cw_ref, cb_ref, dtb_ref, alog_ref, dsk_ref, nw_ref, o_ref, xbuf, state):
    c = pl.program_id(1)
    n = SSD_CHUNK

    @pl.when(c == 0)
    def _():
        xbuf[0:8, :] = jnp.zeros((8, M_CONV_CH), F32)
        state[...] = jnp.zeros_like(state)

    u = u_ref[0]
    z = u[:, 0:W_GROUP]
    dtr = u[:, W_GROUP + M_CONV_CH:M_COLS]
    xbuf[8:8 + n, :] = u[:, W_GROUP:W_GROUP + M_CONV_CH]
    acc = jnp.broadcast_to(cb_ref[...], (n, M_CONV_CH))
    for j in range(M_CONV):
        acc = acc + cw_ref[j:j + 1, :] * xbuf[pl.ds(8 - (M_CONV - 1) + j, n), :]
    xbuf[0:8, :] = xbuf[n:n + 8, :]
    xc = _silu(acc)
    xs = xc[:, 0:W_GROUP]
    bm = xc[:, W_GROUP:W_GROUP + 2 * HEAD]
    cm = xc[:, W_GROUP + 2 * HEAD:]

    dt = _softplus(dtr + dtb_ref[...])
    a = dt * (-jnp.exp(alog_ref[...]))
    row = lax.broadcasted_iota(I32, (n, n), 0)
    col = lax.broadcasted_iota(I32, (n, n), 1)
    causal = row >= col
    a_cum = _mm(causal.astype(F32), a, precise=True)
    a_cum_t = a_cum.T
    a_last = a_cum[n - 1:n, :]

    ys = []
    for h in range(N_HEADS):
        g = h // 2
        b_h = bm[:, g * HEAD:(g + 1) * HEAD]
        c_h = cm[:, g * HEAD:(g + 1) * HEAD]
        x_h = xs[:, h * HEAD:(h + 1) * HEAD]
        xdt = x_h * dt[:, h:h + 1]
        ac = a_cum[:, h:h + 1]
        dec = jnp.exp(jnp.where(causal, ac - a_cum_t[h:h + 1, :], -jnp.inf))
        scores = _mm_nt(c_h, b_h) * dec
        y = _mm(scores, xdt)
        st = state[h]
        y = y + _mm_nt(c_h, st) * jnp.exp(ac)
        al = a_last[:, h:h + 1]
        state[h] = st * jnp.exp(al) + _mm_tn(xdt, b_h * jnp.exp(al - ac))
        ys.append(y + x_h * dsk_ref[:, h * HEAD:(h + 1) * HEAD])
    y = jnp.concatenate(ys, axis=1) * _silu(z)
    y = y * lax.rsqrt(jnp.mean(y * y, axis=-1, keepdims=True) + NORM_EPS) * nw_ref[...]
    o_ref[0] = y.astype(o_ref.dtype)


def _mamba(u_m, conv_w, conv_b, dt_bias, a_log, d_skip, norm_w):
    b, l, _ = u_m.shape
    pad = lambda v: jnp.pad(v.astype(F32), (0, LANES - v.shape[0]))[None, :]
    return pl.pallas_call(
        _mamba_kernel,
        grid=(b, l // SSD_CHUNK),
        in_specs=[pl.BlockSpec((1, SSD_CHUNK, M_COLS), lambda i, j: (i, j, 0)),
                  _full((M_CONV, M_CONV_CH)), _full((1, M_CONV_CH)), _full((1, LANES)), _full((1, LANES)),
                  _full((1, W_GROUP)), _full((1, W_GROUP))],
        out_specs=pl.BlockSpec((1, SSD_CHUNK, W_GROUP), lambda i, j: (i, j, 0)),
        out_shape=jax.ShapeDtypeStruct((b, l, W_GROUP), BF16),
        scratch_shapes=[pltpu.VMEM((SSD_CHUNK + 8, M_CONV_CH), F32), pltpu.VMEM((N_HEADS, HEAD, HEAD), F32)],
        compiler_params=_params("parallel", "arbitrary"),
        name="mamba_ssd",
    )(u_m, conv_w.astype(F32), conv_b.astype(F32)[None, :], pad(dt_bias), pad(a_log),
      jnp.repeat(d_skip.astype(F32), HEAD)[None, :], norm_w.astype(F32)[None, :])


def _rwkv_kernel(u_ref, mu_ref, wl_ref, w0_ref, a0_ref, kk_ref, ka_ref, rk_ref, lnw_ref, lnb_ref, seg_ref,
                 o_ref, ubuf, state):
    c = pl.program_id(1)
    n = R_BLOCK
    cs = R_CHUNK

    @pl.when(c == 0)
    def _():
        ubuf[0:8, :] = jnp.zeros((8, R_COLS), F32)
        state[...] = jnp.zeros_like(state)

    u = u_ref[0]
    ubuf[8:8 + n, :] = u
    u = u + mu_ref[...] * (ubuf[pl.ds(7, n), :] - u)
    ubuf[0:8, :] = ubuf[n:n + 8, :]
    r = u[:, 0:W_GROUP]
    k = u[:, W_GROUP:2 * W_GROUP]
    v = u[:, 2 * W_GROUP:3 * W_GROUP]
    lo = u[:, 3 * W_GROUP:]
    lane = lax.broadcasted_iota(I32, lo.shape, 1)
    lo = jnp.where(lane < 32, jnp.tanh(lo), jnp.where(lane < 64, lo, jax.nn.sigmoid(lo)))
    proj = _mm(lo, wl_ref[...])
    w_log = -_softplus(-(w0_ref[...] + proj[:, 0:W_GROUP])) - 0.5
    ld = -jnp.exp(w_log)
    a_lr = jax.nn.sigmoid(a0_ref[...] + proj[:, W_GROUP:2 * W_GROUP])
    g = proj[:, 2 * W_GROUP:]
    seg = seg_ref[...]
    kk = k * kk_ref[...]
    kk = kk * lax.rsqrt(jnp.maximum(_mm_split(kk * kk, seg), 1e-24))
    k = k * (1.0 + (a_lr - 1.0) * ka_ref[...])
    a_v = -kk
    b_v = kk * a_lr

    row = lax.broadcasted_iota(I32, (n, n), 0)
    col = lax.broadcasted_iota(I32, (n, n), 1)
    tri2 = ((row >= col) & ((row // cs) == (col // cs))).astype(F32)
    lw = _mm_split(tri2, ld, split_rhs=True)
    w_inc = jnp.exp(lw)
    w_inv = jnp.exp(-lw)
    rt = r * w_inc
    at = a_v * jnp.exp(lw - ld)
    bt = b_v * w_inv
    kt = k * w_inv

    wg = W_GROUP
    r2 = lax.broadcasted_iota(I32, (wg, wg), 0)
    c2 = lax.broadcasted_iota(I32, (wg, wg), 1)
    lower = (r2 % cs) >= (c2 % cs)
    strict = (r2 % cs) > (c2 % cs)
    eye = (r2 == c2).astype(F32)
    lane_head = lax.broadcasted_iota(I32, (cs, wg), 1) // HEAD

    def blocks(xc, dtype=BF16):
        return jnp.concatenate([jnp.where(lane_head == h, xc, 0.0) for h in range(N_HEADS)], axis=0).astype(dtype)

    s_cur = state[...]
    outs = []
    for ci in range(n // cs):
        rows = slice(ci * cs, (ci + 1) * cs)
        lw_end = lw[ci * cs + cs - 1:ci * cs + cs, :]
        ratio = jnp.exp(lw_end - lw[rows, :])
        at_b, rt_b, bt_b, kt_b = blocks(at[rows, :]), blocks(rt[rows, :], F32), blocks(bt[rows, :]), blocks(kt[rows, :])
        v_b = blocks(v[rows, :])
        bh = blocks(b_v[rows, :] * ratio)
        kh = blocks(k[rows, :] * ratio)
        a_ab = jnp.where(strict, _mm_nt(at_b, bt_b), 0.0)
        a_ak = jnp.where(strict, _mm_nt(at_b, kt_b), 0.0)
        a_rb = jnp.where(lower, _mm_nt(rt_b, bt_b), 0.0)
        a_rk = jnp.where(lower, _mm_nt(rt_b, kt_b), 0.0)
        tinv = eye + a_ab
        pw = a_ab
        for _ in range(int(math.log2(cs)) - 1):
            pw = _mm(pw, pw)
            tinv = tinv + _mm(pw, tinv)
        ta = _mm(tinv, at_b)
        pv = _mm(tinv, _mm(a_ak, v_b))
        q_eff = rt_b + _mm(a_rb, ta)
        o_v = _mm(a_rb, pv) + _mm(a_rk, v_b)
        s_v = _mm_tn(pv, bh) + _mm_tn(v_b, kh)
        o_b = _mm_nt(q_eff, s_cur) + o_v
        outs.append(o_b[0:cs] + o_b[cs:2 * cs] + o_b[2 * cs:3 * cs] + o_b[3 * cs:])
        s_cur = s_cur * jnp.exp(lw_end) + _mm_split(s_cur, _mm_tn(ta, bh)) + s_v
    state[...] = s_cur
    o = jnp.concatenate(outs, axis=0)

    inv = 1.0 / HEAD
    mean = _mm_split(o, seg) * inv
    d = o - mean
    var = _mm_split(d * d, seg) * inv
    o = d * lax.rsqrt(var + R_GN_EPS) * lnw_ref[...] + lnb_ref[...]
    o = o + _mm_split(r * k * rk_ref[...], seg) * v
    o_ref[0] = (o * g).astype(o_ref.dtype)


def _rwkv(u_r, mu, w0, w2, a0, a2, g2, k_k, k_a, r_k, ln_w, ln_b):
    b, l, _ = u_r.shape
    f = lambda t: t.astype(F32).reshape(1, -1)
    w_lora = jnp.zeros((LANES, 3 * W_GROUP), F32)
    w_lora = w_lora.at[0:32, 0:W_GROUP].set(w2.astype(F32))
    w_lora = w_lora.at[32:64, W_GROUP:2 * W_GROUP].set(a2.astype(F32))
    w_lora = w_lora.at[64:128, 2 * W_GROUP:].set(g2.astype(F32))
    hid = jnp.arange(W_GROUP) // HEAD
    seg = (hid[:, None] == hid[None, :]).astype(F32)
    vec = _full((1, W_GROUP))
    return pl.pallas_call(
        _rwkv_kernel,
        grid=(b, l // R_BLOCK),
        in_specs=[pl.BlockSpec((1, R_BLOCK, R_COLS), lambda i, j: (i, j, 0)),
                  _full((1, R_COLS)), _full((LANES, 3 * W_GROUP)),
                  vec, vec, vec, vec, vec, vec, vec, _full((W_GROUP, W_GROUP))],
        out_specs=pl.BlockSpec((1, R_BLOCK, W_GROUP), lambda i, j: (i, j, 0)),
        out_shape=jax.ShapeDtypeStruct((b, l, W_GROUP), BF16),
        scratch_shapes=[pltpu.VMEM((R_BLOCK + 8, R_COLS), F32), pltpu.VMEM((W_GROUP, W_GROUP), F32)],
        compiler_params=_params("parallel", "arbitrary"),
        name="rwkv7",
    )(u_r, f(mu), w_lora, f(w0), f(a0), f(k_k), f(k_a), f(r_k), f(ln_w), f(ln_b), seg)


def _dsa_kernel(u_ref, kvw_ref, wkv_ref, bias_ref, far_ref, o_ref,
                kt_scr, kv_scr, tail_scr, key_scr, plane_scr, mask_scr, acc_scr, mx_scr, l_scr, cut_scr, *, top_k):
    qb = pl.program_id(1)
    n = Q_BLOCK
    wide = DSA_WIDE
    seq = key_scr.shape[0] - wide

    @pl.when(qb == 0)
    def _():
        kt_scr[...] = jnp.zeros_like(kt_scr)
        kv_scr[...] = jnp.zeros_like(kv_scr)
        tail_scr[...] = jnp.zeros_like(tail_scr)
        key_scr[...] = jnp.full(key_scr.shape, INT_MIN, I32)
        plane_scr[...] = jnp.zeros_like(plane_scr)
        mask_scr[...] = jnp.full(mask_scr.shape, NEG, F32)

    u = u_ref[0]
    q = u[:, 0:W_GROUP]
    ckv = u[:, W_GROUP:W_GROUP + LANES]
    qi = u[:, W_GROUP + LANES:2 * W_GROUP + LANES]
    tail = u[:, 2 * W_GROUP + LANES:]
    ckv = ckv * lax.rsqrt(jnp.mean(ckv * ckv, axis=-1, keepdims=True) + NORM_EPS) * kvw_ref[...]
    kv = _mm(ckv, wkv_ref[...])
    start = pl.multiple_of(qb * n, n)
    end = pl.multiple_of(start + wide + n, n)
    sub = lax.broadcasted_iota(I32, (HEAD, n), 0)
    kt_scr[:, pl.ds(end - n, n)] = jnp.concatenate(
        [kv.T[0:HEAD, :], jnp.where(sub < DSA_AUX, 1.0, 0.0)], axis=0).astype(BF16)
    kv_scr[pl.ds(end - n, n), :] = kv.astype(BF16)
    tail_scr[pl.ds(end - n, n), :] = tail.astype(BF16)
    qi_t = qi.T
    w_qi = jnp.concatenate(
        [jnp.concatenate([qi_t[h * I_DIM:(h + 1) * I_DIM, :] for h in range(I_HEADS)], axis=1),
         jnp.zeros((LANES - I_DIM, I_HEADS * n), F32)], axis=0).astype(BF16)
    wi_t = tail.T[I_DIM:I_DIM + I_HEADS, :] * (I_HEADS ** -0.5 * I_DIM ** -0.5)

    q_pos = start + lax.broadcasted_iota(I32, (wide, n), 1)
    row_w = lax.broadcasted_iota(I32, (wide, n), 0)
    n_wide = (start + n + wide - 1) // wide

    def tile_start(i):
        return pl.multiple_of(end - (i + 1) * wide, n)

    def idx_tile(i, carry):
        ks = tile_start(i)
        s_all = jnp.dot(tail_scr[pl.ds(ks, wide), :], w_qi, preferred_element_type=F32)
        s = jnp.zeros((wide, n), F32)
        for h in range(I_HEADS):
            s = s + jnp.maximum(s_all[:, h * n:(h + 1) * n], 0.0) * wi_t[h:h + 1, :]
        s = s + 0.0
        bits = pltpu.bitcast(s, I32)
        key = jnp.where(bits < 0, bits ^ jnp.int32(0x7FFFFFFF), bits)
        idx = ks - wide + row_w
        key = jnp.where(jnp.logical_and(idx <= q_pos, idx >= 0), key, jnp.int32(INT_MIN))
        key_scr[pl.ds(ks, wide), :] = key
        ukey = key ^ jnp.int32(INT_MIN)
        for blk in range(wide // BIT_ROWS):
            words = _bit_transpose([ukey[blk * BIT_ROWS + 8 * k:blk * BIT_ROWS + 8 * k + 8, :] for k in range(32)])
            row0 = pl.multiple_of((i * (wide // BIT_ROWS) + blk) * 8, 8)
            for b in range(32):
                plane_scr[b, pl.ds(row0, 8), :] = words[31 - b]
        return carry

    lax.fori_loop(0, n_wide, idx_tile, 0)

    part = 64

    def count(pred):
        def body(i, acc):
            ks = tile_start(i)
            kk = key_scr[pl.ds(ks, wide), :]
            hit = pred(kk, ks - wide + row_w).astype(I32)
            for c in range(wide // part):
                acc = acc + hit[c * part:(c + 1) * part, :]
            return acc
        acc = lax.fori_loop(0, n_wide, body, jnp.zeros((part, n), I32))
        return jnp.sum(acc, axis=0, keepdims=True)

    blk_id = lax.broadcasted_iota(I32, (plane_scr.shape[1], n), 0) // 8
    live = jnp.where(blk_id < n_wide * (wide // BIT_ROWS), jnp.int32(-1), jnp.int32(0))
    above = jnp.zeros((1, n), I32)
    thr = jnp.zeros((1, n), I32)
    def n_set(words):
        return jnp.sum(lax.population_count(words), axis=0, keepdims=True)

    for b in range(31, 0, -2):
        hi = live & plane_scr[b]
        lo = live ^ hi
        t3 = hi & plane_scr[b - 1]
        t2 = hi ^ t3
        t1 = lo & plane_scr[b - 1]
        t0 = lo ^ t1
        g3 = above + n_set(t3)
        g2 = g3 + n_set(t2)
        g1 = g2 + n_set(t1)
        d3, d2, d1 = g3 >= top_k, g2 >= top_k, g1 >= top_k
        digit = jnp.where(d3, 3, jnp.where(d2, 2, jnp.where(d1, 1, 0)))
        thr = thr | lax.shift_left(digit, jnp.int32(b - 1))
        above = jnp.where(d3, above, jnp.where(d2, g3, jnp.where(d1, g2, g1)))
        live = jnp.where(d3, t3, jnp.where(d2, t2, jnp.where(d1, t1, t0)))
    thr = thr ^ jnp.int32(INT_MIN)
    n_ge = above + jnp.sum(lax.population_count(live), axis=0, keepdims=True)
    tied = jnp.logical_and(n_ge > top_k, thr != INT_MIN)
    cut_scr[0:1, :] = jnp.full((1, n), seq, I32)

    @pl.when(jnp.max(tied.astype(I32)) > 0)
    def _():
        need = top_k - above
        m = jnp.zeros((1, n), I32)
        for bit in range(int(math.log2(seq)) - 1, -1, -1):
            cand = m | jnp.int32(1 << bit)
            cnt = count(lambda kk, idx: jnp.logical_and(kk == thr, idx < cand))
            m = jnp.where(cnt < need, cand, m)
        cut_scr[0:1, :] = jnp.where(tied, m, seq)

    cut = cut_scr[0:1, :]

    def mask_tile(i, carry):
        ks = tile_start(i)
        kk = key_scr[pl.ds(ks, wide), :]
        sel = jnp.logical_or(kk > thr, jnp.logical_and(kk == thr, ks - wide + row_w <= cut))
        sel = jnp.logical_and(sel, kk != INT_MIN)
        m_t = jnp.where(sel, 0.0, NEG)
        for c in range(wide // n):
            mask_scr[:, pl.ds(ks + c * n, n)] = m_t[c * n:(c + 1) * n, :].T
        return carry

    lax.fori_loop(0, n_wide, mask_tile, 0)

    lane = lax.broadcasted_iota(I32, (n, LANES), 1)

    def split(x):
        hi = x.astype(BF16).astype(F32)
        return hi, x - hi

    def q_operand(shift, far):
        rows = []
        for h in range(N_HEADS):
            op = jnp.concatenate([q[:, h * HEAD:(h + 1) * HEAD] * (HEAD ** -0.5), jnp.zeros((n, HEAD), F32)], axis=1)
            if shift is not None:
                hi, lo = split(shift[h])
                op = op + jnp.where(lane == HEAD, hi, 0.0) + jnp.where(lane == HEAD + 1, lo, 0.0)
            if far:
                hi, lo = split(jnp.full((n, 1), far_ref[h], F32))
                op = op + jnp.where(lane == HEAD + 2, hi, 0.0) + jnp.where(lane == HEAD + 3, lo, 0.0)
            rows.append(op)
        return jnp.concatenate(rows, axis=0).astype(BF16)

    def logits(i, q_op, near):
        ks = tile_start(i)
        s_all = jnp.dot(q_op, kt_scr[:, pl.ds(ks, wide)], preferred_element_type=F32)
        msk = mask_scr[:, pl.ds(ks, wide)]
        out = []
        for h in range(N_HEADS):
            s = s_all[h * n:(h + 1) * n, :] + msk
            out.append(s + bias_ref[h] if near else s)
        return ks, out

    def max_tile(i, q_op, near):
        _, ss = logits(i, q_op, near)
        for h in range(N_HEADS):
            m = mx_scr[h * n:(h + 1) * n, :]
            for c in range(wide // n):
                m = jnp.maximum(m, ss[h][:, c * n:(c + 1) * n])
            mx_scr[h * n:(h + 1) * n, :] = m

    def sum_tile(i, q_op, near):
        ks, ss = logits(i, q_op, near)
        ps = []
        for h in range(N_HEADS):
            p = jnp.exp(ss[h])
            part_l = l_scr[h * n:(h + 1) * n, :]
            for c in range(wide // n):
                part_l = part_l + p[:, c * n:(c + 1) * n]
            l_scr[h * n:(h + 1) * n, :] = part_l
            ps.append(p.astype(BF16))
        acc_scr[...] += jnp.dot(jnp.concatenate(ps, axis=0), kv_scr[pl.ds(ks, wide), :], preferred_element_type=F32)

    def run(tile_fn, shift):
        tile_fn(0, q_operand(shift, False), True)
        q_far = q_operand(shift, True)

        def body(i, carry):
            tile_fn(i, q_far, False)
            return carry
        lax.fori_loop(1, n_wide, body, 0)

    mx_scr[...] = jnp.full(mx_scr.shape, NEG, F32)
    run(max_tile, None)
    neg_max = [-jnp.max(mx_scr[h * n:(h + 1) * n, :], axis=1, keepdims=True) for h in range(N_HEADS)]
    acc_scr[...] = jnp.zeros_like(acc_scr)
    l_scr[...] = jnp.zeros_like(l_scr)
    run(sum_tile, neg_max)
    o_ref[0] = jnp.concatenate(
        [acc_scr[h * n:(h + 1) * n, HEAD:] / jnp.sum(l_scr[h * n:(h + 1) * n, :], axis=1, keepdims=True)
         for h in range(N_HEADS)], axis=1).astype(o_ref.dtype)


def _t5_bucket(dist):
    n = jnp.maximum(dist, 0)
    max_exact = REL_BUCKETS // 2
    log_ratio = jnp.log(jnp.maximum(n, 1).astype(F32) / max_exact) / math.log(REL_MAX_DIST / max_exact)
    large = jnp.minimum(max_exact + (log_ratio * (REL_BUCKETS - max_exact)).astype(I32), REL_BUCKETS - 1)
    return jnp.where(n < max_exact, n, large)


def _dsa(u_c, kv_norm_w, w_uk, w_uv, rel_bias):
    b, l, _ = u_c.shape
    top_k = min(TOPK_MAX, l // 4)
    assert DSA_WIDE - Q_BLOCK + 1 >= REL_MAX_DIST and l % DSA_WIDE == 0
    dist = (jnp.arange(Q_BLOCK, dtype=I32)[:, None] + (DSA_WIDE - Q_BLOCK)) - jnp.arange(DSA_WIDE, dtype=I32)[None, :]
    onehot = _t5_bucket(dist)[None] == jnp.arange(REL_BUCKETS, dtype=I32)[:, None, None]
    table = rel_bias.astype(F32)
    bias = jnp.stack([jnp.sum(jnp.where(onehot, table[:, h][:, None, None], 0.0), axis=0) for h in range(N_HEADS)])
    w_kv = jnp.concatenate([w_uk, w_uv], axis=1).astype(F32)
    kern = functools.partial(_dsa_kernel, top_k=top_k)
    lp = l + DSA_WIDE
    return pl.pallas_call(
        kern,
        grid=(b, l // Q_BLOCK),
        in_specs=[pl.BlockSpec((1, Q_BLOCK, C_COLS), lambda i, j: (i, j, 0)),
                  _full((1, LANES)), _full((LANES, LANES)), _full((N_HEADS, Q_BLOCK, DSA_WIDE)),
                  pl.BlockSpec(memory_space=pltpu.SMEM)],
        out_specs=pl.BlockSpec((1, Q_BLOCK, W_GROUP), lambda i, j: (i, j, 0)),
        out_shape=jax.ShapeDtypeStruct((b, l, W_GROUP), BF16),
        scratch_shapes=[pltpu.VMEM((LANES, lp), BF16), pltpu.VMEM((lp, LANES), BF16), pltpu.VMEM((lp, LANES), BF16),
                        pltpu.VMEM((lp, Q_BLOCK), I32), pltpu.VMEM((32, lp // BIT_ROWS * 8, Q_BLOCK), I32),
                        pltpu.VMEM((Q_BLOCK, lp), F32),
                        pltpu.VMEM((N_HEADS * Q_BLOCK, LANES), F32), pltpu.VMEM((N_HEADS * Q_BLOCK, LANES), F32),
                        pltpu.VMEM((N_HEADS * Q_BLOCK, LANES), F32), pltpu.VMEM((8, Q_BLOCK), I32)],
        compiler_params=_params("parallel", "arbitrary"),
        name="dsa",
    )(u_c, kv_norm_w.astype(F32)[None, :], w_kv, bias, table[REL_BUCKETS - 1])


def _s5_kernel(u_ref, are_ref, aim_ref, ldt_ref, bre_ref, bim_ref, cre_ref, cim_ref, dsk_ref, gw_ref, gb_ref,
               o_ref, pre, pim, bdr, bdi, hre, him, cre_s, cim_s, tre_s, tim_s):
    first = jnp.logical_and(pl.program_id(0) == 0, pl.program_id(1) == 0)
    n = S_BLOCK
    pad = LANES

    @pl.when(first)
    def _():
        lam_re = jnp.minimum(are_ref[...], -1e-4)
        lam_im = aim_ref[...]
        dt = jnp.exp(ldt_ref[...])
        mag = jnp.exp(dt * lam_re)
        ab_re = mag * jnp.cos(dt * lam_im)
        ab_im = mag * jnp.sin(dt * lam_im)
        den = lam_re * lam_re + lam_im * lam_im
        f_re = ((ab_re - 1.0) * lam_re + ab_im * lam_im) / den
        f_im = (ab_im * lam_re - (ab_re - 1.0) * lam_im) / den
        b_re = bre_ref[...]
        b_im = bim_ref[...]
        bb_re = f_re * b_re - f_im * b_im
        bb_im = f_re * b_im + f_im * b_re
        rg = lax.broadcasted_iota(I32, (W_GROUP, S_LANES), 0) // S_GROUP_CH
        cg = lax.broadcasted_iota(I32, (W_GROUP, S_LANES), 1) // S_STATE
        same = rg == cg
        bdr[...] = jnp.where(same, jnp.concatenate([bb_re] * S_GROUPS, axis=0), 0.0)
        bdi[...] = jnp.where(same, jnp.concatenate([bb_im] * S_GROUPS, axis=0), 0.0)
        pre[0:1, :] = ab_re
        pim[0:1, :] = ab_im
        d = 1
        while d < S_WIN:
            sr = pre[d - 1:d, :]
            si = pim[d - 1:d, :]
            xr = pre[0:d, :]
            xi = pim[0:d, :]
            pre[d:2 * d, :] = xr * sr - xi * si
            pim[d:2 * d, :] = xr * si + xi * sr
            d *= 2
        hre[...] = jnp.zeros_like(hre)
        him[...] = jnp.zeros_like(him)

    @pl.when(pl.program_id(1) == 0)
    def _():
        cre_s[...] = jnp.zeros_like(cre_s)
        cim_s[...] = jnp.zeros_like(cim_s)
        tre_s[...] = jnp.zeros_like(tre_s)
        tim_s[...] = jnp.zeros_like(tim_s)

    u = u_ref[0]
    bu_re = _mm(u, bdr[...])
    bu_im = _mm(u, bdi[...])
    lo = pad - S_WIN
    hre[0, lo:pad, :] = tre_s[...]
    him[0, lo:pad, :] = tim_s[...]
    hre[0, pad:pad + n, :] = bu_re
    him[0, pad:pad + n, :] = bu_im
    tre_s[...] = bu_re[n - S_WIN:, :]
    tim_s[...] = bu_im[n - S_WIN:, :]

    def col_block(cb, c_):
        cs = pl.ds(pl.multiple_of(cb * LANES, LANES), LANES)
        src = 0
        d = 1
        while d < S_WIN:
            ar = pre[d - 1:d, cs]
            ai = pim[d - 1:d, cs]
            xr = hre[src, pl.ds(lo - d, n + S_WIN), cs]
            xi = him[src, pl.ds(lo - d, n + S_WIN), cs]
            hre[1 - src, lo:pad + n, cs] = hre[src, lo:pad + n, cs] + ar * xr - ai * xi
            him[1 - src, lo:pad + n, cs] = him[src, lo:pad + n, cs] + ar * xi + ai * xr
            src = 1 - src
            d *= 2
        ar = jnp.broadcast_to(pre[S_WIN - 1:S_WIN, cs], (S_WIN, LANES))
        ai = jnp.broadcast_to(pim[S_WIN - 1:S_WIN, cs], (S_WIN, LANES))
        pr = cre_s[:, cs]
        pi = cim_s[:, cs]
        for v in range(n // S_WIN):
            rows = slice(pad + v * S_WIN, pad + (v + 1) * S_WIN)
            nr = hre[src, rows, cs] + ar * pr - ai * pi
            ni = him[src, rows, cs] + ar * pi + ai * pr
            hre[1 - src, rows, cs] = nr
            him[1 - src, rows, cs] = ni
            pr, pi = nr, ni
        cre_s[:, cs] = pr
        cim_s[:, cs] = pi
        assert src == 1
        return c_

    lax.fori_loop(0, S_LANES // LANES, col_block, 0)

    h_re = hre[0, pad:pad + n, :]
    h_im = him[0, pad:pad + n, :]
    y = _mm(h_re, cre_ref[...]) - _mm(h_im, cim_ref[...]) + dsk_ref[...] * u
    y = jax.nn.gelu(y)
    gl = _mm(y, gw_ref[...]) + gb_ref[...]
    o_ref[0] = (gl[:, 0:W_GROUP] * jax.nn.sigmoid(gl[:, W_GROUP:])).astype(o_ref.dtype)


def _s5(u_s, a_re, a_im, b_re, b_im, c_re, c_im, d_skip, log_dt, glu_w, glu_b):
    b, l, _ = u_s.shape
    f = lambda t: t.astype(F32).reshape(1, -1)
    eye = jnp.eye(S_GROUPS, dtype=F32)
    bt = lambda t: jnp.transpose(t.astype(F32), (2, 0, 1)).reshape(S_GROUP_CH, S_LANES)
    cbd = lambda t: jnp.einsum('gpn,gh->gnhp', t.astype(F32), eye).reshape(S_LANES, W_GROUP)
    ldt = jnp.repeat(log_dt.astype(F32), S_STATE)[None, :]
    return pl.pallas_call(
        _s5_kernel,
        grid=(b, l // S_BLOCK),
        in_specs=[pl.BlockSpec((1, S_BLOCK, S_COLS), lambda i, j: (i, j, 0)),
                  _full((1, S_LANES)), _full((1, S_LANES)), _full((1, S_LANES)),
                  _full((S_GROUP_CH, S_LANES)), _full((S_GROUP_CH, S_LANES)),
                  _full((S_LANES, W_GROUP)), _full((S_LANES, W_GROUP)),
                  _full((1, W_GROUP)), _full((W_GROUP, 2 * W_GROUP)), _full((1, 2 * W_GROUP))],
        out_specs=pl.BlockSpec((1, S_BLOCK, W_GROUP), lambda i, j: (i, j, 0)),
        out_shape=jax.ShapeDtypeStruct((b, l, W_GROUP), BF16),
        scratch_shapes=[pltpu.VMEM((S_WIN, S_LANES), F32), pltpu.VMEM((S_WIN, S_LANES), F32),
                        pltpu.VMEM((W_GROUP, S_LANES), F32), pltpu.VMEM((W_GROUP, S_LANES), F32),
                        pltpu.VMEM((2, S_BLOCK + LANES, S_LANES), F32), pltpu.VMEM((2, S_BLOCK + LANES, S_LANES), F32),
                        pltpu.VMEM((S_WIN, S_LANES), F32), pltpu.VMEM((S_WIN, S_LANES), F32),
                        pltpu.VMEM((S_WIN, S_LANES), F32), pltpu.VMEM((S_WIN, S_LANES), F32)],
        compiler_params=_params("arbitrary", "arbitrary"),
        name="s5",
    )(u_s, f(a_re), f(a_im), ldt, bt(b_re), bt(b_im), cbd(c_re), cbd(c_im), f(d_skip),
      glu_w.astype(F32), f(glu_b))


def _out_proj_kernel(x_ref, ym_ref, yr_ref, yc_ref, ys_ref, w_ref, o_ref):
    acc = x_ref[...]
    for i, y_ref in enumerate((ym_ref, yr_ref, yc_ref, ys_ref)):
        acc = acc + jnp.dot(y_ref[...], w_ref[i * W_GROUP:(i + 1) * W_GROUP, :], preferred_element_type=F32)
    o_ref[...] = acc


def _out_proj(x2, ys, w_out, tm=512):
    t, d = x2.shape
    yspec = pl.BlockSpec((tm, W_GROUP), lambda i: (i, 0))
    return pl.pallas_call(
        _out_proj_kernel,
        grid=(t // tm,),
        in_specs=[pl.BlockSpec((tm, d), lambda i: (i, 0)), yspec, yspec, yspec, yspec, _full(w_out.shape)],
        out_specs=pl.BlockSpec((tm, d), lambda i: (i, 0)),
        out_shape=jax.ShapeDtypeStruct((t, d), F32),
        compiler_params=_params("parallel"),
        name="out_proj",
    )(x2, *ys, w_out)


def _moe_kernel(x_ref, nw_ref, wr_ref, br_ref, wg_ref, wu_ref, wd_ref, fw_ref, o_ref,
                t_scr, cw_scr, rk_scr, rkt_scr, cnt_scr, acc_scr, *, final_norm):
    g = pl.program_id(1)
    tm = x_ref.shape[0]
    cap = MOE_CAP
    lane = lax.broadcasted_iota(I32, (tm, LANES), 1)

    @pl.when(g == 0)
    def _():
        x = x_ref[...]
        t = x * lax.rsqrt(jnp.mean(x * x, axis=-1, keepdims=True) + NORM_EPS) * nw_ref[...]
        t_hi = t.astype(BF16)
        t_scr[...] = t_hi
        t_lo = (t - t_hi.astype(F32)).astype(BF16)
        w_r = wr_ref[...]
        w_hi = w_r.astype(BF16)
        w_lo = (w_r - w_hi.astype(F32)).astype(BF16)
        logits = (jnp.dot(t_hi, w_hi, preferred_element_type=F32) + jnp.dot(t_lo, w_hi, preferred_element_type=F32)
                  + jnp.dot(t_hi, w_lo, preferred_element_type=F32)) + br_ref[...]
        big = jnp.int32(LANES)
        is_g = lane < E_GROUPS
        gl = jnp.where(is_g, logits, -jnp.inf)
        gmax = jnp.max(gl, axis=1, keepdims=True)
        g_sel = jnp.min(jnp.where(jnp.logical_and(is_g, gl == gmax), lane, big), axis=1, keepdims=True)
        g_gate = 1.0 / jnp.sum(jnp.exp(gl - gmax), axis=1, keepdims=True)
        lo = E_GROUPS + g_sel * E_PER_GROUP
        in_g = jnp.logical_and(lane >= lo, lane < lo + E_PER_GROUP)
        el = jnp.where(in_g, logits, -jnp.inf)
        m1 = jnp.max(el, axis=1, keepdims=True)
        i1 = jnp.min(jnp.where(el == m1, lane, big), axis=1, keepdims=True)
        el2 = jnp.where(lane == i1, -jnp.inf, el)
        m2 = jnp.max(el2, axis=1, keepdims=True)
        i2 = jnp.min(jnp.where(el2 == m2, lane, big), axis=1, keepdims=True)
        e2 = jnp.exp(m2 - m1)
        w1 = 1.0 / (1.0 + e2)
        w2 = e2 / (1.0 + e2)
        cw = jnp.where(lane == i1, w1, jnp.where(lane == i2, w2, 0.0)) * g_gate
        cw_hi = cw.astype(BF16)
        cw_scr[...] = jnp.concatenate([cw_hi, (cw - cw_hi.astype(F32)).astype(BF16)], axis=1)
        member = lane == g_sel
        ones = jnp.where(member, 1.0, 0.0)
        row = lax.broadcasted_iota(I32, (LANES, LANES), 0)
        col = lax.broadcasted_iota(I32, (LANES, LANES), 1)
        before = jnp.where(row > col, 1.0, 0.0).astype(BF16)
        running = jnp.zeros((1, LANES), F32)
        ranks = []
        for j in range(tm // LANES):
            blk = ones[j * LANES:(j + 1) * LANES, :]
            ranks.append(jnp.dot(before, blk.astype(BF16), preferred_element_type=F32) + running)
            running = running + jnp.sum(blk, axis=0, keepdims=True)
        rk = jnp.where(member, jnp.concatenate(ranks, axis=0), -1.0)
        rk_scr[...] = rk
        rkt_scr[...] = rk.T[0:8, :]
        cnt_scr[0:1, :] = running
        acc_scr[...] = jnp.zeros_like(acc_scr)

    lane1 = lax.broadcasted_iota(I32, (1, LANES), 1)
    n_tok = jnp.sum(jnp.where(lane1 == g, cnt_scr[0:1, :], 0.0)).astype(I32)
    rk_row = rkt_scr[pl.ds(g, 1), :]
    rk_col = jnp.sum(jnp.where(lane == g, rk_scr[...], 0.0), axis=1, keepdims=True)
    slot_r = lax.broadcasted_iota(I32, (cap, tm), 0).astype(F32)
    slot_c = lax.broadcasted_iota(I32, (tm, cap), 1).astype(F32)
    lane_c = lax.broadcasted_iota(I32, (cap, LANES), 1)

    def one_pass(blk, carry):
        base = (blk * cap).astype(F32)
        gather = jnp.where(rk_row - base == slot_r, 1.0, 0.0).astype(BF16)
        scatter = jnp.where(rk_col - base == slot_c, 1.0, 0.0).astype(BF16)
        xg = jnp.dot(gather, t_scr[...], preferred_element_type=F32).astype(BF16)
        cwg = jnp.dot(gather, cw_scr[...], preferred_element_type=F32)
        cwg = cwg[:, 0:LANES] + cwg[:, LANES:]
        parts = []
        for e in range(E_PER_GROUP):
            hid = _silu(jnp.dot(xg, wg_ref[0, e], preferred_element_type=F32)) * jnp.dot(xg, wu_ref[0, e], preferred_element_type=F32)
            w_e = jnp.sum(jnp.where(lane_c == E_GROUPS + g * E_PER_GROUP + e, cwg, 0.0), axis=1, keepdims=True)
            parts.append((hid * w_e).astype(BF16))
        y = jnp.dot(jnp.concatenate(parts, axis=1), wd_ref[0], preferred_element_type=F32)
        acc_scr[...] += jnp.dot(scatter, y.astype(BF16), preferred_element_type=F32)
        return carry

    lax.fori_loop(0, (n_tok + cap - 1) // cap, one_pass, 0)

    @pl.when(g == pl.num_programs(1) - 1)
    def _():
        y = x_ref[...] + acc_scr[...]
        if final_norm:
            y = y * lax.rsqrt(jnp.mean(y * y, axis=-1, keepdims=True) + NORM_EPS) * fw_ref[...]
        o_ref[...] = y


def _moe(x2, norm_w, wr_g, br_g, wr_e, br_e, w_gate, w_up, w_down, final_w, final_norm, tm=1024):
    t, d = x2.shape
    wr = jnp.zeros((d, LANES), F32).at[:, 0:E_GROUPS].set(wr_g.astype(F32))
    wr = wr.at[:, E_GROUPS:E_GROUPS + N_EXPERTS].set(wr_e.astype(F32))
    br = jnp.zeros((1, LANES), F32).at[0, 0:E_GROUPS].set(br_g.astype(F32))
    br = br.at[0, E_GROUPS:E_GROUPS + N_EXPERTS].set(br_e.astype(F32))
    wide = E_PER_GROUP * D_EXPERT
    wg = w_gate.astype(BF16)
    wu = w_up.astype(BF16)
    wd = w_down.reshape(E_GROUPS, wide, d).astype(BF16)
    kern = functools.partial(_moe_kernel, final_norm=final_norm)
    return pl.pallas_call(
        kern,
        grid=(t // tm, E_GROUPS),
        in_specs=[pl.BlockSpec((tm, d), lambda i, g: (i, 0)), _full((1, d)), _full((d, LANES)), _full((1, LANES)),
                  pl.BlockSpec((1, E_PER_GROUP, d, D_EXPERT), lambda i, g: (g, 0, 0, 0)),
                  pl.BlockSpec((1, E_PER_GROUP, d, D_EXPERT), lambda i, g: (g, 0, 0, 0)),
                  pl.BlockSpec((1, wide, d), lambda i, g: (g, 0, 0)), _full((1, d))],
        out_specs=pl.BlockSpec((tm, d), lambda i, g: (i, 0)),
        out_shape=jax.ShapeDtypeStruct((t, d), F32),
        scratch_shapes=[pltpu.VMEM((tm, d), BF16), pltpu.VMEM((tm, 2 * LANES), BF16), pltpu.VMEM((tm, LANES), F32),
                        pltpu.VMEM((8, tm), F32), pltpu.VMEM((8, LANES), F32), pltpu.VMEM((tm, d), F32)],
        compiler_params=_params("parallel", "arbitrary"),
        name="moe",
    )(x2, norm_w.astype(F32)[None, :], wr, br, wg, wu, wd, final_w.astype(F32)[None, :])


def _pad_in_proj(w):
    d = w.shape[0]
    m_in, r_in = 772, 896
    c_used = 680
    z = lambda k: jnp.zeros((d, k), w.dtype)
    wm = w[:, 0:m_in]
    wr = w[:, m_in:m_in + r_in]
    wc = w[:, m_in + r_in:m_in + r_in + c_used]
    ws = w[:, m_in + r_in + c_used:]
    return jnp.concatenate([wm, z(M_COLS - m_in), wr, wc, z(C_COLS - c_used), ws], axis=1).astype(BF16)


def kernel(x, norm1_w, w_in, m_conv_w, m_conv_b, m_dt_bias, m_a_log, m_d, m_norm_w, r_mu, r_w0, r_w2, r_a0, r_a2, r_g2, r_k_k, r_k_a, r_r_k, r_ln_w, r_ln_b, c_kv_norm_w, c_w_uk, c_w_uv, rel_bias, s_a_re, s_a_im, s_b_re, s_b_im, s_c_re, s_c_im, s_d, s_log_dt, s_glu_w, s_glu_b, w_out, norm2_w, moe_wr_group, moe_br_group, moe_wr_exp, moe_br_exp, moe_w_gate, moe_w_up, moe_w_down, final_norm_w):
    bsz, seq, d = x.shape
    depth = w_in.shape[0]
    x2 = x.astype(F32).reshape(bsz * seq, d)
    for i in range(depth):
        u_m, u_r, u_c, u_s = _in_proj(x2, norm1_w[i].astype(F32)[None, :], _pad_in_proj(w_in[i]))
        sh = lambda t: t.reshape(bsz, seq, t.shape[-1])
        y_m = _mamba(sh(u_m), m_conv_w[i], m_conv_b[i], m_dt_bias[i], m_a_log[i], m_d[i], m_norm_w[i])
        y_r = _rwkv(sh(u_r), r_mu[i], r_w0[i], r_w2[i], r_a0[i], r_a2[i], r_g2[i],
                    r_k_k[i], r_k_a[i], r_r_k[i], r_ln_w[i], r_ln_b[i])
        y_c = _dsa(sh(u_c), c_kv_norm_w[i], c_w_uk[i], c_w_uv[i], rel_bias)
        y_s = _s5(sh(u_s), s_a_re[i], s_a_im[i], s_b_re[i], s_b_im[i], s_c_re[i], s_c_im[i],
                  s_d[i], s_log_dt[i], s_glu_w[i], s_glu_b[i])
        ys = [t.reshape(bsz * seq, W_GROUP) for t in (y_m, y_r, y_c, y_s)]
        x2 = _out_proj(x2, ys, w_out[i].astype(BF16))
        x2 = _moe(x2, norm2_w[i], moe_wr_group[i], moe_br_group[i], moe_wr_exp[i], moe_br_exp[i],
                  moe_w_gate[i], moe_w_up[i], moe_w_down[i], final_norm_w, final_norm=(i == depth - 1))
    return x2.reshape(bsz, seq, d).astype(x.dtype)
```

```python
import functools
import math

import jax
import jax.numpy as jnp
from jax import lax
from jax.experimental import pallas as pl
from jax.experimental.pallas import tpu as pltpu

F32 = jnp.float32
BF16 = jnp.bfloat16
I32 = jnp.int32
HIGHEST = lax.Precision.HIGHEST

LANES = 128
VMEM_LIMIT = 56 * 1024 * 1024

NORM_EPS = 1e-6
W_GROUP = 256
HEAD = 64
N_HEADS = W_GROUP // HEAD
M_CONV_CH = 512
M_CONV = 4
M_COLS = 896
R_COLS = 896
C_COLS = 768
S_COLS = 256
SSD_CHUNK = 128
SSD_BLOCK = 256
R_CHUNK = 64
R_BLOCK = 256
R_GN_EPS = 64e-5
I_HEADS = 8
I_DIM = 32
Q_BLOCK = 128
DSA_WIDE = 512
BIT_ROWS = 256
DSA_AUX = 4
TOPK_MAX = 256
REL_BUCKETS = 32
REL_MAX_DIST = 128
S_GROUPS = 16
S_GROUP_CH = 16
S_STATE = 64
S_LANES = S_GROUPS * S_STATE
S_BLOCK = 256
S_WIN = 8
E_GROUPS = 4
E_PER_GROUP = 8
N_EXPERTS = E_GROUPS * E_PER_GROUP
D_EXPERT = 256
MOE_CAP = 288
INT_MIN = -2 ** 31
NEG = -1e30


def _mm(a, b, precise=False):
    if precise:
        return jnp.dot(a.astype(F32), b.astype(F32), precision=HIGHEST, preferred_element_type=F32)
    return jnp.dot(a.astype(BF16), b.astype(BF16), preferred_element_type=F32)


def _mm_nt(a, b, precise=False):
    dn = (((1,), (1,)), ((), ()))
    if precise:
        return lax.dot_general(a.astype(F32), b.astype(F32), dn, precision=HIGHEST, preferred_element_type=F32)
    return lax.dot_general(a.astype(BF16), b.astype(BF16), dn, preferred_element_type=F32)


def _mm_tn(a, b, precise=False):
    dn = (((0,), (0,)), ((), ()))
    if precise:
        return lax.dot_general(a.astype(F32), b.astype(F32), dn, precision=HIGHEST, preferred_element_type=F32)
    return lax.dot_general(a.astype(BF16), b.astype(BF16), dn, preferred_element_type=F32)


def _mm_split(a, b, split_rhs=False):
    x = b if split_rhs else a
    hi = x.astype(BF16)
    lo = (x - hi.astype(F32)).astype(BF16)
    if split_rhs:
        a = a.astype(BF16)
        return jnp.dot(a, hi, preferred_element_type=F32) + jnp.dot(a, lo, preferred_element_type=F32)
    b = b.astype(BF16)
    return jnp.dot(hi, b, preferred_element_type=F32) + jnp.dot(lo, b, preferred_element_type=F32)


def _i32(v):
    return jnp.int32(v - (1 << 32) if v >= (1 << 31) else v)


def _bit_transpose(words):
    a = list(words)
    j, mask = 16, 0x0000FFFF
    while j:
        k = 0
        while k < 32:
            t = (a[k] ^ lax.shift_right_logical(a[k + j], jnp.int32(j))) & _i32(mask)
            a[k] = a[k] ^ t
            a[k + j] = a[k + j] ^ lax.shift_left(t, jnp.int32(j))
            k = (k + j + 1) & ~j
        j >>= 1
        mask = (mask ^ (mask << j)) & 0xFFFFFFFF
    return a


def _silu(x):
    return x * jax.nn.sigmoid(x)


def _softplus(x):
    return jnp.maximum(x, 0.0) + jnp.log(1.0 + jnp.exp(-jnp.abs(x)))


def _params(*sem):
    return pltpu.CompilerParams(dimension_semantics=sem, vmem_limit_bytes=VMEM_LIMIT)


def _full(shape):
    nd = len(shape)
    return pl.BlockSpec(shape, lambda *_: (0,) * nd)


def _in_proj_kernel(x_ref, nw_ref, w_ref, om_ref, or_ref, oc_ref, os_ref):
    x = x_ref[...]
    h = x * lax.rsqrt(jnp.mean(x * x, axis=-1, keepdims=True) + NORM_EPS) * nw_ref[...]
    h = h.astype(BF16)
    off = 0
    for o_ref, width in ((om_ref, M_COLS), (or_ref, R_COLS), (oc_ref, C_COLS), (os_ref, S_COLS)):
        o_ref[...] = jnp.dot(h, w_ref[:, off:off + width], preferred_element_type=F32)
        off += width


def _in_proj(x2, norm_w, w_pad, tm=512):
    t, d = x2.shape
    n = w_pad.shape[1]
    return pl.pallas_call(
        _in_proj_kernel,
        grid=(t // tm,),
        in_specs=[pl.BlockSpec((tm, d), lambda i: (i, 0)), _full((1, d)), _full((d, n))],
        out_specs=[pl.BlockSpec((tm, c), lambda i: (i, 0)) for c in (M_COLS, R_COLS, C_COLS, S_COLS)],
        out_shape=[jax.ShapeDtypeStruct((t, c), F32) for c in (M_COLS, R_COLS, C_COLS, S_COLS)],
        compiler_params=_params("parallel"),
        name="in_proj",
    )(x2, norm_w, w_pad)


def _mamba_kernel(u_ref, cw_ref, cb_ref, dtb_ref, alog_ref, dsk_ref, nw_ref, o_ref, xbuf, state):
    c = pl.program_id(1)
    n = SSD_BLOCK

    @pl.when(c == 0)
    def _():
        xbuf[0:8, :] = jnp.zeros((8, M_CONV_CH), F32)
        state[...] = jnp.zeros_like(state)

    u = u_ref[0]
    z = u[:, 0:W_GROUP]
    dtr = u[:, W_GROUP + M_CONV_CH:M_COLS]
    xbuf[8:8 + n, :] = u[:, W_GROUP:W_GROUP + M_CONV_CH]
    acc = jnp.broadcast_to(cb_ref[...], (n, M_CONV_CH))
    for j in range(M_CONV):
        acc = acc + cw_ref[j:j + 1, :] * xbuf[pl.ds(8 - (M_CONV - 1) + j, n), :]
    xbuf[0:8, :] = xbuf[n:n + 8, :]
    xc = _silu(acc)
    xs = xc[:, 0:W_GROUP]
    bm = xc[:, W_GROUP:W_GROUP + 2 * HEAD]
    cm = xc[:, W_GROUP + 2 * HEAD:]

    dt = _softplus(dtr + dtb_ref[...])
    a = dt * (-jnp.exp(alog_ref[...]))
    cs = SSD_CHUNK
    row = lax.broadcasted_iota(I32, (n, n), 0)
    col = lax.broadcasted_iota(I32, (n, n), 1)
    same_chunk = jnp.logical_and(row >= col, (row // cs) == (col // cs))
    a_cum = _mm(same_chunk.astype(F32), a, precise=True)
    a_cum_t = a_cum.T
    causal = same_chunk[0:cs, 0:cs]

    ys = []
    for h in range(N_HEADS):
        g = h // 2
        st = state[h]
        ys_h = []
        for ci in range(n // cs):
            rows = slice(ci * cs, (ci + 1) * cs)
            b_h = bm[rows, g * HEAD:(g + 1) * HEAD]
            c_h = cm[rows, g * HEAD:(g + 1) * HEAD]
            x_h = xs[rows, h * HEAD:(h + 1) * HEAD]
            xdt = x_h * dt[rows, h:h + 1]
            ac = a_cum[rows, h:h + 1]
            dec = jnp.exp(jnp.where(causal, ac - a_cum_t[h:h + 1, rows], -jnp.inf))
            y = _mm(_mm_nt(c_h, b_h) * dec, xdt)
            y = y + _mm_nt(c_h, st) * jnp.exp(ac)
            al = a_cum[ci * cs + cs - 1:ci * cs + cs, h:h + 1]
            st = st * jnp.exp(al) + _mm_tn(xdt, b_h * jnp.exp(al - ac))
            ys_h.append(y + x_h * dsk_ref[:, h * HEAD:(h + 1) * HEAD])
        state[h] = st
        ys.append(jnp.concatenate(ys_h, axis=0))
    y = jnp.concatenate(ys, axis=1) * _silu(z)
    y = y * lax.rsqrt(jnp.mean(y * y, axis=-1, keepdims=True) + NORM_EPS) * nw_ref[...]
    o_ref[0] = y.astype(o_ref.dtype)


def _mamba(u_m, conv_w, conv_b, dt_bias, a_log, d_skip, norm_w):
    b, l, _ = u_m.shape
    pad = lambda v: jnp.pad(v.astype(F32), (0, LANES - v.shape[0]))[None, :]
    return pl.pallas_call(
        _mamba_kernel,
        grid=(b, l // SSD_BLOCK),
        in_specs=[pl.BlockSpec((1, SSD_BLOCK, M_COLS), lambda i, j: (i, j, 0)),
                  _full((M_CONV, M_CONV_CH)), _full((1, M_CONV_CH)), _full((1, LANES)), _full((1, LANES)),
                  _full((1, W_GROUP)), _full((1, W_GROUP))],
        out_specs=pl.BlockSpec((1, SSD_BLOCK, W_GROUP), lambda i, j: (i, j, 0)),
        out_shape=jax.ShapeDtypeStruct((b, l, W_GROUP), BF16),
        scratch_shapes=[pltpu.VMEM((SSD_BLOCK + 8, M_CONV_CH), F32), pltpu.VMEM((N_HEADS, HEAD, HEAD), F32)],
        compiler_params=_params("parallel", "arbitrary"),
        name="mamba_ssd",
    )(u_m, conv_w.astype(F32), conv_b.astype(F32)[None, :], pad(dt_bias), pad(a_log),
      jnp.repeat(d_skip.astype(F32), HEAD)[None, :], norm_w.astype(F32)[None, :])


def _rwkv_kernel(u_ref, mu_ref, wl_ref, w0_ref, a0_ref, kk_ref, ka_ref, rk_ref, lnw_ref, lnb_ref, seg_ref,
                 o_ref, ubuf, state):
    c = pl.program_id(1)
    n = R_BLOCK
    cs = R_CHUNK

    @pl.when(c == 0)
    def _():
        ubuf[0:8, :] = jnp.zeros((8, R_COLS), F32)
        state[...] = jnp.zeros_like(state)

    u = u_ref[0]
    ubuf[8:8 + n, :] = u
    u = u + mu_ref[...] * (ubuf[pl.ds(7, n), :] - u)
    ubuf[0:8, :] = ubuf[n:n + 8, :]
    r = u[:, 0:W_GROUP]
    k = u[:, W_GROUP:2 * W_GROUP]
    v = u[:, 2 * W_GROUP:3 * W_GROUP]
    lo = u[:, 3 * W_GROUP:]
    lane = lax.broadcasted_iota(I32, lo.shape, 1)
    lo = jnp.where(lane < 32, jnp.tanh(lo), jnp.where(lane < 64, lo, jax.nn.sigmoid(lo)))
    proj = _mm(lo, wl_ref[...])
    w_log = -_softplus(-(w0_ref[...] + proj[:, 0:W_GROUP])) - 0.5
    ld = -jnp.exp(w_log)
    a_lr = jax.nn.sigmoid(a0_ref[...] + proj[:, W_GROUP:2 * W_GROUP])
    g = proj[:, 2 * W_GROUP:]
    seg = seg_ref[...]
    kk = k * kk_ref[...]
    kk = kk * lax.rsqrt(jnp.maximum(_mm_split(kk * kk, seg), 1e-24))
    k = k * (1.0 + (a_lr - 1.0) * ka_ref[...])
    a_v = -kk
    b_v = kk * a_lr

    row = lax.broadcasted_iota(I32, (n, n), 0)
    col = lax.broadcasted_iota(I32, (n, n), 1)
    tri2 = ((row >= col) & ((row // cs) == (col // cs))).astype(F32)
    lw = _mm_split(tri2, ld, split_rhs=True)
    w_inc = jnp.exp(lw)
    w_inv = jnp.exp(-lw)
    rt = r * w_inc
    at = a_v * jnp.exp(lw - ld)
    bt = b_v * w_inv
    kt = k * w_inv

    wg = W_GROUP
    r2 = lax.broadcasted_iota(I32, (wg, wg), 0)
    c2 = lax.broadcasted_iota(I32, (wg, wg), 1)
    lower = (r2 % cs) >= (c2 % cs)
    strict = (r2 % cs) > (c2 % cs)
    eye = (r2 == c2).astype(F32)
    lane_head = lax.broadcasted_iota(I32, (cs, wg), 1) // HEAD

    def blocks(xc, dtype=BF16):
        return jnp.concatenate([jnp.where(lane_head == h, xc, 0.0) for h in range(N_HEADS)], axis=0).astype(dtype)

    s_cur = state[...]
    outs = []
    for ci in range(n // cs):
        rows = slice(ci * cs, (ci + 1) * cs)
        lw_end = lw[ci * cs + cs - 1:ci * cs + cs, :]
        ratio = jnp.exp(lw_end - lw[rows, :])
        at_b, rt_b, bt_b, kt_b = blocks(at[rows, :]), blocks(rt[rows, :], F32), blocks(bt[rows, :]), blocks(kt[rows, :])
        v_b = blocks(v[rows, :])
        bh = blocks(b_v[rows, :] * ratio)
        kh = blocks(k[rows, :] * ratio)
        ar_b = jnp.concatenate([at_b, rt_b.astype(BF16)], axis=0)
        g_b = _mm_nt(ar_b, bt_b)
        g_k = _mm_nt(ar_b, kt_b)
        a_ab = jnp.where(strict, g_b[0:wg], 0.0)
        a_ak = jnp.where(strict, g_k[0:wg], 0.0)
        a_rb = jnp.where(lower, g_b[wg:], 0.0)
        a_rk = jnp.where(lower, g_k[wg:], 0.0)
        tinv = eye + a_ab
        pw = a_ab
        for _ in range(int(math.log2(cs)) - 1):
            pw = _mm(pw, pw)
            tinv = tinv + _mm(pw, tinv)
        ta = _mm(tinv, at_b)
        akv = _mm(jnp.concatenate([a_ak, a_rk], axis=0), v_b)
        pv = _mm(tinv, akv[0:wg])
        q_eff = rt_b + _mm(a_rb, ta)
        o_v = _mm(a_rb, pv) + akv[wg:]
        tb = _mm_tn(jnp.concatenate([ta, pv], axis=1), bh)
        s_v = tb[wg:] + _mm_tn(v_b, kh)
        o_b = _mm_nt(q_eff, s_cur) + o_v
        outs.append(o_b[0:cs] + o_b[cs:2 * cs] + o_b[2 * cs:3 * cs] + o_b[3 * cs:])
        s_cur = s_cur * jnp.exp(lw_end) + _mm_split(s_cur, tb[0:wg]) + s_v
    state[...] = s_cur
    o = jnp.concatenate(outs, axis=0)

    inv = 1.0 / HEAD
    mean = _mm_split(o, seg) * inv
    d = o - mean
    var = _mm_split(d * d, seg) * inv
    o = d * lax.rsqrt(var + R_GN_EPS) * lnw_ref[...] + lnb_ref[...]
    o = o + _mm_split(r * k * rk_ref[...], seg) * v
    o_ref[0] = (o * g).astype(o_ref.dtype)


def _rwkv(u_r, mu, w0, w2, a0, a2, g2, k_k, k_a, r_k, ln_w, ln_b):
    b, l, _ = u_r.shape
    f = lambda t: t.astype(F32).reshape(1, -1)
    w_lora = jnp.zeros((LANES, 3 * W_GROUP), F32)
    w_lora = w_lora.at[0:32, 0:W_GROUP].set(w2.astype(F32))
    w_lora = w_lora.at[32:64, W_GROUP:2 * W_GROUP].set(a2.astype(F32))
    w_lora = w_lora.at[64:128, 2 * W_GROUP:].set(g2.astype(F32))
    hid = jnp.arange(W_GROUP) // HEAD
    seg = (hid[:, None] == hid[None, :]).astype(F32)
    vec = _full((1, W_GROUP))
    return pl.pallas_call(
        _rwkv_kernel,
        grid=(b, l // R_BLOCK),
        in_specs=[pl.BlockSpec((1, R_BLOCK, R_COLS), lambda i, j: (i, j, 0)),
                  _full((1, R_COLS)), _full((LANES, 3 * W_GROUP)),
                  vec, vec, vec, vec, vec, vec, vec, _full((W_GROUP, W_GROUP))],
        out_specs=pl.BlockSpec((1, R_BLOCK, W_GROUP), lambda i, j: (i, j, 0)),
        out_shape=jax.ShapeDtypeStruct((b, l, W_GROUP), BF16),
        scratch_shapes=[pltpu.VMEM((R_BLOCK + 8, R_COLS), F32), pltpu.VMEM((W_GROUP, W_GROUP), F32)],
        compiler_params=_params("parallel", "arbitrary"),
        name="rwkv7",
    )(u_r, f(mu), w_lora, f(w0), f(a0), f(k_k), f(k_a), f(r_k), f(ln_w), f(ln_b), seg)


def _dsa_kernel(u_ref, kvw_ref, wkv_ref, bias_ref, far_ref, o_ref,
                kt_scr, kv_scr, tail_scr, key_scr, plane_scr, mask_scr, acc_scr, mx_scr, l_scr, cut_scr, *, top_k):
    qb = pl.program_id(1)
    n = Q_BLOCK
    wide = DSA_WIDE
    seq = key_scr.shape[0] - wide

    @pl.when(qb == 0)
    def _():
        kt_scr[...] = jnp.zeros_like(kt_scr)
        kv_scr[...] = jnp.zeros_like(kv_scr)
        tail_scr[...] = jnp.zeros_like(tail_scr)
        key_scr[...] = jnp.full(key_scr.shape, INT_MIN, I32)
        plane_scr[...] = jnp.zeros_like(plane_scr)
        mask_scr[...] = jnp.full(mask_scr.shape, NEG, F32)

    u = u_ref[0]
    q = u[:, 0:W_GROUP]
    ckv = u[:, W_GROUP:W_GROUP + LANES]
    qi = u[:, W_GROUP + LANES:2 * W_GROUP + LANES]
    tail = u[:, 2 * W_GROUP + LANES:]
    ckv = ckv * lax.rsqrt(jnp.mean(ckv * ckv, axis=-1, keepdims=True) + NORM_EPS) * kvw_ref[...]
    kv = _mm(ckv, wkv_ref[...])
    start = pl.multiple_of(qb * n, n)
    end = pl.multiple_of(start + wide + n, n)
    sub = lax.broadcasted_iota(I32, (HEAD, n), 0)
    kt_scr[:, pl.ds(end - n, n)] = jnp.concatenate(
        [kv.T[0:HEAD, :], jnp.where(sub < DSA_AUX, 1.0, 0.0)], axis=0).astype(BF16)
    kv_scr[pl.ds(end - n, n), :] = kv.astype(BF16)
    tail_scr[pl.ds(end - n, n), :] = tail.astype(BF16)
    qi_t = qi.T
    w_qi = jnp.concatenate(
        [jnp.concatenate([qi_t[h * I_DIM:(h + 1) * I_DIM, :] for h in range(I_HEADS)], axis=1),
         jnp.zeros((LANES - I_DIM, I_HEADS * n), F32)], axis=0).astype(BF16)
    wi_t = tail.T[I_DIM:I_DIM + I_HEADS, :] * (I_HEADS ** -0.5 * I_DIM ** -0.5)

    q_pos = start + lax.broadcasted_iota(I32, (wide, n), 1)
    row_w = lax.broadcasted_iota(I32, (wide, n), 0)
    n_wide = (start + n + wide - 1) // wide

    def tile_start(i):
        return pl.multiple_of(end - (i + 1) * wide, n)

    def idx_tile(i, carry, check_causal=False, check_pad=False):
        ks = tile_start(i)
        s_all = jnp.dot(tail_scr[pl.ds(ks, wide), :], w_qi, preferred_element_type=F32)
        s = jnp.zeros((wide, n), F32)
        for h in range(I_HEADS):
            s = s + jnp.maximum(s_all[:, h * n:(h + 1) * n], 0.0) * wi_t[h:h + 1, :]
        s = s + 0.0
        bits = pltpu.bitcast(s, I32)
        key = jnp.where(bits < 0, bits ^ jnp.int32(0x7FFFFFFF), bits)
        idx = ks - wide + row_w
        if check_causal:
            key = jnp.where(idx <= q_pos, key, jnp.int32(INT_MIN))
        if check_pad:
            key = jnp.where(idx >= 0, key, jnp.int32(INT_MIN))
        key_scr[pl.ds(ks, wide), :] = key
        ukey = key ^ jnp.int32(INT_MIN)
        for blk in range(wide // BIT_ROWS):
            words = _bit_transpose([ukey[blk * BIT_ROWS + 8 * k:blk * BIT_ROWS + 8 * k + 8, :] for k in range(32)])
            row0 = pl.multiple_of((i * (wide // BIT_ROWS) + blk) * 8, 8)
            for b in range(32):
                plane_scr[b, pl.ds(row0, 8), :] = words[31 - b]
        return carry

    idx_tile(0, 0, check_causal=True, check_pad=True)
    lax.fori_loop(1, n_wide - 1, idx_tile, 0)

    @pl.when(n_wide > 1)
    def _():
        idx_tile(n_wide - 1, 0, check_pad=True)

    part = 64

    def count(pred):
        def body(i, acc):
            ks = tile_start(i)
            kk = key_scr[pl.ds(ks, wide), :]
            hit = pred(kk, ks - wide + row_w).astype(I32)
            for c in range(wide // part):
                acc = acc + hit[c * part:(c + 1) * part, :]
            return acc
        acc = lax.fori_loop(0, n_wide, body, jnp.zeros((part, n), I32))
        return jnp.sum(acc, axis=0, keepdims=True)

    blk_id = lax.broadcasted_iota(I32, (plane_scr.shape[1], n), 0) // 8
    live = jnp.where(blk_id < n_wide * (wide // BIT_ROWS), jnp.int32(-1), jnp.int32(0))
    above = jnp.zeros((1, n), I32)
    thr = jnp.zeros((1, n), I32)
    def n_set(words):
        return jnp.sum(lax.population_count(words), axis=0, keepdims=True)

    for b in range(31, 0, -2):
        hi = live & plane_scr[b]
        lo = live ^ hi
        t3 = hi & plane_scr[b - 1]
        t2 = hi ^ t3
        t1 = lo & plane_scr[b - 1]
        t0 = lo ^ t1
        g3 = above + n_set(t3)
        g2 = g3 + n_set(t2)
        g1 = g2 + n_set(t1)
        d3, d2, d1 = g3 >= top_k, g2 >= top_k, g1 >= top_k
        digit = jnp.where(d3, 3, jnp.where(d2, 2, jnp.where(d1, 1, 0)))
        thr = thr | lax.shift_left(digit, jnp.int32(b - 1))
        above = jnp.where(d3, above, jnp.where(d2, g3, jnp.where(d1, g2, g1)))
        live = jnp.where(d3, t3, jnp.where(d2, t2, jnp.where(d1, t1, t0)))
    thr = thr ^ jnp.int32(INT_MIN)
    n_ge = above + jnp.sum(lax.population_count(live), axis=0, keepdims=True)
    tied = jnp.logical_and(n_ge > top_k, thr != INT_MIN)
    cut_scr[0:1, :] = jnp.full((1, n), seq, I32)

    @pl.when(jnp.max(tied.astype(I32)) > 0)
    def _():
        need = top_k - above
        m = jnp.zeros((1, n), I32)
        for bit in range(int(math.log2(seq)) - 1, -1, -1):
            cand = m | jnp.int32(1 << bit)
            cnt = count(lambda kk, idx: jnp.logical_and(kk == thr, idx < cand))
            m = jnp.where(cnt < need, cand, m)
        cut_scr[0:1, :] = jnp.where(tied, m, seq)

    cut = cut_scr[0:1, :]

    def build_mask(ks):
        kk = key_scr[pl.ds(ks, wide), :]
        sel = jnp.logical_or(kk > thr, jnp.logical_and(kk == thr, ks - wide + row_w <= cut))
        sel = jnp.logical_and(sel, kk != INT_MIN)
        m_t = jnp.where(sel, 0.0, NEG)
        msk = jnp.concatenate([m_t[c * n:(c + 1) * n, :].T for c in range(wide // n)], axis=1)
        mask_scr[:, pl.ds(ks, wide)] = msk
        return msk

    lane = lax.broadcasted_iota(I32, (n, LANES), 1)

    def split(x):
        hi = x.astype(BF16).astype(F32)
        return hi, x - hi

    def q_operand(shift, far):
        rows = []
        for h in range(N_HEADS):
            op = jnp.concatenate([q[:, h * HEAD:(h + 1) * HEAD] * (HEAD ** -0.5), jnp.zeros((n, HEAD), F32)], axis=1)
            if shift is not None:
                hi, lo = split(shift[h])
                op = op + jnp.where(lane == HEAD, hi, 0.0) + jnp.where(lane == HEAD + 1, lo, 0.0)
            if far:
                hi, lo = split(jnp.full((n, 1), far_ref[h], F32))
                op = op + jnp.where(lane == HEAD + 2, hi, 0.0) + jnp.where(lane == HEAD + 3, lo, 0.0)
            rows.append(op)
        return jnp.concatenate(rows, axis=0).astype(BF16)

    def logits(i, q_op, near, first_pass):
        ks = tile_start(i)
        s_all = jnp.dot(q_op, kt_scr[:, pl.ds(ks, wide)], preferred_element_type=F32)
        msk = build_mask(ks) if first_pass else mask_scr[:, pl.ds(ks, wide)]
        out = []
        for h in range(N_HEADS):
            s = s_all[h * n:(h + 1) * n, :] + msk
            out.append(s + bias_ref[h] if near else s)
        return ks, out

    def max_tile(i, q_op, near):
        _, ss = logits(i, q_op, near, True)
        for h in range(N_HEADS):
            m = mx_scr[h * n:(h + 1) * n, :]
            for c in range(wide // n):
                m = jnp.maximum(m, ss[h][:, c * n:(c + 1) * n])
            mx_scr[h * n:(h + 1) * n, :] = m

    def sum_tile(i, q_op, near):
        ks, ss = logits(i, q_op, near, False)
        ps = []
        for h in range(N_HEADS):
            p = jnp.exp(ss[h])
            part_l = l_scr[h * n:(h + 1) * n, :]
            for c in range(wide // n):
                part_l = part_l + p[:, c * n:(c + 1) * n]
            l_scr[h * n:(h + 1) * n, :] = part_l
            ps.append(p.astype(BF16))
        acc_scr[...] += jnp.dot(jnp.concatenate(ps, axis=0), kv_scr[pl.ds(ks, wide), :], preferred_element_type=F32)

    def run(tile_fn, shift):
        tile_fn(0, q_operand(shift, False), True)
        q_far = q_operand(shift, True)

        def body(i, carry):
            tile_fn(i, q_far, False)
            return carry
        lax.fori_loop(1, n_wide, body, 0)

    mx_scr[...] = jnp.full(mx_scr.shape, NEG, F32)
    run(max_tile, None)
    neg_max = [-jnp.max(mx_scr[h * n:(h + 1) * n, :], axis=1, keepdims=True) for h in range(N_HEADS)]
    acc_scr[...] = jnp.zeros_like(acc_scr)
    l_scr[...] = jnp.zeros_like(l_scr)
    run(sum_tile, neg_max)
    o_ref[0] = jnp.concatenate(
        [acc_scr[h * n:(h + 1) * n, HEAD:] / jnp.sum(l_scr[h * n:(h + 1) * n, :], axis=1, keepdims=True)
         for h in range(N_HEADS)], axis=1).astype(o_ref.dtype)


def _t5_bucket(dist):
    n = jnp.maximum(dist, 0)
    max_exact = REL_BUCKETS // 2
    log_ratio = jnp.log(jnp.maximum(n, 1).astype(F32) / max_exact) / math.log(REL_MAX_DIST / max_exact)
    large = jnp.minimum(max_exact + (log_ratio * (REL_BUCKETS - max_exact)).astype(I32), REL_BUCKETS - 1)
    return jnp.where(n < max_exact, n, large)


def _dsa(u_c, kv_norm_w, w_uk, w_uv, rel_bias):
    b, l, _ = u_c.shape
    top_k = min(TOPK_MAX, l // 4)
    assert DSA_WIDE - Q_BLOCK + 1 >= REL_MAX_DIST and l % DSA_WIDE == 0
    dist = (jnp.arange(Q_BLOCK, dtype=I32)[:, None] + (DSA_WIDE - Q_BLOCK)) - jnp.arange(DSA_WIDE, dtype=I32)[None, :]
    onehot = _t5_bucket(dist)[None] == jnp.arange(REL_BUCKETS, dtype=I32)[:, None, None]
    table = rel_bias.astype(F32)
    bias = jnp.stack([jnp.sum(jnp.where(onehot, table[:, h][:, None, None], 0.0), axis=0) for h in range(N_HEADS)])
    w_kv = jnp.concatenate([w_uk, w_uv], axis=1).astype(F32)
    kern = functools.partial(_dsa_kernel, top_k=top_k)
    lp = l + DSA_WIDE
    return pl.pallas_call(
        kern,
        grid=(b, l // Q_BLOCK),
        in_specs=[pl.BlockSpec((1, Q_BLOCK, C_COLS), lambda i, j: (i, j, 0)),
                  _full((1, LANES)), _full((LANES, LANES)), _full((N_HEADS, Q_BLOCK, DSA_WIDE)),
                  pl.BlockSpec(memory_space=pltpu.SMEM)],
        out_specs=pl.BlockSpec((1, Q_BLOCK, W_GROUP), lambda i, j: (i, j, 0)),
        out_shape=jax.ShapeDtypeStruct((b, l, W_GROUP), BF16),
        scratch_shapes=[pltpu.VMEM((LANES, lp), BF16), pltpu.VMEM((lp, LANES), BF16), pltpu.VMEM((lp, LANES), BF16),
                        pltpu.VMEM((lp, Q_BLOCK), I32), pltpu.VMEM((32, lp // BIT_ROWS * 8, Q_BLOCK), I32),
                        pltpu.VMEM((Q_BLOCK, lp), F32),
                        pltpu.VMEM((N_HEADS * Q_BLOCK, LANES), F32), pltpu.VMEM((N_HEADS * Q_BLOCK, LANES), F32),
                        pltpu.VMEM((N_HEADS * Q_BLOCK, LANES), F32), pltpu.VMEM((8, Q_BLOCK), I32)],
        compiler_params=_params("parallel", "arbitrary"),
        name="dsa",
    )(u_c, kv_norm_w.astype(F32)[None, :], w_kv, bias, table[REL_BUCKETS - 1])


def _s5_kernel(u_ref, are_ref, aim_ref, ldt_ref, bre_ref, bim_ref, cre_ref, cim_ref, dsk_ref, gw_ref, gb_ref,
               o_ref, pre, pim, bdr, bdi, hre, him, cre_s, cim_s, tre_s, tim_s):
    first = jnp.logical_and(pl.program_id(0) == 0, pl.program_id(1) == 0)
    n = S_BLOCK
    pad = LANES

    @pl.when(first)
    def _():
        lam_re = jnp.minimum(are_ref[...], -1e-4)
        lam_im = aim_ref[...]
        dt = jnp.exp(ldt_ref[...])
        mag = jnp.exp(dt * lam_re)
        ab_re = mag * jnp.cos(dt * lam_im)
        ab_im = mag * jnp.sin(dt * lam_im)
        den = lam_re * lam_re + lam_im * lam_im
        f_re = ((ab_re - 1.0) * lam_re + ab_im * lam_im) / den
        f_im = (ab_im * lam_re - (ab_re - 1.0) * lam_im) / den
        b_re = bre_ref[...]
        b_im = bim_ref[...]
        bb_re = f_re * b_re - f_im * b_im
        bb_im = f_re * b_im + f_im * b_re
        rg = lax.broadcasted_iota(I32, (W_GROUP, S_LANES), 0) // S_GROUP_CH
        cg = lax.broadcasted_iota(I32, (W_GROUP, S_LANES), 1) // S_STATE
        same = rg == cg
        bdr[...] = jnp.where(same, jnp.concatenate([bb_re] * S_GROUPS, axis=0), 0.0)
        bdi[...] = jnp.where(same, jnp.concatenate([bb_im] * S_GROUPS, axis=0), 0.0)
        pre[0:1, :] = ab_re
        pim[0:1, :] = ab_im
        d = 1
        while d < S_WIN:
            sr = pre[d - 1:d, :]
            si = pim[d - 1:d, :]
            xr = pre[0:d, :]
            xi = pim[0:d, :]
            pre[d:2 * d, :] = xr * sr - xi * si
            pim[d:2 * d, :] = xr * si + xi * sr
            d *= 2
        hre[...] = jnp.zeros_like(hre)
        him[...] = jnp.zeros_like(him)

    @pl.when(pl.program_id(1) == 0)
    def _():
        cre_s[...] = jnp.zeros_like(cre_s)
        cim_s[...] = jnp.zeros_like(cim_s)
        tre_s[...] = jnp.zeros_like(tre_s)
        tim_s[...] = jnp.zeros_like(tim_s)

    u = u_ref[0]
    bu_re = _mm(u, bdr[...])
    bu_im = _mm(u, bdi[...])
    lo = pad - S_WIN
    hre[0, lo:pad, :] = tre_s[...]
    him[0, lo:pad, :] = tim_s[...]
    hre[0, pad:pad + n, :] = bu_re
    him[0, pad:pad + n, :] = bu_im
    tre_s[...] = bu_re[n - S_WIN:, :]
    tim_s[...] = bu_im[n - S_WIN:, :]

    def col_block(cb, c_):
        cs = pl.ds(pl.multiple_of(cb * LANES, LANES), LANES)
        src = 0
        d = 1
        while d < S_WIN:
            ar = pre[d - 1:d, cs]
            ai = pim[d - 1:d, cs]
            xr = hre[src, pl.ds(lo - d, n + S_WIN), cs]
            xi = him[src, pl.ds(lo - d, n + S_WIN), cs]
            hre[1 - src, lo:pad + n, cs] = hre[src, lo:pad + n, cs] + ar * xr - ai * xi
            him[1 - src, lo:pad + n, cs] = him[src, lo:pad + n, cs] + ar * xi + ai * xr
            src = 1 - src
            d *= 2
        ar = jnp.broadcast_to(pre[S_WIN - 1:S_WIN, cs], (S_WIN, LANES))
        ai = jnp.broadcast_to(pim[S_WIN - 1:S_WIN, cs], (S_WIN, LANES))
        pr = cre_s[:, cs]
        pi = cim_s[:, cs]
        for v in range(n // S_WIN):
            rows = slice(pad + v * S_WIN, pad + (v + 1) * S_WIN)
            nr = hre[src, rows, cs] + ar * pr - ai * pi
            ni = him[src, rows, cs] + ar * pi + ai * pr
            hre[1 - src, rows, cs] = nr
            him[1 - src, rows, cs] = ni
            pr, pi = nr, ni
        cre_s[:, cs] = pr
        cim_s[:, cs] = pi
        assert src == 1
        return c_

    lax.fori_loop(0, S_LANES // LANES, col_block, 0)

    h_re = hre[0, pad:pad + n, :]
    h_im = him[0, pad:pad + n, :]
    y = _mm(h_re, cre_ref[...]) - _mm(h_im, cim_ref[...]) + dsk_ref[...] * u
    y = jax.nn.gelu(y)
    gl = _mm(y, gw_ref[...]) + gb_ref[...]
    o_ref[0] = (gl[:, 0:W_GROUP] * jax.nn.sigmoid(gl[:, W_GROUP:])).astype(o_ref.dtype)


def _s5(u_s, a_re, a_im, b_re, b_im, c_re, c_im, d_skip, log_dt, glu_w, glu_b):
    b, l, _ = u_s.shape
    f = lambda t: t.astype(F32).reshape(1, -1)
    eye = jnp.eye(S_GROUPS, dtype=F32)
    bt = lambda t: jnp.transpose(t.astype(F32), (2, 0, 1)).reshape(S_GROUP_CH, S_LANES)
    cbd = lambda t: jnp.einsum('gpn,gh->gnhp', t.astype(F32), eye).reshape(S_LANES, W_GROUP)
    ldt = jnp.repeat(log_dt.astype(F32), S_STATE)[None, :]
    return pl.pallas_call(
        _s5_kernel,
        grid=(b, l // S_BLOCK),
        in_specs=[pl.BlockSpec((1, S_BLOCK, S_COLS), lambda i, j: (i, j, 0)),
                  _full((1, S_LANES)), _full((1, S_LANES)), _full((1, S_LANES)),
                  _full((S_GROUP_CH, S_LANES)), _full((S_GROUP_CH, S_LANES)),
                  _full((S_LANES, W_GROUP)), _full((S_LANES, W_GROUP)),
                  _full((1, W_GROUP)), _full((W_GROUP, 2 * W_GROUP)), _full((1, 2 * W_GROUP))],
        out_specs=pl.BlockSpec((1, S_BLOCK, W_GROUP), lambda i, j: (i, j, 0)),
        out_shape=jax.ShapeDtypeStruct((b, l, W_GROUP), BF16),
        scratch_shapes=[pltpu.VMEM((S_WIN, S_LANES), F32), pltpu.VMEM((S_WIN, S_LANES), F32),
                        pltpu.VMEM((W_GROUP, S_LANES), F32), pltpu.VMEM((W_GROUP, S_LANES), F32),
                        pltpu.VMEM((2, S_BLOCK + LANES, S_LANES), F32), pltpu.VMEM((2, S_BLOCK + LANES, S_LANES), F32),
                        pltpu.VMEM((S_WIN, S_LANES), F32), pltpu.VMEM((S_WIN, S_LANES), F32),
                        pltpu.VMEM((S_WIN, S_LANES), F32), pltpu.VMEM((S_WIN, S_LANES), F32)],
        compiler_params=_params("arbitrary", "arbitrary"),
        name="s5",
    )(u_s, f(a_re), f(a_im), ldt, bt(b_re), bt(b_im), cbd(c_re), cbd(c_im), f(d_skip),
      glu_w.astype(F32), f(glu_b))


def _out_proj_kernel(x_ref, ym_ref, yr_ref, yc_ref, ys_ref, w_ref, o_ref):
    acc = x_ref[...]
    for i, y_ref in enumerate((ym_ref, yr_ref, yc_ref, ys_ref)):
        acc = acc + jnp.dot(y_ref[...], w_ref[i * W_GROUP:(i + 1) * W_GROUP, :], preferred_element_type=F32)
    o_ref[...] = acc


def _out_proj(x2, ys, w_out, tm=512):
    t, d = x2.shape
    yspec = pl.BlockSpec((tm, W_GROUP), lambda i: (i, 0))
    return pl.pallas_call(
        _out_proj_kernel,
        grid=(t // tm,),
        in_specs=[pl.BlockSpec((tm, d), lambda i: (i, 0)), yspec, yspec, yspec, yspec, _full(w_out.shape)],
        out_specs=pl.BlockSpec((tm, d), lambda i: (i, 0)),
        out_shape=jax.ShapeDtypeStruct((t, d), F32),
        compiler_params=_params("parallel"),
        name="out_proj",
    )(x2, *ys, w_out)


def _moe_kernel(x_ref, nw_ref, wr_ref, br_ref, wg_ref, wu_ref, wd_ref, fw_ref, o_ref,
                t_scr, cw_scr, rk_scr, rkt_scr, cnt_scr, acc_scr, *, final_norm):
    g = pl.program_id(1)
    tm = x_ref.shape[0]
    cap = MOE_CAP
    lane = lax.broadcasted_iota(I32, (tm, LANES), 1)

    @pl.when(g == 0)
    def _():
        x = x_ref[...]
        t = x * lax.rsqrt(jnp.mean(x * x, axis=-1, keepdims=True) + NORM_EPS) * nw_ref[...]
        t_hi = t.astype(BF16)
        t_scr[...] = t_hi
        t_lo = (t - t_hi.astype(F32)).astype(BF16)
        w_r = wr_ref[...]
        w_hi = w_r.astype(BF16)
        w_lo = (w_r - w_hi.astype(F32)).astype(BF16)
        logits = (jnp.dot(t_hi, w_hi, preferred_element_type=F32) + jnp.dot(t_lo, w_hi, preferred_element_type=F32)
                  + jnp.dot(t_hi, w_lo, preferred_element_type=F32)) + br_ref[...]
        big = jnp.int32(LANES)
        is_g = lane < E_GROUPS
        gl = jnp.where(is_g, logits, -jnp.inf)
        gmax = jnp.max(gl, axis=1, keepdims=True)
        g_sel = jnp.min(jnp.where(jnp.logical_and(is_g, gl == gmax), lane, big), axis=1, keepdims=True)
        g_gate = 1.0 / jnp.sum(jnp.exp(gl - gmax), axis=1, keepdims=True)
        lo = E_GROUPS + g_sel * E_PER_GROUP
        in_g = jnp.logical_and(lane >= lo, lane < lo + E_PER_GROUP)
        el = jnp.where(in_g, logits, -jnp.inf)
        m1 = jnp.max(el, axis=1, keepdims=True)
        i1 = jnp.min(jnp.where(el == m1, lane, big), axis=1, keepdims=True)
        el2 = jnp.where(lane == i1, -jnp.inf, el)
        m2 = jnp.max(el2, axis=1, keepdims=True)
        i2 = jnp.min(jnp.where(el2 == m2, lane, big), axis=1, keepdims=True)
        e2 = jnp.exp(m2 - m1)
        w1 = 1.0 / (1.0 + e2)
        w2 = e2 / (1.0 + e2)
        cw = jnp.where(lane == i1, w1, jnp.where(lane == i2, w2, 0.0)) * g_gate
        cw_hi = cw.astype(BF16)
        cw_scr[...] = jnp.concatenate([cw_hi, (cw - cw_hi.astype(F32)).astype(BF16)], axis=1)
        member = lane == g_sel
        ones = jnp.where(member, 1.0, 0.0)
        row = lax.broadcasted_iota(I32, (LANES, LANES), 0)
        col = lax.broadcasted_iota(I32, (LANES, LANES), 1)
        before = jnp.where(row > col, 1.0, 0.0).astype(BF16)
        running = jnp.zeros((1, LANES), F32)
        ranks = []
        for j in range(tm // LANES):
            blk = ones[j * LANES:(j + 1) * LANES, :]
            ranks.append(jnp.dot(before, blk.astype(BF16), preferred_element_type=F32) + running)
            running = running + jnp.sum(blk, axis=0, keepdims=True)
        rk = jnp.where(member, jnp.concatenate(ranks, axis=0), -1.0)
        rk_scr[...] = rk
        rkt_scr[...] = rk.T[0:8, :]
        cnt_scr[0:1, :] = running
        acc_scr[...] = jnp.zeros_like(acc_scr)

    lane1 = lax.broadcasted_iota(I32, (1, LANES), 1)
    n_tok = jnp.sum(jnp.where(lane1 == g, cnt_scr[0:1, :], 0.0)).astype(I32)
    rk_row = rkt_scr[pl.ds(g, 1), :]
    rk_col = jnp.sum(jnp.where(lane == g, rk_scr[...], 0.0), axis=1, keepdims=True)
    slot_r = lax.broadcasted_iota(I32, (cap, tm), 0).astype(F32)
    slot_c = lax.broadcasted_iota(I32, (tm, cap), 1).astype(F32)
    lane_c = lax.broadcasted_iota(I32, (cap, LANES), 1)

    def one_pass(blk, carry):
        base = (blk * cap).astype(F32)
        gather = jnp.where(rk_row - base == slot_r, 1.0, 0.0).astype(BF16)
        scatter = jnp.where(rk_col - base == slot_c, 1.0, 0.0).astype(BF16)
        xg = jnp.dot(gather, t_scr[...], preferred_element_type=F32).astype(BF16)
        cwg = jnp.dot(gather, cw_scr[...], preferred_element_type=F32)
        cwg = cwg[:, 0:LANES] + cwg[:, LANES:]
        parts = []
        for e in range(E_PER_GROUP):
            hid = _silu(jnp.dot(xg, wg_ref[0, e], preferred_element_type=F32)) * jnp.dot(xg, wu_ref[0, e], preferred_element_type=F32)
            w_e = jnp.sum(jnp.where(lane_c == E_GROUPS + g * E_PER_GROUP + e, cwg, 0.0), axis=1, keepdims=True)
            parts.append((hid * w_e).astype(BF16))
        y = jnp.dot(jnp.concatenate(parts, axis=1), wd_ref[0], preferred_element_type=F32)
        acc_scr[...] += jnp.dot(scatter, y.astype(BF16), preferred_element_type=F32)
        return carry

    lax.fori_loop(0, (n_tok + cap - 1) // cap, one_pass, 0)

    @pl.when(g == pl.num_programs(1) - 1)
    def _():
        y = x_ref[...] + acc_scr[...]
        if final_norm:
            y = y * lax.rsqrt(jnp.mean(y * y, axis=-1, keepdims=True) + NORM_EPS) * fw_ref[...]
        o_ref[...] = y


def _moe(x2, norm_w, wr_g, br_g, wr_e, br_e, w_gate, w_up, w_down, final_w, final_norm, tm=1024):
    t, d = x2.shape
    wr = jnp.zeros((d, LANES), F32).at[:, 0:E_GROUPS].set(wr_g.astype(F32))
    wr = wr.at[:, E_GROUPS:E_GROUPS + N_EXPERTS].set(wr_e.astype(F32))
    br = jnp.zeros((1, LANES), F32).at[0, 0:E_GROUPS].set(br_g.astype(F32))
    br = br.at[0, E_GROUPS:E_GROUPS + N_EXPERTS].set(br_e.astype(F32))
    wide = E_PER_GROUP * D_EXPERT
    wg = w_gate.astype(BF16)
    wu = w_up.astype(BF16)
    wd = w_down.reshape(E_GROUPS, wide, d).astype(BF16)
    kern = functools.partial(_moe_kernel, final_norm=final_norm)
    return pl.pallas_call(
        kern,
        grid=(t // tm, E_GROUPS),
        in_specs=[pl.BlockSpec((tm, d), lambda i, g: (i, 0)), _full((1, d)), _full((d, LANES)), _full((1, LANES)),
                  pl.BlockSpec((1, E_PER_GROUP, d, D_EXPERT), lambda i, g: (g, 0, 0, 0)),
                  pl.BlockSpec((1, E_PER_GROUP, d, D_EXPERT), lambda i, g: (g, 0, 0, 0)),
                  pl.BlockSpec((1, wide, d), lambda i, g: (g, 0, 0)), _full((1, d))],
        out_specs=pl.BlockSpec((tm, d), lambda i, g: (i, 0)),
        out_shape=jax.ShapeDtypeStruct((t, d), F32),
        scratch_shapes=[pltpu.VMEM((tm, d), BF16), pltpu.VMEM((tm, 2 * LANES), BF16), pltpu.VMEM((tm, LANES), F32),
                        pltpu.VMEM((8, tm), F32), pltpu.VMEM((8, LANES), F32), pltpu.VMEM((tm, d), F32)],
        compiler_params=_params("parallel", "arbitrary"),
        name="moe",
    )(x2, norm_w.astype(F32)[None, :], wr, br, wg, wu, wd, final_w.astype(F32)[None, :])


def _pad_in_proj(w):
    d = w.shape[0]
    m_in, r_in = 772, 896
    c_used = 680
    z = lambda k: jnp.zeros((d, k), w.dtype)
    wm = w[:, 0:m_in]
    wr = w[:, m_in:m_in + r_in]
    wc = w[:, m_in + r_in:m_in + r_in + c_used]
    ws = w[:, m_in + r_in + c_used:]
    return jnp.concatenate([wm, z(M_COLS - m_in), wr, wc, z(C_COLS - c_used), ws], axis=1).astype(BF16)


def kernel(x, norm1_w, w_in, m_conv_w, m_conv_b, m_dt_bias, m_a_log, m_d, m_norm_w, r_mu, r_w0, r_w2, r_a0, r_a2, r_g2, r_k_k, r_k_a, r_r_k, r_ln_w, r_ln_b, c_kv_norm_w, c_w_uk, c_w_uv, rel_bias, s_a_re, s_a_im, s_b_re, s_b_im, s_c_re, s_c_im, s_d, s_log_dt, s_glu_w, s_glu_b, w_out, norm2_w, moe_wr_group, moe_br_group, moe_wr_exp, moe_br_exp, moe_w_gate, moe_w_up, moe_w_down, final_norm_w):
    bsz, seq, d = x.shape
    depth = w_in.shape[0]
    x2 = x.astype(F32).reshape(bsz * seq, d)
    for i in range(depth):
        u_m, u_r, u_c, u_s = _in_proj(x2, norm1_w[i].astype(F32)[None, :], _pad_in_proj(w_in[i]))
        sh = lambda t: t.reshape(bsz, seq, t.shape[-1])
        y_m = _mamba(sh(u_m), m_conv_w[i], m_conv_b[i], m_dt_bias[i], m_a_log[i], m_d[i], m_norm_w[i])
        y_r = _rwkv(sh(u_r), r_mu[i], r_w0[i], r_w2[i], r_a0[i], r_a2[i], r_g2[i],
                    r_k_k[i], r_k_a[i], r_r_k[i], r_ln_w[i], r_ln_b[i])
        y_c = _dsa(sh(u_c), c_kv_norm_w[i], c_w_uk[i], c_w_uv[i], rel_bias)
        y_s = _s5(sh(u_s), s_a_re[i], s_a_im[i], s_b_re[i], s_b_im[i], s_c_re[i], s_c_im[i],
                  s_d[i], s_log_dt[i], s_glu_w[i], s_glu_b[i])
        ys = [t.reshape(bsz * seq, W_GROUP) for t in (y_m, y_r, y_c, y_s)]
        x2 = _out_proj(x2, ys, w_out[i].astype(BF16))
        x2 = _moe(x2, norm2_w[i], moe_wr_group[i], moe_br_group[i], moe_wr_exp[i], moe_br_exp[i],
                  moe_w_gate[i], moe_w_up[i], moe_w_down[i], final_norm_w, final_norm=(i == depth - 1))
    return x2.reshape(bsz, seq, d).astype(x.dtype)
```

```python
import functools
import math

import jax
import jax.numpy as jnp
from jax import lax
from jax.experimental import pallas as pl
from jax.experimental.pallas import tpu as pltpu

F32 = jnp.float32
BF16 = jnp.bfloat16
I32 = jnp.int32
HIGHEST = lax.Precision.HIGHEST

LANES = 128
VMEM_LIMIT = 56 * 1024 * 1024

NORM_EPS = 1e-6
W_GROUP = 256
HEAD = 64
N_HEADS = W_GROUP // HEAD
M_CONV_CH = 512
M_CONV = 4
M_COLS = 896
R_COLS = 896
C_COLS = 768
S_COLS = 256
SSD_CHUNK = 128
SSD_BLOCK = 256
R_CHUNK = 64
R_BLOCK = 256
R_GN_EPS = 64e-5
I_HEADS = 8
I_DIM = 32
Q_BLOCK = 128
DSA_WIDE = 512
BIT_ROWS = 256
EXP_RANGE = 80.0
DSA_AUX = 4
TOPK_MAX = 256
REL_BUCKETS = 32
REL_MAX_DIST = 128
S_GROUPS = 16
S_GROUP_CH = 16
S_STATE = 64
S_LANES = S_GROUPS * S_STATE
S_BLOCK = 256
S_WIN = 8
E_GROUPS = 4
E_PER_GROUP = 8
N_EXPERTS = E_GROUPS * E_PER_GROUP
D_EXPERT = 256
MOE_CAP = 288
INT_MIN = -2 ** 31
NEG = -1e30


def _mm(a, b, precise=False):
    if precise:
        return jnp.dot(a.astype(F32), b.astype(F32), precision=HIGHEST, preferred_element_type=F32)
    return jnp.dot(a.astype(BF16), b.astype(BF16), preferred_element_type=F32)


def _mm_nt(a, b, precise=False):
    dn = (((1,), (1,)), ((), ()))
    if precise:
        return lax.dot_general(a.astype(F32), b.astype(F32), dn, precision=HIGHEST, preferred_element_type=F32)
    return lax.dot_general(a.astype(BF16), b.astype(BF16), dn, preferred_element_type=F32)


def _mm_tn(a, b, precise=False):
    dn = (((0,), (0,)), ((), ()))
    if precise:
        return lax.dot_general(a.astype(F32), b.astype(F32), dn, precision=HIGHEST, preferred_element_type=F32)
    return lax.dot_general(a.astype(BF16), b.astype(BF16), dn, preferred_element_type=F32)


def _mm_split(a, b, split_rhs=False):
    x = b if split_rhs else a
    hi = x.astype(BF16)
    lo = (x - hi.astype(F32)).astype(BF16)
    if split_rhs:
        a = a.astype(BF16)
        return jnp.dot(a, hi, preferred_element_type=F32) + jnp.dot(a, lo, preferred_element_type=F32)
    b = b.astype(BF16)
    return jnp.dot(hi, b, preferred_element_type=F32) + jnp.dot(lo, b, preferred_element_type=F32)


def _i32(v):
    return jnp.int32(v - (1 << 32) if v >= (1 << 31) else v)


def _bit_transpose(words):
    a = list(words)
    j, mask = 16, 0x0000FFFF
    while j:
        k = 0
        while k < 32:
            t = (a[k] ^ lax.shift_right_logical(a[k + j], jnp.int32(j))) & _i32(mask)
            a[k] = a[k] ^ t
            a[k + j] = a[k + j] ^ lax.shift_left(t, jnp.int32(j))
            k = (k + j + 1) & ~j
        j >>= 1
        mask = (mask ^ (mask << j)) & 0xFFFFFFFF
    return a


def _silu(x):
    return x * jax.nn.sigmoid(x)


def _softplus(x):
    return jnp.maximum(x, 0.0) + jnp.log(1.0 + jnp.exp(-jnp.abs(x)))


def _params(*sem):
    return pltpu.CompilerParams(dimension_semantics=sem, vmem_limit_bytes=VMEM_LIMIT)


def _full(shape):
    nd = len(shape)
    return pl.BlockSpec(shape, lambda *_: (0,) * nd)


def _in_proj_kernel(x_ref, nw_ref, w_ref, om_ref, or_ref, oc_ref, os_ref):
    x = x_ref[...]
    h = x * lax.rsqrt(jnp.mean(x * x, axis=-1, keepdims=True) + NORM_EPS) * nw_ref[...]
    h = h.astype(BF16)
    off = 0
    for o_ref, width in ((om_ref, M_COLS), (or_ref, R_COLS), (oc_ref, C_COLS), (os_ref, S_COLS)):
        o_ref[...] = jnp.dot(h, w_ref[:, off:off + width], preferred_element_type=F32)
        off += width


def _in_proj(x2, norm_w, w_pad, tm=512):
    t, d = x2.shape
    n = w_pad.shape[1]
    return pl.pallas_call(
        _in_proj_kernel,
        grid=(t // tm,),
        in_specs=[pl.BlockSpec((tm, d), lambda i: (i, 0)), _full((1, d)), _full((d, n))],
        out_specs=[pl.BlockSpec((tm, c), lambda i: (i, 0)) for c in (M_COLS, R_COLS, C_COLS, S_COLS)],
        out_shape=[jax.ShapeDtypeStruct((t, c), F32) for c in (M_COLS, R_COLS, C_COLS, S_COLS)],
        compiler_params=_params("parallel"),
        name="in_proj",
    )(x2, norm_w, w_pad)


def _mamba_kernel(u_ref, cw_ref, cb_ref, dtb_ref, alog_ref, dsk_ref, nw_ref, o_ref, xbuf, state):
    c = pl.program_id(1)
    n = SSD_BLOCK

    @pl.when(c == 0)
    def _():
        xbuf[0:8, :] = jnp.zeros((8, M_CONV_CH), F32)
        state[...] = jnp.zeros_like(state)

    u = u_ref[0]
    z = u[:, 0:W_GROUP]
    dtr = u[:, W_GROUP + M_CONV_CH:M_COLS]
    xbuf[8:8 + n, :] = u[:, W_GROUP:W_GROUP + M_CONV_CH]
    acc = jnp.broadcast_to(cb_ref[...], (n, M_CONV_CH))
    for j in range(M_CONV):
        acc = acc + cw_ref[j:j + 1, :] * xbuf[pl.ds(8 - (M_CONV - 1) + j, n), :]
    xbuf[0:8, :] = xbuf[n:n + 8, :]
    xc = _silu(acc)
    xs = xc[:, 0:W_GROUP]
    bm = xc[:, W_GROUP:W_GROUP + 2 * HEAD]
    cm = xc[:, W_GROUP + 2 * HEAD:]

    dt = _softplus(dtr + dtb_ref[...])
    a = dt * (-jnp.exp(alog_ref[...]))
    cs = SSD_CHUNK
    row = lax.broadcasted_iota(I32, (n, n), 0)
    col = lax.broadcasted_iota(I32, (n, n), 1)
    same_chunk = jnp.logical_and(row >= col, (row // cs) == (col // cs))
    a_cum = _mm(same_chunk.astype(F32), a, precise=True)
    a_cum_t = a_cum.T
    causal = same_chunk[0:cs, 0:cs]

    ys = []
    for h in range(N_HEADS):
        g = h // 2
        st = state[h]
        ys_h = []
        for ci in range(n // cs):
            rows = slice(ci * cs, (ci + 1) * cs)
            b_h = bm[rows, g * HEAD:(g + 1) * HEAD]
            c_h = cm[rows, g * HEAD:(g + 1) * HEAD]
            x_h = xs[rows, h * HEAD:(h + 1) * HEAD]
            xdt = x_h * dt[rows, h:h + 1]
            ac = a_cum[rows, h:h + 1]
            dec = jnp.exp(jnp.where(causal, ac - a_cum_t[h:h + 1, rows], -jnp.inf))
            y = _mm(_mm_nt(c_h, b_h) * dec, xdt)
            y = y + _mm_nt(c_h, st) * jnp.exp(ac)
            al = a_cum[ci * cs + cs - 1:ci * cs + cs, h:h + 1]
            st = st * jnp.exp(al) + _mm_tn(xdt, b_h * jnp.exp(al - ac))
            ys_h.append(y + x_h * dsk_ref[:, h * HEAD:(h + 1) * HEAD])
        state[h] = st
        ys.append(jnp.concatenate(ys_h, axis=0))
    y = jnp.concatenate(ys, axis=1) * _silu(z)
    y = y * lax.rsqrt(jnp.mean(y * y, axis=-1, keepdims=True) + NORM_EPS) * nw_ref[...]
    o_ref[0] = y.astype(o_ref.dtype)


def _mamba(u_m, conv_w, conv_b, dt_bias, a_log, d_skip, norm_w):
    b, l, _ = u_m.shape
    pad = lambda v: jnp.pad(v.astype(F32), (0, LANES - v.shape[0]))[None, :]
    return pl.pallas_call(
        _mamba_kernel,
        grid=(b, l // SSD_BLOCK),
        in_specs=[pl.BlockSpec((1, SSD_BLOCK, M_COLS), lambda i, j: (i, j, 0)),
                  _full((M_CONV, M_CONV_CH)), _full((1, M_CONV_CH)), _full((1, LANES)), _full((1, LANES)),
                  _full((1, W_GROUP)), _full((1, W_GROUP))],
        out_specs=pl.BlockSpec((1, SSD_BLOCK, W_GROUP), lambda i, j: (i, j, 0)),
        out_shape=jax.ShapeDtypeStruct((b, l, W_GROUP), BF16),
        scratch_shapes=[pltpu.VMEM((SSD_BLOCK + 8, M_CONV_CH), F32), pltpu.VMEM((N_HEADS, HEAD, HEAD), F32)],
        compiler_params=_params("parallel", "arbitrary"),
        name="mamba_ssd",
    )(u_m, conv_w.astype(F32), conv_b.astype(F32)[None, :], pad(dt_bias), pad(a_log),
      jnp.repeat(d_skip.astype(F32), HEAD)[None, :], norm_w.astype(F32)[None, :])


def _rwkv_kernel(u_ref, mu_ref, wl_ref, w0_ref, a0_ref, kk_ref, ka_ref, rk_ref, lnw_ref, lnb_ref, seg_ref,
                 o_ref, ubuf, state):
    c = pl.program_id(1)
    n = R_BLOCK
    cs = R_CHUNK

    @pl.when(c == 0)
    def _():
        ubuf[0:8, :] = jnp.zeros((8, R_COLS), F32)
        state[...] = jnp.zeros_like(state)

    u = u_ref[0]
    ubuf[8:8 + n, :] = u
    u = u + mu_ref[...] * (ubuf[pl.ds(7, n), :] - u)
    ubuf[0:8, :] = ubuf[n:n + 8, :]
    r = u[:, 0:W_GROUP]
    k = u[:, W_GROUP:2 * W_GROUP]
    v = u[:, 2 * W_GROUP:3 * W_GROUP]
    lo = u[:, 3 * W_GROUP:]
    lane = lax.broadcasted_iota(I32, lo.shape, 1)
    lo = jnp.where(lane < 32, jnp.tanh(lo), jnp.where(lane < 64, lo, jax.nn.sigmoid(lo)))
    proj = _mm(lo, wl_ref[...])
    w_log = -_softplus(-(w0_ref[...] + proj[:, 0:W_GROUP])) - 0.5
    ld = -jnp.exp(w_log)
    a_lr = jax.nn.sigmoid(a0_ref[...] + proj[:, W_GROUP:2 * W_GROUP])
    g = proj[:, 2 * W_GROUP:]
    seg = seg_ref[...]
    kk = k * kk_ref[...]
    kk = kk * lax.rsqrt(jnp.maximum(_mm_split(kk * kk, seg), 1e-24))
    k = k * (1.0 + (a_lr - 1.0) * ka_ref[...])
    a_v = -kk
    b_v = kk * a_lr

    row = lax.broadcasted_iota(I32, (n, n), 0)
    col = lax.broadcasted_iota(I32, (n, n), 1)
    tri2 = ((row >= col) & ((row // cs) == (col // cs))).astype(F32)
    lw = _mm_split(tri2, ld, split_rhs=True)
    w_inc = jnp.exp(lw)
    w_inv = jnp.exp(-lw)
    rt = r * w_inc
    at = a_v * jnp.exp(lw - ld)
    bt = b_v * w_inv
    kt = k * w_inv

    wg = W_GROUP
    r2 = lax.broadcasted_iota(I32, (wg, wg), 0)
    c2 = lax.broadcasted_iota(I32, (wg, wg), 1)
    lower = (r2 % cs) >= (c2 % cs)
    strict = (r2 % cs) > (c2 % cs)
    eye = (r2 == c2).astype(F32)
    lane_head = lax.broadcasted_iota(I32, (cs, wg), 1) // HEAD

    def blocks(xc, dtype=BF16):
        return jnp.concatenate([jnp.where(lane_head == h, xc, 0.0) for h in range(N_HEADS)], axis=0).astype(dtype)

    s_cur = state[...]
    outs = []
    for ci in range(n // cs):
        rows = slice(ci * cs, (ci + 1) * cs)
        lw_end = lw[ci * cs + cs - 1:ci * cs + cs, :]
        ratio = jnp.exp(lw_end - lw[rows, :])
        at_b, rt_b, bt_b, kt_b = blocks(at[rows, :]), blocks(rt[rows, :], F32), blocks(bt[rows, :]), blocks(kt[rows, :])
        v_b = blocks(v[rows, :])
        bh = blocks(b_v[rows, :] * ratio)
        kh = blocks(k[rows, :] * ratio)
        ar_b = jnp.concatenate([at_b, rt_b.astype(BF16)], axis=0)
        g_b = _mm_nt(ar_b, bt_b)
        g_k = _mm_nt(ar_b, kt_b)
        a_ab = jnp.where(strict, g_b[0:wg], 0.0)
        a_ak = jnp.where(strict, g_k[0:wg], 0.0)
        a_rb = jnp.where(lower, g_b[wg:], 0.0)
        a_rk = jnp.where(lower, g_k[wg:], 0.0)
        tinv = eye + a_ab
        pw = a_ab
        for _ in range(int(math.log2(cs)) - 1):
            pw = _mm(pw, pw)
            tinv = tinv + _mm(pw, tinv)
        ta = _mm(tinv, at_b)
        akv = _mm(jnp.concatenate([a_ak, a_rk], axis=0), v_b)
        pv = _mm(tinv, akv[0:wg])
        q_eff = rt_b + _mm(a_rb, ta)
        o_v = _mm(a_rb, pv) + akv[wg:]
        tb = _mm_tn(jnp.concatenate([ta, pv], axis=1), bh)
        s_v = tb[wg:] + _mm_tn(v_b, kh)
        o_b = _mm_nt(q_eff, s_cur) + o_v
        outs.append(o_b[0:cs] + o_b[cs:2 * cs] + o_b[2 * cs:3 * cs] + o_b[3 * cs:])
        s_cur = s_cur * jnp.exp(lw_end) + _mm_split(s_cur, tb[0:wg]) + s_v
    state[...] = s_cur
    o = jnp.concatenate(outs, axis=0)

    inv = 1.0 / HEAD
    mean = _mm_split(o, seg) * inv
    d = o - mean
    var = _mm_split(d * d, seg) * inv
    o = d * lax.rsqrt(var + R_GN_EPS) * lnw_ref[...] + lnb_ref[...]
    o = o + _mm_split(r * k * rk_ref[...], seg) * v
    o_ref[0] = (o * g).astype(o_ref.dtype)


def _rwkv(u_r, mu, w0, w2, a0, a2, g2, k_k, k_a, r_k, ln_w, ln_b):
    b, l, _ = u_r.shape
    f = lambda t: t.astype(F32).reshape(1, -1)
    w_lora = jnp.zeros((LANES, 3 * W_GROUP), F32)
    w_lora = w_lora.at[0:32, 0:W_GROUP].set(w2.astype(F32))
    w_lora = w_lora.at[32:64, W_GROUP:2 * W_GROUP].set(a2.astype(F32))
    w_lora = w_lora.at[64:128, 2 * W_GROUP:].set(g2.astype(F32))
    hid = jnp.arange(W_GROUP) // HEAD
    seg = (hid[:, None] == hid[None, :]).astype(F32)
    vec = _full((1, W_GROUP))
    return pl.pallas_call(
        _rwkv_kernel,
        grid=(b, l // R_BLOCK),
        in_specs=[pl.BlockSpec((1, R_BLOCK, R_COLS), lambda i, j: (i, j, 0)),
                  _full((1, R_COLS)), _full((LANES, 3 * W_GROUP)),
                  vec, vec, vec, vec, vec, vec, vec, _full((W_GROUP, W_GROUP))],
        out_specs=pl.BlockSpec((1, R_BLOCK, W_GROUP), lambda i, j: (i, j, 0)),
        out_shape=jax.ShapeDtypeStruct((b, l, W_GROUP), BF16),
        scratch_shapes=[pltpu.VMEM((R_BLOCK + 8, R_COLS), F32), pltpu.VMEM((W_GROUP, W_GROUP), F32)],
        compiler_params=_params("parallel", "arbitrary"),
        name="rwkv7",
    )(u_r, f(mu), w_lora, f(w0), f(a0), f(k_k), f(k_a), f(r_k), f(ln_w), f(ln_b), seg)


def _dsa_kernel(u_ref, kvw_ref, wkv_ref, bias_ref, far_ref, o_ref,
                kt_scr, kv_scr, tail_scr, key_scr, plane_scr, mask_scr, acc_scr, mx_scr, l_scr, cut_scr, kn_scr,
                *, top_k):
    qb = pl.program_id(1)
    n = Q_BLOCK
    wide = DSA_WIDE
    seq = key_scr.shape[0] - wide

    @pl.when(qb == 0)
    def _():
        kt_scr[...] = jnp.zeros_like(kt_scr)
        kv_scr[...] = jnp.zeros_like(kv_scr)
        tail_scr[...] = jnp.zeros_like(tail_scr)
        key_scr[...] = jnp.full(key_scr.shape, INT_MIN, I32)
        plane_scr[...] = jnp.zeros_like(plane_scr)
        mask_scr[...] = jnp.full(mask_scr.shape, NEG, F32)
        kn_scr[...] = jnp.zeros_like(kn_scr)

    u = u_ref[0]
    q = u[:, 0:W_GROUP]
    ckv = u[:, W_GROUP:W_GROUP + LANES]
    qi = u[:, W_GROUP + LANES:2 * W_GROUP + LANES]
    tail = u[:, 2 * W_GROUP + LANES:]
    ckv = ckv * lax.rsqrt(jnp.mean(ckv * ckv, axis=-1, keepdims=True) + NORM_EPS) * kvw_ref[...]
    kv = _mm(ckv, wkv_ref[...])
    start = pl.multiple_of(qb * n, n)
    end = pl.multiple_of(start + wide + n, n)
    sub = lax.broadcasted_iota(I32, (HEAD, n), 0)
    kt_scr[:, pl.ds(end - n, n)] = jnp.concatenate(
        [kv.T[0:HEAD, :], jnp.where(sub < DSA_AUX, 1.0, 0.0)], axis=0).astype(BF16)
    kv_scr[pl.ds(end - n, n), :] = kv.astype(BF16)
    tail_scr[pl.ds(end - n, n), :] = tail.astype(BF16)
    k_sq = jnp.sum(jnp.square(kv.astype(BF16).astype(F32)[:, 0:HEAD]), axis=1, keepdims=True)
    kn_scr[...] = jnp.maximum(kn_scr[...], jnp.max(k_sq))
    qi_t = qi.T
    w_qi = jnp.concatenate(
        [jnp.concatenate([qi_t[h * I_DIM:(h + 1) * I_DIM, :] for h in range(I_HEADS)], axis=1),
         jnp.zeros((LANES - I_DIM, I_HEADS * n), F32)], axis=0).astype(BF16)
    wi_t = tail.T[I_DIM:I_DIM + I_HEADS, :] * (I_HEADS ** -0.5 * I_DIM ** -0.5)

    q_pos = start + lax.broadcasted_iota(I32, (wide, n), 1)
    row_w = lax.broadcasted_iota(I32, (wide, n), 0)
    n_wide = (start + n + wide - 1) // wide

    def tile_start(i):
        return pl.multiple_of(end - (i + 1) * wide, n)

    def idx_tile(i, carry, check_causal=False, check_pad=False):
        ks = tile_start(i)
        s_all = jnp.dot(tail_scr[pl.ds(ks, wide), :], w_qi, preferred_element_type=F32)
        s = jnp.zeros((wide, n), F32)
        for h in range(I_HEADS):
            s = s + jnp.maximum(s_all[:, h * n:(h + 1) * n], 0.0) * wi_t[h:h + 1, :]
        s = s + 0.0
        bits = pltpu.bitcast(s, I32)
        key = jnp.where(bits < 0, bits ^ jnp.int32(0x7FFFFFFF), bits)
        idx = ks - wide + row_w
        if check_causal:
            key = jnp.where(idx <= q_pos, key, jnp.int32(INT_MIN))
        if check_pad:
            key = jnp.where(idx >= 0, key, jnp.int32(INT_MIN))
        key_scr[pl.ds(ks, wide), :] = key
        ukey = key ^ jnp.int32(INT_MIN)
        for blk in range(wide // BIT_ROWS):
            words = _bit_transpose([ukey[blk * BIT_ROWS + 8 * k:blk * BIT_ROWS + 8 * k + 8, :] for k in range(32)])
            row0 = pl.multiple_of((i * (wide // BIT_ROWS) + blk) * 8, 8)
            for b in range(32):
                plane_scr[b, pl.ds(row0, 8), :] = words[31 - b]
        return carry

    idx_tile(0, 0, check_causal=True, check_pad=True)
    lax.fori_loop(1, n_wide - 1, idx_tile, 0)

    @pl.when(n_wide > 1)
    def _():
        idx_tile(n_wide - 1, 0, check_pad=True)

    part = 64

    def count(pred):
        def body(i, acc):
            ks = tile_start(i)
            kk = key_scr[pl.ds(ks, wide), :]
            hit = pred(kk, ks - wide + row_w).astype(I32)
            for c in range(wide // part):
                acc = acc + hit[c * part:(c + 1) * part, :]
            return acc
        acc = lax.fori_loop(0, n_wide, body, jnp.zeros((part, n), I32))
        return jnp.sum(acc, axis=0, keepdims=True)

    blk_id = lax.broadcasted_iota(I32, (plane_scr.shape[1], n), 0) // 8
    live = jnp.where(blk_id < n_wide * (wide // BIT_ROWS), jnp.int32(-1), jnp.int32(0))
    above = jnp.zeros((1, n), I32)
    thr = jnp.zeros((1, n), I32)
    def n_set(words):
        return jnp.sum(lax.population_count(words), axis=0, keepdims=True)

    for b in range(31, 0, -2):
        hi = live & plane_scr[b]
        lo = live ^ hi
        t3 = hi & plane_scr[b - 1]
        t2 = hi ^ t3
        t1 = lo & plane_scr[b - 1]
        t0 = lo ^ t1
        g3 = above + n_set(t3)
        g2 = g3 + n_set(t2)
        g1 = g2 + n_set(t1)
        d3, d2, d1 = g3 >= top_k, g2 >= top_k, g1 >= top_k
        digit = jnp.where(d3, 3, jnp.where(d2, 2, jnp.where(d1, 1, 0)))
        thr = thr | lax.shift_left(digit, jnp.int32(b - 1))
        above = jnp.where(d3, above, jnp.where(d2, g3, jnp.where(d1, g2, g1)))
        live = jnp.where(d3, t3, jnp.where(d2, t2, jnp.where(d1, t1, t0)))
    thr = thr ^ jnp.int32(INT_MIN)
    n_ge = above + jnp.sum(lax.population_count(live), axis=0, keepdims=True)
    tied = jnp.logical_and(n_ge > top_k, thr != INT_MIN)
    cut_scr[0:1, :] = jnp.full((1, n), seq, I32)

    @pl.when(jnp.max(tied.astype(I32)) > 0)
    def _():
        need = top_k - above
        m = jnp.zeros((1, n), I32)
        for bit in range(int(math.log2(seq)) - 1, -1, -1):
            cand = m | jnp.int32(1 << bit)
            cnt = count(lambda kk, idx: jnp.logical_and(kk == thr, idx < cand))
            m = jnp.where(cnt < need, cand, m)
        cut_scr[0:1, :] = jnp.where(tied, m, seq)

    cut = cut_scr[0:1, :]

    def build_mask(ks):
        kk = key_scr[pl.ds(ks, wide), :]
        sel = jnp.logical_or(kk > thr, jnp.logical_and(kk == thr, ks - wide + row_w <= cut))
        sel = jnp.logical_and(sel, kk != INT_MIN)
        m_t = jnp.where(sel, 0.0, NEG)
        msk = jnp.concatenate([m_t[c * n:(c + 1) * n, :].T for c in range(wide // n)], axis=1)
        mask_scr[:, pl.ds(ks, wide)] = msk
        return msk

    lane = lax.broadcasted_iota(I32, (n, LANES), 1)

    def split(x):
        hi = x.astype(BF16).astype(F32)
        return hi, x - hi

    def q_operand(shift, far):
        rows = []
        for h in range(N_HEADS):
            op = jnp.concatenate([q[:, h * HEAD:(h + 1) * HEAD] * (HEAD ** -0.5), jnp.zeros((n, HEAD), F32)], axis=1)
            if shift is not None:
                hi, lo = split(shift[h])
                op = op + jnp.where(lane == HEAD, hi, 0.0) + jnp.where(lane == HEAD + 1, lo, 0.0)
            if far:
                hi, lo = split(jnp.full((n, 1), far_ref[0, h], F32))
                op = op + jnp.where(lane == HEAD + 2, hi, 0.0) + jnp.where(lane == HEAD + 3, lo, 0.0)
            rows.append(op)
        return jnp.concatenate(rows, axis=0).astype(BF16)

    def logits(i, q_op, near, first_pass):
        ks = tile_start(i)
        s_all = jnp.dot(q_op, kt_scr[:, pl.ds(ks, wide)], preferred_element_type=F32)
        msk = build_mask(ks) if first_pass else mask_scr[:, pl.ds(ks, wide)]
        out = []
        for h in range(N_HEADS):
            s = s_all[h * n:(h + 1) * n, :] + msk
            out.append(s + bias_ref[h] if near else s)
        return ks, out

    def max_tile(i, q_op, near):
        _, ss = logits(i, q_op, near, True)
        for h in range(N_HEADS):
            m = mx_scr[h * n:(h + 1) * n, :]
            for c in range(wide // n):
                m = jnp.maximum(m, ss[h][:, c * n:(c + 1) * n])
            mx_scr[h * n:(h + 1) * n, :] = m

    def sum_tile(i, q_op, near, first_pass=False):
        ks, ss = logits(i, q_op, near, first_pass)
        ps = []
        for h in range(N_HEADS):
            p = jnp.exp(ss[h])
            part_l = l_scr[h * n:(h + 1) * n, :]
            for c in range(wide // n):
                part_l = part_l + p[:, c * n:(c + 1) * n]
            l_scr[h * n:(h + 1) * n, :] = part_l
            ps.append(p.astype(BF16))
        acc_scr[...] += jnp.dot(jnp.concatenate(ps, axis=0), kv_scr[pl.ds(ks, wide), :], preferred_element_type=F32)

    def run(tile_fn, shift, *extra):
        tile_fn(0, q_operand(shift, False), True, *extra)
        q_far = q_operand(shift, True)

        def body(i, carry):
            tile_fn(i, q_far, False, *extra)
            return carry
        lax.fori_loop(1, n_wide, body, 0)

    k_max = jnp.sqrt(kn_scr[0:1, 0:1]) * 1.02
    q_norm = [jnp.sqrt(jnp.sum(jnp.square((q[:, h * HEAD:(h + 1) * HEAD] * (HEAD ** -0.5)).astype(BF16).astype(F32)),
                               axis=1, keepdims=True)) for h in range(N_HEADS)]
    bound = [q_norm[h] * k_max + far_ref[1, h] for h in range(N_HEADS)]
    spread = [2.0 * q_norm[h] * k_max + (far_ref[1, h] - far_ref[2, h]) for h in range(N_HEADS)]
    worst = jnp.max(jnp.maximum(jnp.maximum(spread[0], spread[1]), jnp.maximum(spread[2], spread[3])))
    acc_scr[...] = jnp.zeros_like(acc_scr)
    l_scr[...] = jnp.zeros_like(l_scr)

    @pl.when(worst <= EXP_RANGE)
    def _():
        run(sum_tile, [-bound[h] for h in range(N_HEADS)], True)

    @pl.when(jnp.logical_not(worst <= EXP_RANGE))
    def _():
        mx_scr[...] = jnp.full(mx_scr.shape, NEG, F32)
        run(max_tile, None)
        neg_max = [-jnp.max(mx_scr[h * n:(h + 1) * n, :], axis=1, keepdims=True) for h in range(N_HEADS)]
        run(sum_tile, neg_max)

    o_ref[0] = jnp.concatenate(
        [acc_scr[h * n:(h + 1) * n, HEAD:] / jnp.sum(l_scr[h * n:(h + 1) * n, :], axis=1, keepdims=True)
         for h in range(N_HEADS)], axis=1).astype(o_ref.dtype)


def _t5_bucket(dist):
    n = jnp.maximum(dist, 0)
    max_exact = REL_BUCKETS // 2
    log_ratio = jnp.log(jnp.maximum(n, 1).astype(F32) / max_exact) / math.log(REL_MAX_DIST / max_exact)
    large = jnp.minimum(max_exact + (log_ratio * (REL_BUCKETS - max_exact)).astype(I32), REL_BUCKETS - 1)
    return jnp.where(n < max_exact, n, large)


def _dsa(u_c, kv_norm_w, w_uk, w_uv, rel_bias):
    b, l, _ = u_c.shape
    top_k = min(TOPK_MAX, l // 4)
    assert DSA_WIDE - Q_BLOCK + 1 >= REL_MAX_DIST and l % DSA_WIDE == 0
    dist = (jnp.arange(Q_BLOCK, dtype=I32)[:, None] + (DSA_WIDE - Q_BLOCK)) - jnp.arange(DSA_WIDE, dtype=I32)[None, :]
    onehot = _t5_bucket(dist)[None] == jnp.arange(REL_BUCKETS, dtype=I32)[:, None, None]
    table = rel_bias.astype(F32)
    bias = jnp.stack([jnp.sum(jnp.where(onehot, table[:, h][:, None, None], 0.0), axis=0) for h in range(N_HEADS)])
    w_kv = jnp.concatenate([w_uk, w_uv], axis=1).astype(F32)
    kern = functools.partial(_dsa_kernel, top_k=top_k)
    lp = l + DSA_WIDE
    return pl.pallas_call(
        kern,
        grid=(b, l // Q_BLOCK),
        in_specs=[pl.BlockSpec((1, Q_BLOCK, C_COLS), lambda i, j: (i, j, 0)),
                  _full((1, LANES)), _full((LANES, LANES)), _full((N_HEADS, Q_BLOCK, DSA_WIDE)),
                  pl.BlockSpec(memory_space=pltpu.SMEM)],
        out_specs=pl.BlockSpec((1, Q_BLOCK, W_GROUP), lambda i, j: (i, j, 0)),
        out_shape=jax.ShapeDtypeStruct((b, l, W_GROUP), BF16),
        scratch_shapes=[pltpu.VMEM((LANES, lp), BF16), pltpu.VMEM((lp, LANES), BF16), pltpu.VMEM((lp, LANES), BF16),
                        pltpu.VMEM((lp, Q_BLOCK), I32), pltpu.VMEM((32, lp // BIT_ROWS * 8, Q_BLOCK), I32),
                        pltpu.VMEM((Q_BLOCK, lp), F32),
                        pltpu.VMEM((N_HEADS * Q_BLOCK, LANES), F32), pltpu.VMEM((N_HEADS * Q_BLOCK, LANES), F32),
                        pltpu.VMEM((N_HEADS * Q_BLOCK, LANES), F32), pltpu.VMEM((8, Q_BLOCK), I32),
                        pltpu.VMEM((8, LANES), F32)],
        compiler_params=_params("parallel", "arbitrary"),
        name="dsa",
    )(u_c, kv_norm_w.astype(F32)[None, :], w_kv, bias,
      jnp.stack([table[REL_BUCKETS - 1], jnp.max(table, axis=0), jnp.min(table, axis=0)]))


def _s5_kernel(u_ref, are_ref, aim_ref, ldt_ref, bre_ref, bim_ref, cre_ref, cim_ref, dsk_ref, gw_ref, gb_ref,
               o_ref, pre, pim, bdr, bdi, hre, him, cre_s, cim_s, tre_s, tim_s):
    first = jnp.logical_and(pl.program_id(0) == 0, pl.program_id(1) == 0)
    n = S_BLOCK
    pad = LANES

    @pl.when(first)
    def _():
        lam_re = jnp.minimum(are_ref[...], -1e-4)
        lam_im = aim_ref[...]
        dt = jnp.exp(ldt_ref[...])
        mag = jnp.exp(dt * lam_re)
        ab_re = mag * jnp.cos(dt * lam_im)
        ab_im = mag * jnp.sin(dt * lam_im)
        den = lam_re * lam_re + lam_im * lam_im
        f_re = ((ab_re - 1.0) * lam_re + ab_im * lam_im) / den
        f_im = (ab_im * lam_re - (ab_re - 1.0) * lam_im) / den
        b_re = bre_ref[...]
        b_im = bim_ref[...]
        bb_re = f_re * b_re - f_im * b_im
        bb_im = f_re * b_im + f_im * b_re
        rg = lax.broadcasted_iota(I32, (W_GROUP, S_LANES), 0) // S_GROUP_CH
        cg = lax.broadcasted_iota(I32, (W_GROUP, S_LANES), 1) // S_STATE
        same = rg == cg
        bdr[...] = jnp.where(same, jnp.concatenate([bb_re] * S_GROUPS, axis=0), 0.0)
        bdi[...] = jnp.where(same, jnp.concatenate([bb_im] * S_GROUPS, axis=0), 0.0)
        pre[0:1, :] = ab_re
        pim[0:1, :] = ab_im
        d = 1
        while d < S_WIN:
            sr = pre[d - 1:d, :]
            si = pim[d - 1:d, :]
            xr = pre[0:d, :]
            xi = pim[0:d, :]
            pre[d:2 * d, :] = xr * sr - xi * si
            pim[d:2 * d, :] = xr * si + xi * sr
            d *= 2
        hre[...] = jnp.zeros_like(hre)
        him[...] = jnp.zeros_like(him)

    @pl.when(pl.program_id(1) == 0)
    def _():
        cre_s[...] = jnp.zeros_like(cre_s)
        cim_s[...] = jnp.zeros_like(cim_s)
        tre_s[...] = jnp.zeros_like(tre_s)
        tim_s[...] = jnp.zeros_like(tim_s)

    u = u_ref[0]
    bu_re = _mm(u, bdr[...])
    bu_im = _mm(u, bdi[...])
    lo = pad - S_WIN
    hre[0, lo:pad, :] = tre_s[...]
    him[0, lo:pad, :] = tim_s[...]
    hre[0, pad:pad + n, :] = bu_re
    him[0, pad:pad + n, :] = bu_im
    tre_s[...] = bu_re[n - S_WIN:, :]
    tim_s[...] = bu_im[n - S_WIN:, :]

    def col_block(cb, c_):
        cs = pl.ds(pl.multiple_of(cb * LANES, LANES), LANES)
        src = 0
        d = 1
        while d < S_WIN:
            ar = pre[d - 1:d, cs]
            ai = pim[d - 1:d, cs]
            xr = hre[src, pl.ds(lo - d, n + S_WIN), cs]
            xi = him[src, pl.ds(lo - d, n + S_WIN), cs]
            hre[1 - src, lo:pad + n, cs] = hre[src, lo:pad + n, cs] + ar * xr - ai * xi
            him[1 - src, lo:pad + n, cs] = him[src, lo:pad + n, cs] + ar * xi + ai * xr
            src = 1 - src
            d *= 2
        ar = jnp.broadcast_to(pre[S_WIN - 1:S_WIN, cs], (S_WIN, LANES))
        ai = jnp.broadcast_to(pim[S_WIN - 1:S_WIN, cs], (S_WIN, LANES))
        pr = cre_s[:, cs]
        pi = cim_s[:, cs]
        for v in range(n // S_WIN):
            rows = slice(pad + v * S_WIN, pad + (v + 1) * S_WIN)
            nr = hre[src, rows, cs] + ar * pr - ai * pi
            ni = him[src, rows, cs] + ar * pi + ai * pr
            hre[1 - src, rows, cs] = nr
            him[1 - src, rows, cs] = ni
            pr, pi = nr, ni
        cre_s[:, cs] = pr
        cim_s[:, cs] = pi
        assert src == 1
        return c_

    lax.fori_loop(0, S_LANES // LANES, col_block, 0)

    h_re = hre[0, pad:pad + n, :]
    h_im = him[0, pad:pad + n, :]
    y = _mm(h_re, cre_ref[...]) - _mm(h_im, cim_ref[...]) + dsk_ref[...] * u
    y = jax.nn.gelu(y)
    gl = _mm(y, gw_ref[...]) + gb_ref[...]
    o_ref[0] = (gl[:, 0:W_GROUP] * jax.nn.sigmoid(gl[:, W_GROUP:])).astype(o_ref.dtype)


def _s5(u_s, a_re, a_im, b_re, b_im, c_re, c_im, d_skip, log_dt, glu_w, glu_b):
    b, l, _ = u_s.shape
    f = lambda t: t.astype(F32).reshape(1, -1)
    eye = jnp.eye(S_GROUPS, dtype=F32)
    bt = lambda t: jnp.transpose(t.astype(F32), (2, 0, 1)).reshape(S_GROUP_CH, S_LANES)
    cbd = lambda t: jnp.einsum('gpn,gh->gnhp', t.astype(F32), eye).reshape(S_LANES, W_GROUP)
    ldt = jnp.repeat(log_dt.astype(F32), S_STATE)[None, :]
    return pl.pallas_call(
        _s5_kernel,
        grid=(b, l // S_BLOCK),
        in_specs=[pl.BlockSpec((1, S_BLOCK, S_COLS), lambda i, j: (i, j, 0)),
                  _full((1, S_LANES)), _full((1, S_LANES)), _full((1, S_LANES)),
                  _full((S_GROUP_CH, S_LANES)), _full((S_GROUP_CH, S_LANES)),
                  _full((S_LANES, W_GROUP)), _full((S_LANES, W_GROUP)),
                  _full((1, W_GROUP)), _full((W_GROUP, 2 * W_GROUP)), _full((1, 2 * W_GROUP))],
        out_specs=pl.BlockSpec((1, S_BLOCK, W_GROUP), lambda i, j: (i, j, 0)),
        out_shape=jax.ShapeDtypeStruct((b, l, W_GROUP), BF16),
        scratch_shapes=[pltpu.VMEM((S_WIN, S_LANES), F32), pltpu.VMEM((S_WIN, S_LANES), F32),
                        pltpu.VMEM((W_GROUP, S_LANES), F32), pltpu.VMEM((W_GROUP, S_LANES), F32),
                        pltpu.VMEM((2, S_BLOCK + LANES, S_LANES), F32), pltpu.VMEM((2, S_BLOCK + LANES, S_LANES), F32),
                        pltpu.VMEM((S_WIN, S_LANES), F32), pltpu.VMEM((S_WIN, S_LANES), F32),
                        pltpu.VMEM((S_WIN, S_LANES), F32), pltpu.VMEM((S_WIN, S_LANES), F32)],
        compiler_params=_params("arbitrary", "arbitrary"),
        name="s5",
    )(u_s, f(a_re), f(a_im), ldt, bt(b_re), bt(b_im), cbd(c_re), cbd(c_im), f(d_skip),
      glu_w.astype(F32), f(glu_b))


def _out_proj_kernel(x_ref, ym_ref, yr_ref, yc_ref, ys_ref, w_ref, o_ref):
    acc = x_ref[...]
    for i, y_ref in enumerate((ym_ref, yr_ref, yc_ref, ys_ref)):
        acc = acc + jnp.dot(y_ref[...], w_ref[i * W_GROUP:(i + 1) * W_GROUP, :], preferred_element_type=F32)
    o_ref[...] = acc


def _out_proj(x2, ys, w_out, tm=512):
    t, d = x2.shape
    yspec = pl.BlockSpec((tm, W_GROUP), lambda i: (i, 0))
    return pl.pallas_call(
        _out_proj_kernel,
        grid=(t // tm,),
        in_specs=[pl.BlockSpec((tm, d), lambda i: (i, 0)), yspec, yspec, yspec, yspec, _full(w_out.shape)],
        out_specs=pl.BlockSpec((tm, d), lambda i: (i, 0)),
        out_shape=jax.ShapeDtypeStruct((t, d), F32),
        compiler_params=_params("parallel"),
        name="out_proj",
    )(x2, *ys, w_out)


def _moe_kernel(x_ref, nw_ref, wr_ref, br_ref, wg_ref, wu_ref, wd_ref, fw_ref, o_ref,
                t_scr, cw_scr, rk_scr, rkt_scr, cnt_scr, acc_scr, *, final_norm):
    g = pl.program_id(1)
    tm = x_ref.shape[0]
    cap = MOE_CAP
    lane = lax.broadcasted_iota(I32, (tm, LANES), 1)

    @pl.when(g == 0)
    def _():
        x = x_ref[...]
        t = x * lax.rsqrt(jnp.mean(x * x, axis=-1, keepdims=True) + NORM_EPS) * nw_ref[...]
        t_hi = t.astype(BF16)
        t_scr[...] = t_hi
        t_lo = (t - t_hi.astype(F32)).astype(BF16)
        w_r = wr_ref[...]
        w_hi = w_r.astype(BF16)
        w_lo = (w_r - w_hi.astype(F32)).astype(BF16)
        logits = (jnp.dot(t_hi, w_hi, preferred_element_type=F32) + jnp.dot(t_lo, w_hi, preferred_element_type=F32)
                  + jnp.dot(t_hi, w_lo, preferred_element_type=F32)) + br_ref[...]
        big = jnp.int32(LANES)
        is_g = lane < E_GROUPS
        gl = jnp.where(is_g, logits, -jnp.inf)
        gmax = jnp.max(gl, axis=1, keepdims=True)
        g_sel = jnp.min(jnp.where(jnp.logical_and(is_g, gl == gmax), lane, big), axis=1, keepdims=True)
        g_gate = 1.0 / jnp.sum(jnp.exp(gl - gmax), axis=1, keepdims=True)
        lo = E_GROUPS + g_sel * E_PER_GROUP
        in_g = jnp.logical_and(lane >= lo, lane < lo + E_PER_GROUP)
        el = jnp.where(in_g, logits, -jnp.inf)
        m1 = jnp.max(el, axis=1, keepdims=True)
        i1 = jnp.min(jnp.where(el == m1, lane, big), axis=1, keepdims=True)
        el2 = jnp.where(lane == i1, -jnp.inf, el)
        m2 = jnp.max(el2, axis=1, keepdims=True)
        i2 = jnp.min(jnp.where(el2 == m2, lane, big), axis=1, keepdims=True)
        e2 = jnp.exp(m2 - m1)
        w1 = 1.0 / (1.0 + e2)
        w2 = e2 / (1.0 + e2)
        cw = jnp.where(lane == i1, w1, jnp.where(lane == i2, w2, 0.0)) * g_gate
        cw_hi = cw.astype(BF16)
        cw_scr[...] = jnp.concatenate([cw_hi, (cw - cw_hi.astype(F32)).astype(BF16)], axis=1)
        member = lane == g_sel
        ones = jnp.where(member, 1.0, 0.0)
        row = lax.broadcasted_iota(I32, (LANES, LANES), 0)
        col = lax.broadcasted_iota(I32, (LANES, LANES), 1)
        before = jnp.where(row > col, 1.0, 0.0).astype(BF16)
        running = jnp.zeros((1, LANES), F32)
        ranks = []
        for j in range(tm // LANES):
            blk = ones[j * LANES:(j + 1) * LANES, :]
            ranks.append(jnp.dot(before, blk.astype(BF16), preferred_element_type=F32) + running)
            running = running + jnp.sum(blk, axis=0, keepdims=True)
        rk = jnp.where(member, jnp.concatenate(ranks, axis=0), -1.0)
        rk_scr[...] = rk
        rkt_scr[...] = rk.T[0:8, :]
        cnt_scr[0:1, :] = running
        acc_scr[...] = jnp.zeros_like(acc_scr)

    lane1 = lax.broadcasted_iota(I32, (1, LANES), 1)
    n_tok = jnp.sum(jnp.where(lane1 == g, cnt_scr[0:1, :], 0.0)).astype(I32)
    rk_row = rkt_scr[pl.ds(g, 1), :]
    rk_col = jnp.sum(jnp.where(lane == g, rk_scr[...], 0.0), axis=1, keepdims=True)
    slot_r = lax.broadcasted_iota(I32, (cap, tm), 0).astype(F32)
    slot_c = lax.broadcasted_iota(I32, (tm, cap), 1).astype(F32)
    lane_c = lax.broadcasted_iota(I32, (cap, LANES), 1)

    def one_pass(blk, carry):
        base = (blk * cap).astype(F32)
        gather = jnp.where(rk_row - base == slot_r, 1.0, 0.0).astype(BF16)
        scatter = jnp.where(rk_col - base == slot_c, 1.0, 0.0).astype(BF16)
        xg = jnp.dot(gather, t_scr[...], preferred_element_type=F32).astype(BF16)
        cwg = jnp.dot(gather, cw_scr[...], preferred_element_type=F32)
        cwg = cwg[:, 0:LANES] + cwg[:, LANES:]
        parts = []
        for e in range(E_PER_GROUP):
            hid = _silu(jnp.dot(xg, wg_ref[0, e], preferred_element_type=F32)) * jnp.dot(xg, wu_ref[0, e], preferred_element_type=F32)
            w_e = jnp.sum(jnp.where(lane_c == E_GROUPS + g * E_PER_GROUP + e, cwg, 0.0), axis=1, keepdims=True)
            parts.append((hid * w_e).astype(BF16))
        y = jnp.dot(jnp.concatenate(parts, axis=1), wd_ref[0], preferred_element_type=F32)
        acc_scr[...] += jnp.dot(scatter, y.astype(BF16), preferred_element_type=F32)
        return carry

    lax.fori_loop(0, (n_tok + cap - 1) // cap, one_pass, 0)

    @pl.when(g == pl.num_programs(1) - 1)
    def _():
        y = x_ref[...] + acc_scr[...]
        if final_norm:
            y = y * lax.rsqrt(jnp.mean(y * y, axis=-1, keepdims=True) + NORM_EPS) * fw_ref[...]
        o_ref[...] = y


def _moe(x2, norm_w, wr_g, br_g, wr_e, br_e, w_gate, w_up, w_down, final_w, final_norm, tm=1024):
    t, d = x2.shape
    wr = jnp.zeros((d, LANES), F32).at[:, 0:E_GROUPS].set(wr_g.astype(F32))
    wr = wr.at[:, E_GROUPS:E_GROUPS + N_EXPERTS].set(wr_e.astype(F32))
    br = jnp.zeros((1, LANES), F32).at[0, 0:E_GROUPS].set(br_g.astype(F32))
    br = br.at[0, E_GROUPS:E_GROUPS + N_EXPERTS].set(br_e.astype(F32))
    wide = E_PER_GROUP * D_EXPERT
    wg = w_gate.astype(BF16)
    wu = w_up.astype(BF16)
    wd = w_down.reshape(E_GROUPS, wide, d).astype(BF16)
    kern = functools.partial(_moe_kernel, final_norm=final_norm)
    return pl.pallas_call(
        kern,
        grid=(t // tm, E_GROUPS),
        in_specs=[pl.BlockSpec((tm, d), lambda i, g: (i, 0)), _full((1, d)), _full((d, LANES)), _full((1, LANES)),
                  pl.BlockSpec((1, E_PER_GROUP, d, D_EXPERT), lambda i, g: (g, 0, 0, 0)),
                  pl.BlockSpec((1, E_PER_GROUP, d, D_EXPERT), lambda i, g: (g, 0, 0, 0)),
                  pl.BlockSpec((1, wide, d), lambda i, g: (g, 0, 0)), _full((1, d))],
        out_specs=pl.BlockSpec((tm, d), lambda i, g: (i, 0)),
        out_shape=jax.ShapeDtypeStruct((t, d), F32),
        scratch_shapes=[pltpu.VMEM((tm, d), BF16), pltpu.VMEM((tm, 2 * LANES), BF16), pltpu.VMEM((tm, LANES), F32),
                        pltpu.VMEM((8, tm), F32), pltpu.VMEM((8, LANES), F32), pltpu.VMEM((tm, d), F32)],
        compiler_params=_params("parallel", "arbitrary"),
        name="moe",
    )(x2, norm_w.astype(F32)[None, :], wr, br, wg, wu, wd, final_w.astype(F32)[None, :])


def _pad_in_proj(w):
    d = w.shape[0]
    m_in, r_in = 772, 896
    c_used = 680
    z = lambda k: jnp.zeros((d, k), w.dtype)
    wm = w[:, 0:m_in]
    wr = w[:, m_in:m_in + r_in]
    wc = w[:, m_in + r_in:m_in + r_in + c_used]
    ws = w[:, m_in + r_in + c_used:]
    return jnp.concatenate([wm, z(M_COLS - m_in), wr, wc, z(C_COLS - c_used), ws], axis=1).astype(BF16)


def kernel(x, norm1_w, w_in, m_conv_w, m_conv_b, m_dt_bias, m_a_log, m_d, m_norm_w, r_mu, r_w0, r_w2, r_a0, r_a2, r_g2, r_k_k, r_k_a, r_r_k, r_ln_w, r_ln_b, c_kv_norm_w, c_w_uk, c_w_uv, rel_bias, s_a_re, s_a_im, s_b_re, s_b_im, s_c_re, s_c_im, s_d, s_log_dt, s_glu_w, s_glu_b, w_out, norm2_w, moe_wr_group, moe_br_group, moe_wr_exp, moe_br_exp, moe_w_gate, moe_w_up, moe_w_down, final_norm_w):
    bsz, seq, d = x.shape
    depth = w_in.shape[0]
    x2 = x.astype(F32).reshape(bsz * seq, d)
    for i in range(depth):
        u_m, u_r, u_c, u_s = _in_proj(x2, norm1_w[i].astype(F32)[None, :], _pad_in_proj(w_in[i]))
        sh = lambda t: t.reshape(bsz, seq, t.shape[-1])
        y_m = _mamba(sh(u_m), m_conv_w[i], m_conv_b[i], m_dt_bias[i], m_a_log[i], m_d[i], m_norm_w[i])
        y_r = _rwkv(sh(u_r), r_mu[i], r_w0[i], r_w2[i], r_a0[i], r_a2[i], r_g2[i],
                    r_k_k[i], r_k_a[i], r_r_k[i], r_ln_w[i], r_ln_b[i])
        y_c = _dsa(sh(u_c), c_kv_norm_w[i], c_w_uk[i], c_w_uv[i], rel_bias)
        y_s = _s5(sh(u_s), s_a_re[i], s_a_im[i], s_b_re[i], s_b_im[i], s_c_re[i], s_c_im[i],
                  s_d[i], s_log_dt[i], s_glu_w[i], s_glu_b[i])
        ys = [t.reshape(bsz * seq, W_GROUP) for t in (y_m, y_r, y_c, y_s)]
        x2 = _out_proj(x2, ys, w_out[i].astype(BF16))
        x2 = _moe(x2, norm2_w[i], moe_wr_group[i], moe_br_group[i], moe_wr_exp[i], moe_br_exp[i],
                  moe_w_gate[i], moe_w_up[i], moe_w_down[i], final_norm_w, final_norm=(i == depth - 1))
    return x2.reshape(bsz, seq, d).astype(x.dtype)
```

```python
import functools
import math

import jax
import jax.numpy as jnp
from jax import lax
from jax.experimental import pallas as pl
from jax.experimental.pallas import tpu as pltpu

F32 = jnp.float32
BF16 = jnp.bfloat16
I32 = jnp.int32
HIGHEST = lax.Precision.HIGHEST

LANES = 128
VMEM_LIMIT = 56 * 1024 * 1024

NORM_EPS = 1e-6
W_GROUP = 256
HEAD = 64
N_HEADS = W_GROUP // HEAD
M_CONV_CH = 512
M_CONV = 4
M_COLS = 896
R_COLS = 896
C_COLS = 768
S_COLS = 256
SSD_CHUNK = 128
SSD_BLOCK = 128
R_CHUNK = 64
R_BLOCK = 256
R_GN_EPS = 64e-5
I_HEADS = 8
I_DIM = 32
Q_BLOCK = 128
DSA_WIDE = 512
BIT_ROWS = 256
EXP_RANGE = 80.0
DSA_AUX = 4
TOPK_MAX = 256
REL_BUCKETS = 32
REL_MAX_DIST = 128
S_GROUPS = 16
S_GROUP_CH = 16
S_STATE = 64
S_LANES = S_GROUPS * S_STATE
S_BLOCK = 256
S_WIN = 8
E_GROUPS = 4
E_PER_GROUP = 8
N_EXPERTS = E_GROUPS * E_PER_GROUP
D_EXPERT = 256
MOE_CAP = 288
INT_MIN = -2 ** 31
NEG = -1e30


def _mm(a, b, precise=False):
    if precise:
        return jnp.dot(a.astype(F32), b.astype(F32), precision=HIGHEST, preferred_element_type=F32)
    return jnp.dot(a.astype(BF16), b.astype(BF16), preferred_element_type=F32)


def _mm_nt(a, b, precise=False):
    dn = (((1,), (1,)), ((), ()))
    if precise:
        return lax.dot_general(a.astype(F32), b.astype(F32), dn, precision=HIGHEST, preferred_element_type=F32)
    return lax.dot_general(a.astype(BF16), b.astype(BF16), dn, preferred_element_type=F32)


def _mm_tn(a, b, precise=False):
    dn = (((0,), (0,)), ((), ()))
    if precise:
        return lax.dot_general(a.astype(F32), b.astype(F32), dn, precision=HIGHEST, preferred_element_type=F32)
    return lax.dot_general(a.astype(BF16), b.astype(BF16), dn, preferred_element_type=F32)


def _mm_split(a, b, split_rhs=False):
    x = b if split_rhs else a
    hi = x.astype(BF16)
    lo = (x - hi.astype(F32)).astype(BF16)
    if split_rhs:
        a = a.astype(BF16)
        return jnp.dot(a, hi, preferred_element_type=F32) + jnp.dot(a, lo, preferred_element_type=F32)
    b = b.astype(BF16)
    return jnp.dot(hi, b, preferred_element_type=F32) + jnp.dot(lo, b, preferred_element_type=F32)


def _i32(v):
    return jnp.int32(v - (1 << 32) if v >= (1 << 31) else v)


def _bit_transpose(words):
    a = list(words)
    j, mask = 16, 0x0000FFFF
    while j:
        k = 0
        while k < 32:
            t = (a[k] ^ lax.shift_right_logical(a[k + j], jnp.int32(j))) & _i32(mask)
            a[k] = a[k] ^ t
            a[k + j] = a[k + j] ^ lax.shift_left(t, jnp.int32(j))
            k = (k + j + 1) & ~j
        j >>= 1
        mask = (mask ^ (mask << j)) & 0xFFFFFFFF
    return a


def _silu(x):
    return x * jax.nn.sigmoid(x)


def _softplus(x):
    return jnp.maximum(x, 0.0) + jnp.log(1.0 + jnp.exp(-jnp.abs(x)))


def _params(*sem):
    return pltpu.CompilerParams(dimension_semantics=sem, vmem_limit_bytes=VMEM_LIMIT)


def _full(shape):
    nd = len(shape)
    return pl.BlockSpec(shape, lambda *_: (0,) * nd)


def _in_proj_kernel(x_ref, nw_ref, w_ref, om_ref, or_ref, oc_ref, os_ref):
    x = x_ref[...]
    h = x * lax.rsqrt(jnp.mean(x * x, axis=-1, keepdims=True) + NORM_EPS) * nw_ref[...]
    h = h.astype(BF16)
    off = 0
    for o_ref, width in ((om_ref, M_COLS), (or_ref, R_COLS), (oc_ref, C_COLS), (os_ref, S_COLS)):
        o_ref[...] = jnp.dot(h, w_ref[:, off:off + width], preferred_element_type=F32)
        off += width


def _in_proj(x2, norm_w, w_pad, tm=512):
    t, d = x2.shape
    n = w_pad.shape[1]
    return pl.pallas_call(
        _in_proj_kernel,
        grid=(t // tm,),
        in_specs=[pl.BlockSpec((tm, d), lambda i: (i, 0)), _full((1, d)), _full((d, n))],
        out_specs=[pl.BlockSpec((tm, c), lambda i: (i, 0)) for c in (M_COLS, R_COLS, C_COLS, S_COLS)],
        out_shape=[jax.ShapeDtypeStruct((t, c), F32) for c in (M_COLS, R_COLS, C_COLS, S_COLS)],
        compiler_params=_params("parallel"),
        name="in_proj",
    )(x2, norm_w, w_pad)


def _mamba_kernel(u_ref, cw_ref, cb_ref, dtb_ref, alog_ref, dsk_ref, nw_ref, o_ref, xbuf, state):
    c = pl.program_id(1)
    n = SSD_BLOCK

    @pl.when(c == 0)
    def _():
        xbuf[0:8, :] = jnp.zeros((8, M_CONV_CH), F32)
        state[...] = jnp.zeros_like(state)

    u = u_ref[0]
    z = u[:, 0:W_GROUP]
    dtr = u[:, W_GROUP + M_CONV_CH:M_COLS]
    xbuf[8:8 + n, :] = u[:, W_GROUP:W_GROUP + M_CONV_CH]
    acc = jnp.broadcast_to(cb_ref[...], (n, M_CONV_CH))
    for j in range(M_CONV):
        acc = acc + cw_ref[j:j + 1, :] * xbuf[pl.ds(8 - (M_CONV - 1) + j, n), :]
    xbuf[0:8, :] = xbuf[n:n + 8, :]
    xc = _silu(acc)
    xs = xc[:, 0:W_GROUP]
    bm = xc[:, W_GROUP:W_GROUP + 2 * HEAD]
    cm = xc[:, W_GROUP + 2 * HEAD:]

    dt = _softplus(dtr + dtb_ref[...])
    a = dt * (-jnp.exp(alog_ref[...]))
    cs = SSD_CHUNK
    row = lax.broadcasted_iota(I32, (n, n), 0)
    col = lax.broadcasted_iota(I32, (n, n), 1)
    same_chunk = jnp.logical_and(row >= col, (row // cs) == (col // cs))
    a_cum = _mm(same_chunk.astype(F32), a, precise=True)
    a_cum_t = a_cum.T
    causal = same_chunk[0:cs, 0:cs]

    ys = []
    for h in range(N_HEADS):
        g = h // 2
        st = state[h]
        ys_h = []
        for ci in range(n // cs):
            rows = slice(ci * cs, (ci + 1) * cs)
            b_h = bm[rows, g * HEAD:(g + 1) * HEAD]
            c_h = cm[rows, g * HEAD:(g + 1) * HEAD]
            x_h = xs[rows, h * HEAD:(h + 1) * HEAD]
            xdt = x_h * dt[rows, h:h + 1]
            ac = a_cum[rows, h:h + 1]
            dec = jnp.exp(jnp.where(causal, ac - a_cum_t[h:h + 1, rows], -jnp.inf))
            y = _mm(_mm_nt(c_h, b_h) * dec, xdt)
            y = y + _mm_nt(c_h, st) * jnp.exp(ac)
            al = a_cum[ci * cs + cs - 1:ci * cs + cs, h:h + 1]
            st = st * jnp.exp(al) + _mm_tn(xdt, b_h * jnp.exp(al - ac))
            ys_h.append(y + x_h * dsk_ref[:, h * HEAD:(h + 1) * HEAD])
        state[h] = st
        ys.append(jnp.concatenate(ys_h, axis=0))
    y = jnp.concatenate(ys, axis=1) * _silu(z)
    y = y * lax.rsqrt(jnp.mean(y * y, axis=-1, keepdims=True) + NORM_EPS) * nw_ref[...]
    o_ref[0] = y.astype(o_ref.dtype)


def _mamba(u_m, conv_w, conv_b, dt_bias, a_log, d_skip, norm_w):
    b, l, _ = u_m.shape
    pad = lambda v: jnp.pad(v.astype(F32), (0, LANES - v.shape[0]))[None, :]
    return pl.pallas_call(
        _mamba_kernel,
        grid=(b, l // SSD_BLOCK),
        in_specs=[pl.BlockSpec((1, SSD_BLOCK, M_COLS), lambda i, j: (i, j, 0)),
                  _full((M_CONV, M_CONV_CH)), _full((1, M_CONV_CH)), _full((1, LANES)), _full((1, LANES)),
                  _full((1, W_GROUP)), _full((1, W_GROUP))],
        out_specs=pl.BlockSpec((1, SSD_BLOCK, W_GROUP), lambda i, j: (i, j, 0)),
        out_shape=jax.ShapeDtypeStruct((b, l, W_GROUP), BF16),
        scratch_shapes=[pltpu.VMEM((SSD_BLOCK + 8, M_CONV_CH), F32), pltpu.VMEM((N_HEADS, HEAD, HEAD), F32)],
        compiler_params=_params("parallel", "arbitrary"),
        name="mamba_ssd",
    )(u_m, conv_w.astype(F32), conv_b.astype(F32)[None, :], pad(dt_bias), pad(a_log),
      jnp.repeat(d_skip.astype(F32), HEAD)[None, :], norm_w.astype(F32)[None, :])


def _rwkv_kernel(u_ref, mu_ref, wl_ref, w0_ref, a0_ref, kk_ref, ka_ref, rk_ref, lnw_ref, lnb_ref, seg_ref,
                 o_ref, ubuf, state):
    c = pl.program_id(1)
    n = R_BLOCK
    cs = R_CHUNK

    @pl.when(c == 0)
    def _():
        ubuf[0:8, :] = jnp.zeros((8, R_COLS), F32)
        state[...] = jnp.zeros_like(state)

    u = u_ref[0]
    ubuf[8:8 + n, :] = u
    u = u + mu_ref[...] * (ubuf[pl.ds(7, n), :] - u)
    ubuf[0:8, :] = ubuf[n:n + 8, :]
    r = u[:, 0:W_GROUP]
    k = u[:, W_GROUP:2 * W_GROUP]
    v = u[:, 2 * W_GROUP:3 * W_GROUP]
    lo = u[:, 3 * W_GROUP:]
    lane = lax.broadcasted_iota(I32, lo.shape, 1)
    lo = jnp.where(lane < 32, jnp.tanh(lo), jnp.where(lane < 64, lo, jax.nn.sigmoid(lo)))
    proj = _mm(lo, wl_ref[...])
    w_log = -_softplus(-(w0_ref[...] + proj[:, 0:W_GROUP])) - 0.5
    ld = -jnp.exp(w_log)
    a_lr = jax.nn.sigmoid(a0_ref[...] + proj[:, W_GROUP:2 * W_GROUP])
    g = proj[:, 2 * W_GROUP:]
    seg = seg_ref[...]
    kk = k * kk_ref[...]
    kk = kk * lax.rsqrt(jnp.maximum(_mm_split(kk * kk, seg), 1e-24))
    k = k * (1.0 + (a_lr - 1.0) * ka_ref[...])
    a_v = -kk
    b_v = kk * a_lr

    row = lax.broadcasted_iota(I32, (n, n), 0)
    col = lax.broadcasted_iota(I32, (n, n), 1)
    tri2 = ((row >= col) & ((row // cs) == (col // cs))).astype(F32)
    lw = _mm_split(tri2, ld, split_rhs=True)
    w_inc = jnp.exp(lw)
    w_inv = jnp.exp(-lw)
    rt = r * w_inc
    at = a_v * jnp.exp(lw - ld)
    bt = b_v * w_inv
    kt = k * w_inv

    wg = W_GROUP
    r2 = lax.broadcasted_iota(I32, (wg, wg), 0)
    c2 = lax.broadcasted_iota(I32, (wg, wg), 1)
    lower = (r2 % cs) >= (c2 % cs)
    strict = (r2 % cs) > (c2 % cs)
    eye = (r2 == c2).astype(F32)
    lane_head = lax.broadcasted_iota(I32, (cs, wg), 1) // HEAD

    def blocks(xc, dtype=BF16):
        return jnp.concatenate([jnp.where(lane_head == h, xc, 0.0) for h in range(N_HEADS)], axis=0).astype(dtype)

    s_cur = state[...]
    outs = []
    for ci in range(n // cs):
        rows = slice(ci * cs, (ci + 1) * cs)
        lw_end = lw[ci * cs + cs - 1:ci * cs + cs, :]
        ratio = jnp.exp(lw_end - lw[rows, :])
        at_b, rt_b, bt_b, kt_b = blocks(at[rows, :]), blocks(rt[rows, :], F32), blocks(bt[rows, :]), blocks(kt[rows, :])
        v_b = blocks(v[rows, :])
        bh = blocks(b_v[rows, :] * ratio)
        kh = blocks(k[rows, :] * ratio)
        ar_b = jnp.concatenate([at_b, rt_b.astype(BF16)], axis=0)
        g_b = _mm_nt(ar_b, bt_b)
        g_k = _mm_nt(ar_b, kt_b)
        a_ab = jnp.where(strict, g_b[0:wg], 0.0)
        a_ak = jnp.where(strict, g_k[0:wg], 0.0)
        a_rb = jnp.where(lower, g_b[wg:], 0.0)
        a_rk = jnp.where(lower, g_k[wg:], 0.0)
        tinv = eye + a_ab
        pw = a_ab
        for _ in range(int(math.log2(cs)) - 1):
            pw = _mm(pw, pw)
            tinv = tinv + _mm(pw, tinv)
        ta = _mm(tinv, at_b)
        akv = _mm(jnp.concatenate([a_ak, a_rk], axis=0), v_b)
        pv = _mm(tinv, akv[0:wg])
        q_eff = rt_b + _mm(a_rb, ta)
        o_v = _mm(a_rb, pv) + akv[wg:]
        tb = _mm_tn(jnp.concatenate([ta, pv], axis=1), bh)
        s_v = tb[wg:] + _mm_tn(v_b, kh)
        o_b = _mm_nt(q_eff, s_cur) + o_v
        outs.append(o_b[0:cs] + o_b[cs:2 * cs] + o_b[2 * cs:3 * cs] + o_b[3 * cs:])
        s_cur = s_cur * jnp.exp(lw_end) + _mm_split(s_cur, tb[0:wg]) + s_v
    state[...] = s_cur
    o = jnp.concatenate(outs, axis=0)

    inv = 1.0 / HEAD
    mean = _mm_split(o, seg) * inv
    d = o - mean
    var = _mm_split(d * d, seg) * inv
    o = d * lax.rsqrt(var + R_GN_EPS) * lnw_ref[...] + lnb_ref[...]
    o = o + _mm_split(r * k * rk_ref[...], seg) * v
    o_ref[0] = (o * g).astype(o_ref.dtype)


def _rwkv(u_r, mu, w0, w2, a0, a2, g2, k_k, k_a, r_k, ln_w, ln_b):
    b, l, _ = u_r.shape
    f = lambda t: t.astype(F32).reshape(1, -1)
    w_lora = jnp.zeros((LANES, 3 * W_GROUP), F32)
    w_lora = w_lora.at[0:32, 0:W_GROUP].set(w2.astype(F32))
    w_lora = w_lora.at[32:64, W_GROUP:2 * W_GROUP].set(a2.astype(F32))
    w_lora = w_lora.at[64:128, 2 * W_GROUP:].set(g2.astype(F32))
    hid = jnp.arange(W_GROUP) // HEAD
    seg = (hid[:, None] == hid[None, :]).astype(F32)
    vec = _full((1, W_GROUP))
    return pl.pallas_call(
        _rwkv_kernel,
        grid=(b, l // R_BLOCK),
        in_specs=[pl.BlockSpec((1, R_BLOCK, R_COLS), lambda i, j: (i, j, 0)),
                  _full((1, R_COLS)), _full((LANES, 3 * W_GROUP)),
                  vec, vec, vec, vec, vec, vec, vec, _full((W_GROUP, W_GROUP))],
        out_specs=pl.BlockSpec((1, R_BLOCK, W_GROUP), lambda i, j: (i, j, 0)),
        out_shape=jax.ShapeDtypeStruct((b, l, W_GROUP), BF16),
        scratch_shapes=[pltpu.VMEM((R_BLOCK + 8, R_COLS), F32), pltpu.VMEM((W_GROUP, W_GROUP), F32)],
        compiler_params=_params("parallel", "arbitrary"),
        name="rwkv7",
    )(u_r, f(mu), w_lora, f(w0), f(a0), f(k_k), f(k_a), f(r_k), f(ln_w), f(ln_b), seg)


def _dsa_kernel(u_ref, kvw_ref, wkv_ref, bias_ref, far_ref, o_ref,
                kt_scr, kv_scr, tail_scr, key_scr, plane_scr, mask_scr, acc_scr, mx_scr, cut_scr, kn_scr,
                *, top_k):
    qb = pl.program_id(1)
    n = Q_BLOCK
    wide = DSA_WIDE
    seq = key_scr.shape[0] - wide

    @pl.when(qb == 0)
    def _():
        kt_scr[...] = jnp.zeros_like(kt_scr)
        kv_scr[...] = jnp.zeros_like(kv_scr)
        tail_scr[...] = jnp.zeros_like(tail_scr)
        key_scr[...] = jnp.full(key_scr.shape, INT_MIN, I32)
        plane_scr[...] = jnp.zeros_like(plane_scr)
        mask_scr[...] = jnp.full(mask_scr.shape, NEG, F32)
        kn_scr[...] = jnp.zeros_like(kn_scr)

    u = u_ref[0]
    q = u[:, 0:W_GROUP]
    ckv = u[:, W_GROUP:W_GROUP + LANES]
    qi = u[:, W_GROUP + LANES:2 * W_GROUP + LANES]
    tail = u[:, 2 * W_GROUP + LANES:]
    ckv = ckv * lax.rsqrt(jnp.mean(ckv * ckv, axis=-1, keepdims=True) + NORM_EPS) * kvw_ref[...]
    kv = _mm(ckv, wkv_ref[...])
    start = pl.multiple_of(qb * n, n)
    end = pl.multiple_of(start + wide + n, n)
    sub = lax.broadcasted_iota(I32, (HEAD, n), 0)
    kt_scr[:, pl.ds(end - n, n)] = jnp.concatenate(
        [kv.T[0:HEAD, :], jnp.where(sub < DSA_AUX, 1.0, 0.0)], axis=0).astype(BF16)
    kv_scr[pl.ds(end - n, n), :] = jnp.concatenate([kv, jnp.ones((n, LANES), F32)], axis=1).astype(BF16)
    tail_scr[pl.ds(end - n, n), :] = tail.astype(BF16)
    k_sq = jnp.sum(jnp.square(kv.astype(BF16).astype(F32)[:, 0:HEAD]), axis=1, keepdims=True)
    kn_scr[...] = jnp.maximum(kn_scr[...], jnp.max(k_sq))
    qi_t = qi.T
    w_qi = jnp.concatenate(
        [jnp.concatenate([qi_t[h * I_DIM:(h + 1) * I_DIM, :] for h in range(I_HEADS)], axis=1),
         jnp.zeros((LANES - I_DIM, I_HEADS * n), F32)], axis=0).astype(BF16)
    wi_t = tail.T[I_DIM:I_DIM + I_HEADS, :] * (I_HEADS ** -0.5 * I_DIM ** -0.5)

    q_pos = start + lax.broadcasted_iota(I32, (wide, n), 1)
    row_w = lax.broadcasted_iota(I32, (wide, n), 0)
    n_wide = (start + n + wide - 1) // wide

    def tile_start(i):
        return pl.multiple_of(end - (i + 1) * wide, n)

    def idx_tile(i, carry, check_causal=False, check_pad=False):
        ks = tile_start(i)
        s_all = jnp.dot(tail_scr[pl.ds(ks, wide), :], w_qi, preferred_element_type=F32)
        s = jnp.zeros((wide, n), F32)
        for h in range(I_HEADS):
            s = s + jnp.maximum(s_all[:, h * n:(h + 1) * n], 0.0) * wi_t[h:h + 1, :]
        s = s + 0.0
        bits = pltpu.bitcast(s, I32)
        key = jnp.where(bits < 0, bits ^ jnp.int32(0x7FFFFFFF), bits)
        idx = ks - wide + row_w
        if check_causal:
            key = jnp.where(idx <= q_pos, key, jnp.int32(INT_MIN))
        if check_pad:
            key = jnp.where(idx >= 0, key, jnp.int32(INT_MIN))
        key_scr[pl.ds(ks, wide), :] = key
        ukey = key ^ jnp.int32(INT_MIN)
        for blk in range(wide // BIT_ROWS):
            words = _bit_transpose([ukey[blk * BIT_ROWS + 8 * k:blk * BIT_ROWS + 8 * k + 8, :] for k in range(32)])
            row0 = pl.multiple_of((i * (wide // BIT_ROWS) + blk) * 8, 8)
            for b in range(32):
                plane_scr[b, pl.ds(row0, 8), :] = words[31 - b]
        return carry

    idx_tile(0, 0, check_causal=True, check_pad=True)
    lax.fori_loop(1, n_wide - 1, idx_tile, 0)

    @pl.when(n_wide > 1)
    def _():
        idx_tile(n_wide - 1, 0, check_pad=True)

    part = 64

    def count(pred):
        def body(i, acc):
            ks = tile_start(i)
            kk = key_scr[pl.ds(ks, wide), :]
            hit = pred(kk, ks - wide + row_w).astype(I32)
            for c in range(wide // part):
                acc = acc + hit[c * part:(c + 1) * part, :]
            return acc
        acc = lax.fori_loop(0, n_wide, body, jnp.zeros((part, n), I32))
        return jnp.sum(acc, axis=0, keepdims=True)

    blk_id = lax.broadcasted_iota(I32, (plane_scr.shape[1], n), 0) // 8
    live = jnp.where(blk_id < n_wide * (wide // BIT_ROWS), jnp.int32(-1), jnp.int32(0))
    above = jnp.zeros((1, n), I32)
    thr = jnp.zeros((1, n), I32)
    def n_set(words):
        return jnp.sum(lax.population_count(words), axis=0, keepdims=True)

    for b in range(31, 0, -2):
        hi = live & plane_scr[b]
        lo = live ^ hi
        t3 = hi & plane_scr[b - 1]
        t2 = hi ^ t3
        t1 = lo & plane_scr[b - 1]
        t0 = lo ^ t1
        g3 = above + n_set(t3)
        g2 = g3 + n_set(t2)
        g1 = g2 + n_set(t1)
        d3, d2, d1 = g3 >= top_k, g2 >= top_k, g1 >= top_k
        digit = jnp.where(d3, 3, jnp.where(d2, 2, jnp.where(d1, 1, 0)))
        thr = thr | lax.shift_left(digit, jnp.int32(b - 1))
        above = jnp.where(d3, above, jnp.where(d2, g3, jnp.where(d1, g2, g1)))
        live = jnp.where(d3, t3, jnp.where(d2, t2, jnp.where(d1, t1, t0)))
    thr = thr ^ jnp.int32(INT_MIN)
    n_ge = above + jnp.sum(lax.population_count(live), axis=0, keepdims=True)
    tied = jnp.logical_and(n_ge > top_k, thr != INT_MIN)
    cut_scr[0:1, :] = jnp.full((1, n), seq, I32)

    @pl.when(jnp.max(tied.astype(I32)) > 0)
    def _():
        need = top_k - above
        m = jnp.zeros((1, n), I32)
        for bit in range(int(math.log2(seq)) - 1, -1, -1):
            cand = m | jnp.int32(1 << bit)
            cnt = count(lambda kk, idx: jnp.logical_and(kk == thr, idx < cand))
            m = jnp.where(cnt < need, cand, m)
        cut_scr[0:1, :] = jnp.where(tied, m, seq)

    cut = cut_scr[0:1, :]

    def build_mask(ks):
        kk = key_scr[pl.ds(ks, wide), :]
        sel = jnp.logical_or(kk > thr, jnp.logical_and(kk == thr, ks - wide + row_w <= cut))
        sel = jnp.logical_and(sel, kk != INT_MIN)
        m_t = jnp.where(sel, 0.0, NEG)
        msk = jnp.concatenate([m_t[c * n:(c + 1) * n, :].T for c in range(wide // n)], axis=1)
        mask_scr[:, pl.ds(ks, wide)] = msk
        return msk

    lane = lax.broadcasted_iota(I32, (n, LANES), 1)

    def split(x):
        hi = x.astype(BF16).astype(F32)
        return hi, x - hi

    def q_operand(shift, far):
        rows = []
        for h in range(N_HEADS):
            op = jnp.concatenate([q[:, h * HEAD:(h + 1) * HEAD] * (HEAD ** -0.5), jnp.zeros((n, HEAD), F32)], axis=1)
            if shift is not None:
                hi, lo = split(shift[h])
                op = op + jnp.where(lane == HEAD, hi, 0.0) + jnp.where(lane == HEAD + 1, lo, 0.0)
            if far:
                hi, lo = split(jnp.full((n, 1), far_ref[0, h], F32))
                op = op + jnp.where(lane == HEAD + 2, hi, 0.0) + jnp.where(lane == HEAD + 3, lo, 0.0)
            rows.append(op)
        return jnp.concatenate(rows, axis=0).astype(BF16)

    def logits(i, q_op, near, first_pass):
        ks = tile_start(i)
        s_all = jnp.dot(q_op, kt_scr[:, pl.ds(ks, wide)], preferred_element_type=F32)
        msk = build_mask(ks) if first_pass else mask_scr[:, pl.ds(ks, wide)]
        out = []
        for h in range(N_HEADS):
            s = s_all[h * n:(h + 1) * n, :] + msk
            out.append(s + bias_ref[h] if near else s)
        return ks, out

    def max_tile(i, q_op, near):
        _, ss = logits(i, q_op, near, True)
        for h in range(N_HEADS):
            m = mx_scr[h * n:(h + 1) * n, :]
            for c in range(wide // n):
                m = jnp.maximum(m, ss[h][:, c * n:(c + 1) * n])
            mx_scr[h * n:(h + 1) * n, :] = m

    def sum_tile(i, q_op, near, first_pass=False):
        ks, ss = logits(i, q_op, near, first_pass)
        ps = []
        for h in range(N_HEADS):
            ps.append(jnp.exp(ss[h]).astype(BF16))
        acc_scr[...] += jnp.dot(jnp.concatenate(ps, axis=0), kv_scr[pl.ds(ks, wide), :], preferred_element_type=F32)

    def run(tile_fn, shift, *extra):
        tile_fn(0, q_operand(shift, False), True, *extra)
        q_far = q_operand(shift, True)

        def body(i, carry):
            tile_fn(i, q_far, False, *extra)
            return carry
        lax.fori_loop(1, n_wide, body, 0)

    k_max = jnp.sqrt(kn_scr[0:1, 0:1]) * 1.02
    q_norm = [jnp.sqrt(jnp.sum(jnp.square((q[:, h * HEAD:(h + 1) * HEAD] * (HEAD ** -0.5)).astype(BF16).astype(F32)),
                               axis=1, keepdims=True)) for h in range(N_HEADS)]
    bound = [q_norm[h] * k_max + far_ref[1, h] for h in range(N_HEADS)]
    spread = [2.0 * q_norm[h] * k_max + (far_ref[1, h] - far_ref[2, h]) for h in range(N_HEADS)]
    worst = jnp.max(jnp.maximum(jnp.maximum(spread[0], spread[1]), jnp.maximum(spread[2], spread[3])))
    acc_scr[...] = jnp.zeros_like(acc_scr)

    @pl.when(worst <= EXP_RANGE)
    def _():
        run(sum_tile, [-bound[h] for h in range(N_HEADS)], True)

    @pl.when(jnp.logical_not(worst <= EXP_RANGE))
    def _():
        mx_scr[...] = jnp.full(mx_scr.shape, NEG, F32)
        run(max_tile, None)
        neg_max = [-jnp.max(mx_scr[h * n:(h + 1) * n, :], axis=1, keepdims=True) for h in range(N_HEADS)]
        run(sum_tile, neg_max)

    o_ref[0] = jnp.concatenate(
        [acc_scr[h * n:(h + 1) * n, HEAD:LANES] / acc_scr[h * n:(h + 1) * n, LANES:LANES + 1]
         for h in range(N_HEADS)], axis=1).astype(o_ref.dtype)


def _t5_bucket(dist):
    n = jnp.maximum(dist, 0)
    max_exact = REL_BUCKETS // 2
    log_ratio = jnp.log(jnp.maximum(n, 1).astype(F32) / max_exact) / math.log(REL_MAX_DIST / max_exact)
    large = jnp.minimum(max_exact + (log_ratio * (REL_BUCKETS - max_exact)).astype(I32), REL_BUCKETS - 1)
    return jnp.where(n < max_exact, n, large)


def _dsa(u_c, kv_norm_w, w_uk, w_uv, rel_bias):
    b, l, _ = u_c.shape
    top_k = min(TOPK_MAX, l // 4)
    assert DSA_WIDE - Q_BLOCK + 1 >= REL_MAX_DIST and l % DSA_WIDE == 0
    dist = (jnp.arange(Q_BLOCK, dtype=I32)[:, None] + (DSA_WIDE - Q_BLOCK)) - jnp.arange(DSA_WIDE, dtype=I32)[None, :]
    onehot = _t5_bucket(dist)[None] == jnp.arange(REL_BUCKETS, dtype=I32)[:, None, None]
    table = rel_bias.astype(F32)
    bias = jnp.stack([jnp.sum(jnp.where(onehot, table[:, h][:, None, None], 0.0), axis=0) for h in range(N_HEADS)])
    w_kv = jnp.concatenate([w_uk, w_uv], axis=1).astype(F32)
    kern = functools.partial(_dsa_kernel, top_k=top_k)
    lp = l + DSA_WIDE
    return pl.pallas_call(
        kern,
        grid=(b, l // Q_BLOCK),
        in_specs=[pl.BlockSpec((1, Q_BLOCK, C_COLS), lambda i, j: (i, j, 0)),
                  _full((1, LANES)), _full((LANES, LANES)), _full((N_HEADS, Q_BLOCK, DSA_WIDE)),
                  pl.BlockSpec(memory_space=pltpu.SMEM)],
        out_specs=pl.BlockSpec((1, Q_BLOCK, W_GROUP), lambda i, j: (i, j, 0)),
        out_shape=jax.ShapeDtypeStruct((b, l, W_GROUP), BF16),
        scratch_shapes=[pltpu.VMEM((LANES, lp), BF16), pltpu.VMEM((lp, 2 * LANES), BF16), pltpu.VMEM((lp, LANES), BF16),
                        pltpu.VMEM((lp, Q_BLOCK), I32), pltpu.VMEM((32, lp // BIT_ROWS * 8, Q_BLOCK), I32),
                        pltpu.VMEM((Q_BLOCK, lp), F32),
                        pltpu.VMEM((N_HEADS * Q_BLOCK, 2 * LANES), F32), pltpu.VMEM((N_HEADS * Q_BLOCK, LANES), F32),
                        pltpu.VMEM((8, Q_BLOCK), I32), pltpu.VMEM((8, LANES), F32)],
        compiler_params=_params("parallel", "arbitrary"),
        name="dsa",
    )(u_c, kv_norm_w.astype(F32)[None, :], w_kv, bias,
      jnp.stack([table[REL_BUCKETS - 1], jnp.max(table, axis=0), jnp.min(table, axis=0)]))


def _s5_kernel(u_ref, are_ref, aim_ref, ldt_ref, bre_ref, bim_ref, cre_ref, cim_ref, dsk_ref, gw_ref, gb_ref,
               o_ref, pre, pim, bdr, bdi, hre, him, cre_s, cim_s, tre_s, tim_s):
    first = jnp.logical_and(pl.program_id(0) == 0, pl.program_id(1) == 0)
    n = S_BLOCK
    pad = LANES

    @pl.when(first)
    def _():
        lam_re = jnp.minimum(are_ref[...], -1e-4)
        lam_im = aim_ref[...]
        dt = jnp.exp(ldt_ref[...])
        mag = jnp.exp(dt * lam_re)
        ab_re = mag * jnp.cos(dt * lam_im)
        ab_im = mag * jnp.sin(dt * lam_im)
        den = lam_re * lam_re + lam_im * lam_im
        f_re = ((ab_re - 1.0) * lam_re + ab_im * lam_im) / den
        f_im = (ab_im * lam_re - (ab_re - 1.0) * lam_im) / den
        b_re = bre_ref[...]
        b_im = bim_ref[...]
        bb_re = f_re * b_re - f_im * b_im
        bb_im = f_re * b_im + f_im * b_re
        rg = lax.broadcasted_iota(I32, (W_GROUP, S_LANES), 0) // S_GROUP_CH
        cg = lax.broadcasted_iota(I32, (W_GROUP, S_LANES), 1) // S_STATE
        same = rg == cg
        bdr[...] = jnp.where(same, jnp.concatenate([bb_re] * S_GROUPS, axis=0), 0.0)
        bdi[...] = jnp.where(same, jnp.concatenate([bb_im] * S_GROUPS, axis=0), 0.0)
        pre[0:1, :] = ab_re
        pim[0:1, :] = ab_im
        d = 1
        while d < S_WIN:
            sr = pre[d - 1:d, :]
            si = pim[d - 1:d, :]
            xr = pre[0:d, :]
            xi = pim[0:d, :]
            pre[d:2 * d, :] = xr * sr - xi * si
            pim[d:2 * d, :] = xr * si + xi * sr
            d *= 2
        hre[...] = jnp.zeros_like(hre)
        him[...] = jnp.zeros_like(him)

    @pl.when(pl.program_id(1) == 0)
    def _():
        cre_s[...] = jnp.zeros_like(cre_s)
        cim_s[...] = jnp.zeros_like(cim_s)
        tre_s[...] = jnp.zeros_like(tre_s)
        tim_s[...] = jnp.zeros_like(tim_s)

    u = u_ref[0]
    bu_re = _mm(u, bdr[...])
    bu_im = _mm(u, bdi[...])
    lo = pad - S_WIN
    hre[0, lo:pad, :] = tre_s[...]
    him[0, lo:pad, :] = tim_s[...]
    hre[0, pad:pad + n, :] = bu_re
    him[0, pad:pad + n, :] = bu_im
    tre_s[...] = bu_re[n - S_WIN:, :]
    tim_s[...] = bu_im[n - S_WIN:, :]

    def col_block(cb, c_):
        cs = pl.ds(pl.multiple_of(cb * LANES, LANES), LANES)
        src = 0
        d = 1
        while d < S_WIN:
            ar = pre[d - 1:d, cs]
            ai = pim[d - 1:d, cs]
            xr = hre[src, pl.ds(lo - d, n + S_WIN), cs]
            xi = him[src, pl.ds(lo - d, n + S_WIN), cs]
            hre[1 - src, lo:pad + n, cs] = hre[src, lo:pad + n, cs] + ar * xr - ai * xi
            him[1 - src, lo:pad + n, cs] = him[src, lo:pad + n, cs] + ar * xi + ai * xr
            src = 1 - src
            d *= 2
        ar = jnp.broadcast_to(pre[S_WIN - 1:S_WIN, cs], (S_WIN, LANES))
        ai = jnp.broadcast_to(pim[S_WIN - 1:S_WIN, cs], (S_WIN, LANES))
        pr = cre_s[:, cs]
        pi = cim_s[:, cs]
        for v in range(n // S_WIN):
            rows = slice(pad + v * S_WIN, pad + (v + 1) * S_WIN)
            nr = hre[src, rows, cs] + ar * pr - ai * pi
            ni = him[src, rows, cs] + ar * pi + ai * pr
            hre[1 - src, rows, cs] = nr
            him[1 - src, rows, cs] = ni
            pr, pi = nr, ni
        cre_s[:, cs] = pr
        cim_s[:, cs] = pi
        assert src == 1
        return c_

    lax.fori_loop(0, S_LANES // LANES, col_block, 0)

    h_re = hre[0, pad:pad + n, :]
    h_im = him[0, pad:pad + n, :]
    y = _mm(h_re, cre_ref[...]) - _mm(h_im, cim_ref[...]) + dsk_ref[...] * u
    y = jax.nn.gelu(y)
    gl = _mm(y, gw_ref[...]) + gb_ref[...]
    o_ref[0] = (gl[:, 0:W_GROUP] * jax.nn.sigmoid(gl[:, W_GROUP:])).astype(o_ref.dtype)


def _s5(u_s, a_re, a_im, b_re, b_im, c_re, c_im, d_skip, log_dt, glu_w, glu_b):
    b, l, _ = u_s.shape
    f = lambda t: t.astype(F32).reshape(1, -1)
    eye = jnp.eye(S_GROUPS, dtype=F32)
    bt = lambda t: jnp.transpose(t.astype(F32), (2, 0, 1)).reshape(S_GROUP_CH, S_LANES)
    cbd = lambda t: jnp.einsum('gpn,gh->gnhp', t.astype(F32), eye).reshape(S_LANES, W_GROUP)
    ldt = jnp.repeat(log_dt.astype(F32), S_STATE)[None, :]
    return pl.pallas_call(
        _s5_kernel,
        grid=(b, l // S_BLOCK),
        in_specs=[pl.BlockSpec((1, S_BLOCK, S_COLS), lambda i, j: (i, j, 0)),
                  _full((1, S_LANES)), _full((1, S_LANES)), _full((1, S_LANES)),
                  _full((S_GROUP_CH, S_LANES)), _full((S_GROUP_CH, S_LANES)),
                  _full((S_LANES, W_GROUP)), _full((S_LANES, W_GROUP)),
                  _full((1, W_GROUP)), _full((W_GROUP, 2 * W_GROUP)), _full((1, 2 * W_GROUP))],
        out_specs=pl.BlockSpec((1, S_BLOCK, W_GROUP), lambda i, j: (i, j, 0)),
        out_shape=jax.ShapeDtypeStruct((b, l, W_GROUP), BF16),
        scratch_shapes=[pltpu.VMEM((S_WIN, S_LANES), F32), pltpu.VMEM((S_WIN, S_LANES), F32),
                        pltpu.VMEM((W_GROUP, S_LANES), F32), pltpu.VMEM((W_GROUP, S_LANES), F32),
                        pltpu.VMEM((2, S_BLOCK + LANES, S_LANES), F32), pltpu.VMEM((2, S_BLOCK + LANES, S_LANES), F32),
                        pltpu.VMEM((S_WIN, S_LANES), F32), pltpu.VMEM((S_WIN, S_LANES), F32),
                        pltpu.VMEM((S_WIN, S_LANES), F32), pltpu.VMEM((S_WIN, S_LANES), F32)],
        compiler_params=_params("arbitrary", "arbitrary"),
        name="s5",
    )(u_s, f(a_re), f(a_im), ldt, bt(b_re), bt(b_im), cbd(c_re), cbd(c_im), f(d_skip),
      glu_w.astype(F32), f(glu_b))


def _out_proj_kernel(x_ref, ym_ref, yr_ref, yc_ref, ys_ref, w_ref, o_ref):
    acc = x_ref[...]
    for i, y_ref in enumerate((ym_ref, yr_ref, yc_ref, ys_ref)):
        acc = acc + jnp.dot(y_ref[...], w_ref[i * W_GROUP:(i + 1) * W_GROUP, :], preferred_element_type=F32)
    o_ref[...] = acc


def _out_proj(x2, ys, w_out, tm=512):
    t, d = x2.shape
    yspec = pl.BlockSpec((tm, W_GROUP), lambda i: (i, 0))
    return pl.pallas_call(
        _out_proj_kernel,
        grid=(t // tm,),
        in_specs=[pl.BlockSpec((tm, d), lambda i: (i, 0)), yspec, yspec, yspec, yspec, _full(w_out.shape)],
        out_specs=pl.BlockSpec((tm, d), lambda i: (i, 0)),
        out_shape=jax.ShapeDtypeStruct((t, d), F32),
        compiler_params=_params("parallel"),
        name="out_proj",
    )(x2, *ys, w_out)


def _moe_kernel(x_ref, nw_ref, wr_ref, br_ref, wg_ref, wu_ref, wd_ref, fw_ref, o_ref,
                t_scr, cw_scr, rk_scr, rkt_scr, cnt_scr, acc_scr, *, final_norm):
    g = pl.program_id(1)
    tm = x_ref.shape[0]
    cap = MOE_CAP
    lane = lax.broadcasted_iota(I32, (tm, LANES), 1)

    @pl.when(g == 0)
    def _():
        x = x_ref[...]
        t = x * lax.rsqrt(jnp.mean(x * x, axis=-1, keepdims=True) + NORM_EPS) * nw_ref[...]
        t_hi = t.astype(BF16)
        t_scr[...] = t_hi
        t_lo = (t - t_hi.astype(F32)).astype(BF16)
        w_r = wr_ref[...]
        w_hi = w_r.astype(BF16)
        w_lo = (w_r - w_hi.astype(F32)).astype(BF16)
        logits = (jnp.dot(t_hi, w_hi, preferred_element_type=F32) + jnp.dot(t_lo, w_hi, preferred_element_type=F32)
                  + jnp.dot(t_hi, w_lo, preferred_element_type=F32)) + br_ref[...]
        big = jnp.int32(LANES)
        is_g = lane < E_GROUPS
        gl = jnp.where(is_g, logits, -jnp.inf)
        gmax = jnp.max(gl, axis=1, keepdims=True)
        g_sel = jnp.min(jnp.where(jnp.logical_and(is_g, gl == gmax), lane, big), axis=1, keepdims=True)
        g_gate = 1.0 / jnp.sum(jnp.exp(gl - gmax), axis=1, keepdims=True)
        lo = E_GROUPS + g_sel * E_PER_GROUP
        in_g = jnp.logical_and(lane >= lo, lane < lo + E_PER_GROUP)
        el = jnp.where(in_g, logits, -jnp.inf)
        m1 = jnp.max(el, axis=1, keepdims=True)
        i1 = jnp.min(jnp.where(el == m1, lane, big), axis=1, keepdims=True)
        el2 = jnp.where(lane == i1, -jnp.inf, el)
        m2 = jnp.max(el2, axis=1, keepdims=True)
        i2 = jnp.min(jnp.where(el2 == m2, lane, big), axis=1, keepdims=True)
        e2 = jnp.exp(m2 - m1)
        w1 = 1.0 / (1.0 + e2)
        w2 = e2 / (1.0 + e2)
        cw = jnp.where(lane == i1, w1, jnp.where(lane == i2, w2, 0.0)) * g_gate
        cw_hi = cw.astype(BF16)
        cw_scr[...] = jnp.concatenate([cw_hi, (cw - cw_hi.astype(F32)).astype(BF16)], axis=1)
        member = lane == g_sel
        ones = jnp.where(member, 1.0, 0.0)
        row = lax.broadcasted_iota(I32, (LANES, LANES), 0)
        col = lax.broadcasted_iota(I32, (LANES, LANES), 1)
        before = jnp.where(row > col, 1.0, 0.0).astype(BF16)
        running = jnp.zeros((1, LANES), F32)
        ranks = []
        for j in range(tm // LANES):
            blk = ones[j * LANES:(j + 1) * LANES, :]
            ranks.append(jnp.dot(before, blk.astype(BF16), preferred_element_type=F32) + running)
            running = running + jnp.sum(blk, axis=0, keepdims=True)
        rk = jnp.where(member, jnp.concatenate(ranks, axis=0), -1.0)
        rk_scr[...] = rk
        rkt_scr[...] = rk.T[0:8, :]
        cnt_scr[0:1, :] = running
        acc_scr[...] = jnp.zeros_like(acc_scr)

    lane1 = lax.broadcasted_iota(I32, (1, LANES), 1)
    n_tok = jnp.sum(jnp.where(lane1 == g, cnt_scr[0:1, :], 0.0)).astype(I32)
    rk_row = rkt_scr[pl.ds(g, 1), :]
    rk_col = jnp.sum(jnp.where(lane == g, rk_scr[...], 0.0), axis=1, keepdims=True)
    slot_r = lax.broadcasted_iota(I32, (cap, tm), 0).astype(F32)
    slot_c = lax.broadcasted_iota(I32, (tm, cap), 1).astype(F32)
    lane_c = lax.broadcasted_iota(I32, (cap, LANES), 1)

    def one_pass(blk, carry):
        base = (blk * cap).astype(F32)
        gather = jnp.where(rk_row - base == slot_r, 1.0, 0.0).astype(BF16)
        scatter = jnp.where(rk_col - base == slot_c, 1.0, 0.0).astype(BF16)
        xg = jnp.dot(gather, t_scr[...], preferred_element_type=F32).astype(BF16)
        cwg = jnp.dot(gather, cw_scr[...], preferred_element_type=F32)
        cwg = cwg[:, 0:LANES] + cwg[:, LANES:]
        parts = []
        for e in range(E_PER_GROUP):
            hid = _silu(jnp.dot(xg, wg_ref[0, e], preferred_element_type=F32)) * jnp.dot(xg, wu_ref[0, e], preferred_element_type=F32)
            w_e = jnp.sum(jnp.where(lane_c == E_GROUPS + g * E_PER_GROUP + e, cwg, 0.0), axis=1, keepdims=True)
            parts.append((hid * w_e).astype(BF16))
        y = jnp.dot(jnp.concatenate(parts, axis=1), wd_ref[0], preferred_element_type=F32)
        acc_scr[...] += jnp.dot(scatter, y.astype(BF16), preferred_element_type=F32)
        return carry

    lax.fori_loop(0, (n_tok + cap - 1) // cap, one_pass, 0)

    @pl.when(g == pl.num_programs(1) - 1)
    def _():
        y = x_ref[...] + acc_scr[...]
        if final_norm:
            y = y * lax.rsqrt(jnp.mean(y * y, axis=-1, keepdims=True) + NORM_EPS) * fw_ref[...]
        o_ref[...] = y


def _moe(x2, norm_w, wr_g, br_g, wr_e, br_e, w_gate, w_up, w_down, final_w, final_norm, tm=1024):
    t, d = x2.shape
    wr = jnp.zeros((d, LANES), F32).at[:, 0:E_GROUPS].set(wr_g.astype(F32))
    wr = wr.at[:, E_GROUPS:E_GROUPS + N_EXPERTS].set(wr_e.astype(F32))
    br = jnp.zeros((1, LANES), F32).at[0, 0:E_GROUPS].set(br_g.astype(F32))
    br = br.at[0, E_GROUPS:E_GROUPS + N_EXPERTS].set(br_e.astype(F32))
    wide = E_PER_GROUP * D_EXPERT
    wg = w_gate.astype(BF16)
    wu = w_up.astype(BF16)
    wd = w_down.reshape(E_GROUPS, wide, d).astype(BF16)
    kern = functools.partial(_moe_kernel, final_norm=final_norm)
    return pl.pallas_call(
        kern,
        grid=(t // tm, E_GROUPS),
        in_specs=[pl.BlockSpec((tm, d), lambda i, g: (i, 0)), _full((1, d)), _full((d, LANES)), _full((1, LANES)),
                  pl.BlockSpec((1, E_PER_GROUP, d, D_EXPERT), lambda i, g: (g, 0, 0, 0)),
                  pl.BlockSpec((1, E_PER_GROUP, d, D_EXPERT), lambda i, g: (g, 0, 0, 0)),
                  pl.BlockSpec((1, wide, d), lambda i, g: (g, 0, 0)), _full((1, d))],
        out_specs=pl.BlockSpec((tm, d), lambda i, g: (i, 0)),
        out_shape=jax.ShapeDtypeStruct((t, d), F32),
        scratch_shapes=[pltpu.VMEM((tm, d), BF16), pltpu.VMEM((tm, 2 * LANES), BF16), pltpu.VMEM((tm, LANES), F32),
                        pltpu.VMEM((8, tm), F32), pltpu.VMEM((8, LANES), F32), pltpu.VMEM((tm, d), F32)],
        compiler_params=_params("parallel", "arbitrary"),
        name="moe",
    )(x2, norm_w.astype(F32)[None, :], wr, br, wg, wu, wd, final_w.astype(F32)[None, :])


def _pad_in_proj(w):
    d = w.shape[0]
    m_in, r_in = 772, 896
    c_used = 680
    z = lambda k: jnp.zeros((d, k), w.dtype)
    wm = w[:, 0:m_in]
    wr = w[:, m_in:m_in + r_in]
    wc = w[:, m_in + r_in:m_in + r_in + c_used]
    ws = w[:, m_in + r_in + c_used:]
    return jnp.concatenate([wm, z(M_COLS - m_in), wr, wc, z(C_COLS - c_used), ws], axis=1).astype(BF16)


def kernel(x, norm1_w, w_in, m_conv_w, m_conv_b, m_dt_bias, m_a_log, m_d, m_norm_w, r_mu, r_w0, r_w2, r_a0, r_a2, r_g2, r_k_k, r_k_a, r_r_k, r_ln_w, r_ln_b, c_kv_norm_w, c_w_uk, c_w_uv, rel_bias, s_a_re, s_a_im, s_b_re, s_b_im, s_c_re, s_c_im, s_d, s_log_dt, s_glu_w, s_glu_b, w_out, norm2_w, moe_wr_group, moe_br_group, moe_wr_exp, moe_br_exp, moe_w_gate, moe_w_up, moe_w_down, final_norm_w):
    bsz, seq, d = x.shape
    depth = w_in.shape[0]
    x2 = x.astype(F32).reshape(bsz * seq, d)
    for i in range(depth):
        u_m, u_r, u_c, u_s = _in_proj(x2, norm1_w[i].astype(F32)[None, :], _pad_in_proj(w_in[i]))
        sh = lambda t: t.reshape(bsz, seq, t.shape[-1])
        y_m = _mamba(sh(u_m), m_conv_w[i], m_conv_b[i], m_dt_bias[i], m_a_log[i], m_d[i], m_norm_w[i])
        y_r = _rwkv(sh(u_r), r_mu[i], r_w0[i], r_w2[i], r_a0[i], r_a2[i], r_g2[i],
                    r_k_k[i], r_k_a[i], r_r_k[i], r_ln_w[i], r_ln_b[i])
        y_c = _dsa(sh(u_c), c_kv_norm_w[i], c_w_uk[i], c_w_uv[i], rel_bias)
        y_s = _s5(sh(u_s), s_a_re[i], s_a_im[i], s_b_re[i], s_b_im[i], s_c_re[i], s_c_im[i],
                  s_d[i], s_log_dt[i], s_glu_w[i], s_glu_b[i])
        ys = [t.reshape(bsz * seq, W_GROUP) for t in (y_m, y_r, y_c, y_s)]
        x2 = _out_proj(x2, ys, w_out[i].astype(BF16))
        x2 = _moe(x2, norm2_w[i], moe_wr_group[i], moe_br_group[i], moe_wr_exp[i], moe_br_exp[i],
                  moe_w_gate[i], moe_w_up[i], moe_w_down[i], final_norm_w, final_norm=(i == depth - 1))
    return x2.reshape(bsz, seq, d).astype(x.dtype)
```

```python
import functools
import math

import jax
import jax.numpy as jnp
from jax import lax
from jax.experimental import pallas as pl
from jax.experimental.pallas import tpu as pltpu

F32 = jnp.float32
BF16 = jnp.bfloat16
I32 = jnp.int32
HIGHEST = lax.Precision.HIGHEST

LANES = 128
VMEM_LIMIT = 56 * 1024 * 1024

NORM_EPS = 1e-6
W_GROUP = 256
HEAD = 64
N_HEADS = W_GROUP // HEAD
M_CONV_CH = 512
M_CONV = 4
M_COLS = 896
R_COLS = 896
C_COLS = 768
S_COLS = 256
SSD_CHUNK = 128
SSD_BLOCK = 128
R_CHUNK = 64
R_BLOCK = 256
R_GN_EPS = 64e-5
I_HEADS = 8
I_DIM = 32
Q_BLOCK = 128
DSA_WIDE = 512
BIT_ROWS = 256
EXP_RANGE = 80.0
DSA_AUX = 4
TOPK_MAX = 256
REL_BUCKETS = 32
REL_MAX_DIST = 128
S_GROUPS = 16
S_GROUP_CH = 16
S_STATE = 64
S_LANES = S_GROUPS * S_STATE
S_BLOCK = 256
S_WIN = 8
E_GROUPS = 4
E_PER_GROUP = 8
N_EXPERTS = E_GROUPS * E_PER_GROUP
D_EXPERT = 256
MOE_CAP = 288
INT_MIN = -2 ** 31
NEG = -1e30


def _mm(a, b, precise=False):
    if precise:
        return jnp.dot(a.astype(F32), b.astype(F32), precision=HIGHEST, preferred_element_type=F32)
    return jnp.dot(a.astype(BF16), b.astype(BF16), preferred_element_type=F32)


def _mm_nt(a, b, precise=False):
    dn = (((1,), (1,)), ((), ()))
    if precise:
        return lax.dot_general(a.astype(F32), b.astype(F32), dn, precision=HIGHEST, preferred_element_type=F32)
    return lax.dot_general(a.astype(BF16), b.astype(BF16), dn, preferred_element_type=F32)


def _mm_tn(a, b, precise=False):
    dn = (((0,), (0,)), ((), ()))
    if precise:
        return lax.dot_general(a.astype(F32), b.astype(F32), dn, precision=HIGHEST, preferred_element_type=F32)
    return lax.dot_general(a.astype(BF16), b.astype(BF16), dn, preferred_element_type=F32)


def _mm_split(a, b, split_rhs=False):
    x = b if split_rhs else a
    hi = x.astype(BF16)
    lo = (x - hi.astype(F32)).astype(BF16)
    if split_rhs:
        a = a.astype(BF16)
        return jnp.dot(a, hi, preferred_element_type=F32) + jnp.dot(a, lo, preferred_element_type=F32)
    b = b.astype(BF16)
    return jnp.dot(hi, b, preferred_element_type=F32) + jnp.dot(lo, b, preferred_element_type=F32)


def _i32(v):
    return jnp.int32(v - (1 << 32) if v >= (1 << 31) else v)


def _bit_transpose(words):
    a = list(words)
    j, mask = 16, 0x0000FFFF
    while j:
        k = 0
        while k < 32:
            t = (a[k] ^ lax.shift_right_logical(a[k + j], jnp.int32(j))) & _i32(mask)
            a[k] = a[k] ^ t
            a[k + j] = a[k + j] ^ lax.shift_left(t, jnp.int32(j))
            k = (k + j + 1) & ~j
        j >>= 1
        mask = (mask ^ (mask << j)) & 0xFFFFFFFF
    return a


def _silu(x):
    return x * jax.nn.sigmoid(x)


def _softplus(x):
    return jnp.maximum(x, 0.0) + jnp.log(1.0 + jnp.exp(-jnp.abs(x)))


def _params(*sem):
    return pltpu.CompilerParams(dimension_semantics=sem, vmem_limit_bytes=VMEM_LIMIT)


def _full(shape):
    nd = len(shape)
    return pl.BlockSpec(shape, lambda *_: (0,) * nd)


def _in_proj_kernel(x_ref, nw_ref, w_ref, om_ref, or_ref, oc_ref, os_ref):
    x = x_ref[...]
    h = x * lax.rsqrt(jnp.mean(x * x, axis=-1, keepdims=True) + NORM_EPS) * nw_ref[...]
    h = h.astype(BF16)
    off = 0
    for o_ref, width in ((om_ref, M_COLS), (or_ref, R_COLS), (oc_ref, C_COLS), (os_ref, S_COLS)):
        o_ref[...] = jnp.dot(h, w_ref[:, off:off + width], preferred_element_type=F32)
        off += width


def _in_proj(x2, norm_w, w_pad, tm=512):
    t, d = x2.shape
    n = w_pad.shape[1]
    return pl.pallas_call(
        _in_proj_kernel,
        grid=(t // tm,),
        in_specs=[pl.BlockSpec((tm, d), lambda i: (i, 0)), _full((1, d)), _full((d, n))],
        out_specs=[pl.BlockSpec((tm, c), lambda i: (i, 0)) for c in (M_COLS, R_COLS, C_COLS, S_COLS)],
        out_shape=[jax.ShapeDtypeStruct((t, c), F32) for c in (M_COLS, R_COLS, C_COLS, S_COLS)],
        compiler_params=_params("parallel"),
        name="in_proj",
    )(x2, norm_w, w_pad)


def _mamba_kernel(u_ref, cw_ref, cb_ref, dtb_ref, alog_ref, dsk_ref, nw_ref, o_ref, xbuf, state):
    c = pl.program_id(1)
    n = SSD_BLOCK

    @pl.when(c == 0)
    def _():
        xbuf[0:8, :] = jnp.zeros((8, M_CONV_CH), F32)
        state[...] = jnp.zeros_like(state)

    u = u_ref[0]
    z = u[:, 0:W_GROUP]
    dtr = u[:, W_GROUP + M_CONV_CH:M_COLS]
    xbuf[8:8 + n, :] = u[:, W_GROUP:W_GROUP + M_CONV_CH]
    acc = jnp.broadcast_to(cb_ref[...], (n, M_CONV_CH))
    for j in range(M_CONV):
        acc = acc + cw_ref[j:j + 1, :] * xbuf[pl.ds(8 - (M_CONV - 1) + j, n), :]
    xbuf[0:8, :] = xbuf[n:n + 8, :]
    xc = _silu(acc)
    xs = xc[:, 0:W_GROUP]
    bm = xc[:, W_GROUP:W_GROUP + 2 * HEAD]
    cm = xc[:, W_GROUP + 2 * HEAD:]

    dt = _softplus(dtr + dtb_ref[...])
    a = dt * (-jnp.exp(alog_ref[...]))
    cs = SSD_CHUNK
    row = lax.broadcasted_iota(I32, (n, n), 0)
    col = lax.broadcasted_iota(I32, (n, n), 1)
    same_chunk = jnp.logical_and(row >= col, (row // cs) == (col // cs))
    a_cum = _mm(same_chunk.astype(F32), a, precise=True)
    a_cum_t = a_cum.T
    causal = same_chunk[0:cs, 0:cs]

    ys = []
    for h in range(N_HEADS):
        g = h // 2
        st = state[h]
        ys_h = []
        for ci in range(n // cs):
            rows = slice(ci * cs, (ci + 1) * cs)
            b_h = bm[rows, g * HEAD:(g + 1) * HEAD]
            c_h = cm[rows, g * HEAD:(g + 1) * HEAD]
            x_h = xs[rows, h * HEAD:(h + 1) * HEAD]
            xdt = x_h * dt[rows, h:h + 1]
            ac = a_cum[rows, h:h + 1]
            dec = jnp.exp(jnp.where(causal, ac - a_cum_t[h:h + 1, rows], -jnp.inf))
            y = _mm(_mm_nt(c_h, b_h) * dec, xdt)
            y = y + _mm_nt(c_h, st) * jnp.exp(ac)
            al = a_cum[ci * cs + cs - 1:ci * cs + cs, h:h + 1]
            st = st * jnp.exp(al) + _mm_tn(xdt, b_h * jnp.exp(al - ac))
            ys_h.append(y + x_h * dsk_ref[:, h * HEAD:(h + 1) * HEAD])
        state[h] = st
        ys.append(jnp.concatenate(ys_h, axis=0))
    y = jnp.concatenate(ys, axis=1) * _silu(z)
    y = y * lax.rsqrt(jnp.mean(y * y, axis=-1, keepdims=True) + NORM_EPS) * nw_ref[...]
    o_ref[0] = y.astype(o_ref.dtype)


def _mamba(u_m, conv_w, conv_b, dt_bias, a_log, d_skip, norm_w):
    b, l, _ = u_m.shape
    pad = lambda v: jnp.pad(v.astype(F32), (0, LANES - v.shape[0]))[None, :]
    return pl.pallas_call(
        _mamba_kernel,
        grid=(b, l // SSD_BLOCK),
        in_specs=[pl.BlockSpec((1, SSD_BLOCK, M_COLS), lambda i, j: (i, j, 0)),
                  _full((M_CONV, M_CONV_CH)), _full((1, M_CONV_CH)), _full((1, LANES)), _full((1, LANES)),
                  _full((1, W_GROUP)), _full((1, W_GROUP))],
        out_specs=pl.BlockSpec((1, SSD_BLOCK, W_GROUP), lambda i, j: (i, j, 0)),
        out_shape=jax.ShapeDtypeStruct((b, l, W_GROUP), BF16),
        scratch_shapes=[pltpu.VMEM((SSD_BLOCK + 8, M_CONV_CH), F32), pltpu.VMEM((N_HEADS, HEAD, HEAD), F32)],
        compiler_params=_params("parallel", "arbitrary"),
        name="mamba_ssd",
    )(u_m, conv_w.astype(F32), conv_b.astype(F32)[None, :], pad(dt_bias), pad(a_log),
      jnp.repeat(d_skip.astype(F32), HEAD)[None, :], norm_w.astype(F32)[None, :])


def _rwkv_kernel(u_ref, mu_ref, wl_ref, w0_ref, a0_ref, kk_ref, ka_ref, rk_ref, lnw_ref, lnb_ref, seg_ref,
                 o_ref, ubuf, state):
    c = pl.program_id(1)
    n = R_BLOCK
    cs = R_CHUNK

    @pl.when(c == 0)
    def _():
        ubuf[0:8, :] = jnp.zeros((8, R_COLS), F32)
        state[...] = jnp.zeros_like(state)

    u = u_ref[0]
    ubuf[8:8 + n, :] = u
    u = u + mu_ref[...] * (ubuf[pl.ds(7, n), :] - u)
    ubuf[0:8, :] = ubuf[n:n + 8, :]
    r = u[:, 0:W_GROUP]
    k = u[:, W_GROUP:2 * W_GROUP]
    v = u[:, 2 * W_GROUP:3 * W_GROUP]
    lo = u[:, 3 * W_GROUP:]
    lane = lax.broadcasted_iota(I32, lo.shape, 1)
    lo = jnp.where(lane < 32, jnp.tanh(lo), jnp.where(lane < 64, lo, jax.nn.sigmoid(lo)))
    proj = _mm(lo, wl_ref[...])
    w_log = -_softplus(-(w0_ref[...] + proj[:, 0:W_GROUP])) - 0.5
    ld = -jnp.exp(w_log)
    a_lr = jax.nn.sigmoid(a0_ref[...] + proj[:, W_GROUP:2 * W_GROUP])
    g = proj[:, 2 * W_GROUP:]
    seg = seg_ref[...]
    kk = k * kk_ref[...]
    kk = kk * lax.rsqrt(jnp.maximum(_mm_split(kk * kk, seg), 1e-24))
    k = k * (1.0 + (a_lr - 1.0) * ka_ref[...])
    a_v = -kk
    b_v = kk * a_lr

    row = lax.broadcasted_iota(I32, (n, n), 0)
    col = lax.broadcasted_iota(I32, (n, n), 1)
    tri2 = ((row >= col) & ((row // cs) == (col // cs))).astype(F32)
    lw = _mm_split(tri2, ld, split_rhs=True)
    w_inc = jnp.exp(lw)
    w_inv = jnp.exp(-lw)
    rt = r * w_inc
    at = a_v * jnp.exp(lw - ld)
    bt = b_v * w_inv
    kt = k * w_inv

    wg = W_GROUP
    r2 = lax.broadcasted_iota(I32, (wg, wg), 0)
    c2 = lax.broadcasted_iota(I32, (wg, wg), 1)
    lower = (r2 % cs) >= (c2 % cs)
    strict = (r2 % cs) > (c2 % cs)
    eye = (r2 == c2).astype(F32)
    lane_head = lax.broadcasted_iota(I32, (cs, wg), 1) // HEAD

    def blocks(xc, dtype=BF16):
        return jnp.concatenate([jnp.where(lane_head == h, xc, 0.0) for h in range(N_HEADS)], axis=0).astype(dtype)

    s_cur = state[...]
    outs = []
    for ci in range(n // cs):
        rows = slice(ci * cs, (ci + 1) * cs)
        lw_end = lw[ci * cs + cs - 1:ci * cs + cs, :]
        ratio = jnp.exp(lw_end - lw[rows, :])
        at_b, rt_b, bt_b, kt_b = blocks(at[rows, :]), blocks(rt[rows, :], F32), blocks(bt[rows, :]), blocks(kt[rows, :])
        v_b = blocks(v[rows, :])
        bh = blocks(b_v[rows, :] * ratio)
        kh = blocks(k[rows, :] * ratio)
        ar_b = jnp.concatenate([at_b, rt_b.astype(BF16)], axis=0)
        g_b = _mm_nt(ar_b, bt_b)
        g_k = _mm_nt(ar_b, kt_b)
        a_ab = jnp.where(strict, g_b[0:wg], 0.0)
        a_ak = jnp.where(strict, g_k[0:wg], 0.0)
        a_rb = jnp.where(lower, g_b[wg:], 0.0)
        a_rk = jnp.where(lower, g_k[wg:], 0.0)
        tinv = eye + a_ab
        pw = a_ab
        for _ in range(int(math.log2(cs)) - 1):
            pw = _mm(pw, pw)
            tinv = tinv + _mm(pw, tinv)
        ta = _mm(tinv, at_b)
        akv = _mm(jnp.concatenate([a_ak, a_rk], axis=0), v_b)
        pv = _mm(tinv, akv[0:wg])
        q_eff = rt_b + _mm(a_rb, ta)
        o_v = _mm(a_rb, pv) + akv[wg:]
        tb = _mm_tn(jnp.concatenate([ta, pv], axis=1), bh)
        s_v = tb[wg:] + _mm_tn(v_b, kh)
        o_b = _mm_nt(q_eff, s_cur) + o_v
        outs.append(o_b[0:cs] + o_b[cs:2 * cs] + o_b[2 * cs:3 * cs] + o_b[3 * cs:])
        s_cur = s_cur * jnp.exp(lw_end) + _mm_split(s_cur, tb[0:wg]) + s_v
    state[...] = s_cur
    o = jnp.concatenate(outs, axis=0)

    inv = 1.0 / HEAD
    mean = _mm_split(o, seg) * inv
    d = o - mean
    var = _mm_split(d * d, seg) * inv
    o = d * lax.rsqrt(var + R_GN_EPS) * lnw_ref[...] + lnb_ref[...]
    o = o + _mm_split(r * k * rk_ref[...], seg) * v
    o_ref[0] = (o * g).astype(o_ref.dtype)


def _rwkv(u_r, mu, w0, w2, a0, a2, g2, k_k, k_a, r_k, ln_w, ln_b):
    b, l, _ = u_r.shape
    f = lambda t: t.astype(F32).reshape(1, -1)
    w_lora = jnp.zeros((LANES, 3 * W_GROUP), F32)
    w_lora = w_lora.at[0:32, 0:W_GROUP].set(w2.astype(F32))
    w_lora = w_lora.at[32:64, W_GROUP:2 * W_GROUP].set(a2.astype(F32))
    w_lora = w_lora.at[64:128, 2 * W_GROUP:].set(g2.astype(F32))
    hid = jnp.arange(W_GROUP) // HEAD
    seg = (hid[:, None] == hid[None, :]).astype(F32)
    vec = _full((1, W_GROUP))
    return pl.pallas_call(
        _rwkv_kernel,
        grid=(b, l // R_BLOCK),
        in_specs=[pl.BlockSpec((1, R_BLOCK, R_COLS), lambda i, j: (i, j, 0)),
                  _full((1, R_COLS)), _full((LANES, 3 * W_GROUP)),
                  vec, vec, vec, vec, vec, vec, vec, _full((W_GROUP, W_GROUP))],
        out_specs=pl.BlockSpec((1, R_BLOCK, W_GROUP), lambda i, j: (i, j, 0)),
        out_shape=jax.ShapeDtypeStruct((b, l, W_GROUP), BF16),
        scratch_shapes=[pltpu.VMEM((R_BLOCK + 8, R_COLS), F32), pltpu.VMEM((W_GROUP, W_GROUP), F32)],
        compiler_params=_params("parallel", "arbitrary"),
        name="rwkv7",
    )(u_r, f(mu), w_lora, f(w0), f(a0), f(k_k), f(k_a), f(r_k), f(ln_w), f(ln_b), seg)


def _dsa_kernel(u_ref, kvw_ref, wkv_ref, bias_ref, far_ref, o_ref,
                kt_scr, kv_scr, tail_scr, key_scr, plane_scr, mask_scr, acc_scr, mx_scr, cut_scr, kn_scr,
                *, top_k):
    qb = pl.program_id(1)
    n = Q_BLOCK
    wide = DSA_WIDE
    seq = key_scr.shape[0] - wide

    @pl.when(qb == 0)
    def _():
        kt_scr[...] = jnp.zeros_like(kt_scr)
        kv_scr[...] = jnp.zeros_like(kv_scr)
        tail_scr[...] = jnp.zeros_like(tail_scr)
        key_scr[...] = jnp.full(key_scr.shape, INT_MIN, I32)
        plane_scr[...] = jnp.zeros_like(plane_scr)
        mask_scr[...] = jnp.full(mask_scr.shape, NEG, F32)
        kn_scr[...] = jnp.zeros_like(kn_scr)

    u = u_ref[0]
    q = u[:, 0:W_GROUP]
    ckv = u[:, W_GROUP:W_GROUP + LANES]
    qi = u[:, W_GROUP + LANES:2 * W_GROUP + LANES]
    tail = u[:, 2 * W_GROUP + LANES:]
    ckv = ckv * lax.rsqrt(jnp.mean(ckv * ckv, axis=-1, keepdims=True) + NORM_EPS) * kvw_ref[...]
    kv = _mm(ckv, wkv_ref[...])
    start = pl.multiple_of(qb * n, n)
    end = pl.multiple_of(start + wide + n, n)
    sub = lax.broadcasted_iota(I32, (HEAD, n), 0)
    kt_scr[:, pl.ds(end - n, n)] = jnp.concatenate(
        [kv.T[0:HEAD, :], jnp.where(sub < DSA_AUX, 1.0, 0.0)], axis=0).astype(BF16)
    kv_scr[pl.ds(end - n, n), :] = jnp.concatenate([kv, jnp.ones((n, LANES), F32)], axis=1).astype(BF16)
    tail_scr[pl.ds(end - n, n), :] = tail.astype(BF16)
    k_sq = jnp.sum(jnp.square(kv.astype(BF16).astype(F32)[:, 0:HEAD]), axis=1, keepdims=True)
    kn_scr[...] = jnp.maximum(kn_scr[...], jnp.max(k_sq))
    qi_t = qi.T
    w_qi = jnp.concatenate(
        [jnp.concatenate([qi_t[h * I_DIM:(h + 1) * I_DIM, :] for h in range(I_HEADS)], axis=1),
         jnp.zeros((LANES - I_DIM, I_HEADS * n), F32)], axis=0).astype(BF16)
    wi_t = tail.T[I_DIM:I_DIM + I_HEADS, :] * (I_HEADS ** -0.5 * I_DIM ** -0.5)

    q_pos = start + lax.broadcasted_iota(I32, (wide, n), 1)
    row_w = lax.broadcasted_iota(I32, (wide, n), 0)
    n_wide = (start + n + wide - 1) // wide

    def tile_start(i):
        return pl.multiple_of(end - (i + 1) * wide, n)

    def idx_tile(i, carry, check_causal=False, check_pad=False):
        ks = tile_start(i)
        s_all = jnp.dot(tail_scr[pl.ds(ks, wide), :], w_qi, preferred_element_type=F32)
        s = jnp.zeros((wide, n), F32)
        for h in range(I_HEADS):
            s = s + jnp.maximum(s_all[:, h * n:(h + 1) * n], 0.0) * wi_t[h:h + 1, :]
        s = s + 0.0
        bits = pltpu.bitcast(s, I32)
        key = jnp.where(bits < 0, bits ^ jnp.int32(0x7FFFFFFF), bits)
        idx = ks - wide + row_w
        if check_causal:
            key = jnp.where(idx <= q_pos, key, jnp.int32(INT_MIN))
        if check_pad:
            key = jnp.where(idx >= 0, key, jnp.int32(INT_MIN))
        key_scr[pl.ds(ks, wide), :] = key
        ukey = key ^ jnp.int32(INT_MIN)
        for blk in range(wide // BIT_ROWS):
            words = _bit_transpose([ukey[blk * BIT_ROWS + 8 * k:blk * BIT_ROWS + 8 * k + 8, :] for k in range(32)])
            row0 = pl.multiple_of((i * (wide // BIT_ROWS) + blk) * 8, 8)
            for b in range(32):
                plane_scr[b, pl.ds(row0, 8), :] = words[31 - b]
        return carry

    idx_tile(0, 0, check_causal=True, check_pad=True)
    lax.fori_loop(1, n_wide - 1, idx_tile, 0)

    @pl.when(n_wide > 1)
    def _():
        idx_tile(n_wide - 1, 0, check_pad=True)

    blk_id = lax.broadcasted_iota(I32, (plane_scr.shape[1], n), 0) // 8
    live = jnp.where(blk_id < n_wide * (wide // BIT_ROWS), jnp.int32(-1), jnp.int32(0))
    above = jnp.zeros((1, n), I32)
    thr = jnp.zeros((1, n), I32)
    def n_set(words):
        return jnp.sum(lax.population_count(words), axis=0, keepdims=True)

    for b in range(31, 0, -2):
        hi = live & plane_scr[b]
        lo = live ^ hi
        t3 = hi & plane_scr[b - 1]
        t2 = hi ^ t3
        t1 = lo & plane_scr[b - 1]
        t0 = lo ^ t1
        g3 = above + n_set(t3)
        g2 = g3 + n_set(t2)
        g1 = g2 + n_set(t1)
        d3, d2, d1 = g3 >= top_k, g2 >= top_k, g1 >= top_k
        digit = jnp.where(d3, 3, jnp.where(d2, 2, jnp.where(d1, 1, 0)))
        thr = thr | lax.shift_left(digit, jnp.int32(b - 1))
        above = jnp.where(d3, above, jnp.where(d2, g3, jnp.where(d1, g2, g1)))
        live = jnp.where(d3, t3, jnp.where(d2, t2, jnp.where(d1, t1, t0)))
    thr = thr ^ jnp.int32(INT_MIN)
    n_ge = above + jnp.sum(lax.population_count(live), axis=0, keepdims=True)
    tied = jnp.logical_and(n_ge > top_k, thr != INT_MIN)
    cut_scr[0:1, :] = jnp.full((1, n), seq, I32)

    @pl.when(jnp.max(tied.astype(I32)) > 0)
    def _():
        need = top_k - above
        row = lax.broadcasted_iota(I32, (plane_scr.shape[1], n), 0)
        per_tile = wide // BIT_ROWS
        first = (start + n) - (blk_id // per_tile + 1) * wide + (blk_id % per_tile) * BIT_ROWS + row % 8

        def below(m):
            kc = jnp.clip(lax.shift_right_arithmetic(m - first + 7, jnp.int32(3)), 0, 32)
            keep = jnp.where(kc > 0, lax.shift_left(jnp.int32(-1), 32 - jnp.maximum(kc, 1)), 0)
            return n_set(live & keep)

        m = jnp.zeros((1, n), I32)
        for bit in range(int(math.log2(seq)) - 1, -1, -1):
            cand = m | jnp.int32(1 << bit)
            m = jnp.where(below(cand) < need, cand, m)
        cut_scr[0:1, :] = jnp.where(tied, m, seq)

    cut = cut_scr[0:1, :]

    def build_mask(ks):
        kk = key_scr[pl.ds(ks, wide), :]
        sel = jnp.logical_or(kk > thr, jnp.logical_and(kk == thr, ks - wide + row_w <= cut))
        sel = jnp.logical_and(sel, kk != INT_MIN)
        m_t = jnp.where(sel, 0.0, NEG)
        msk = jnp.concatenate([m_t[c * n:(c + 1) * n, :].T for c in range(wide // n)], axis=1)
        mask_scr[:, pl.ds(ks, wide)] = msk
        return msk

    lane = lax.broadcasted_iota(I32, (n, LANES), 1)

    def split(x):
        hi = x.astype(BF16).astype(F32)
        return hi, x - hi

    def q_operand(shift, far):
        rows = []
        for h in range(N_HEADS):
            op = jnp.concatenate([q[:, h * HEAD:(h + 1) * HEAD] * (HEAD ** -0.5), jnp.zeros((n, HEAD), F32)], axis=1)
            if shift is not None:
                hi, lo = split(shift[h])
                op = op + jnp.where(lane == HEAD, hi, 0.0) + jnp.where(lane == HEAD + 1, lo, 0.0)
            if far:
                hi, lo = split(jnp.full((n, 1), far_ref[0, h], F32))
                op = op + jnp.where(lane == HEAD + 2, hi, 0.0) + jnp.where(lane == HEAD + 3, lo, 0.0)
            rows.append(op)
        return jnp.concatenate(rows, axis=0).astype(BF16)

    def logits(i, q_op, near, first_pass):
        ks = tile_start(i)
        s_all = jnp.dot(q_op, kt_scr[:, pl.ds(ks, wide)], preferred_element_type=F32)
        msk = build_mask(ks) if first_pass else mask_scr[:, pl.ds(ks, wide)]
        out = []
        for h in range(N_HEADS):
            s = s_all[h * n:(h + 1) * n, :] + msk
            out.append(s + bias_ref[h] if near else s)
        return ks, out

    def max_tile(i, q_op, near):
        _, ss = logits(i, q_op, near, True)
        for h in range(N_HEADS):
            m = mx_scr[h * n:(h + 1) * n, :]
            for c in range(wide // n):
                m = jnp.maximum(m, ss[h][:, c * n:(c + 1) * n])
            mx_scr[h * n:(h + 1) * n, :] = m

    def sum_tile(i, q_op, near, first_pass=False):
        ks, ss = logits(i, q_op, near, first_pass)
        ps = []
        for h in range(N_HEADS):
            ps.append(jnp.exp(ss[h]).astype(BF16))
        acc_scr[...] += jnp.dot(jnp.concatenate(ps, axis=0), kv_scr[pl.ds(ks, wide), :], preferred_element_type=F32)

    def run(tile_fn, shift, *extra):
        tile_fn(0, q_operand(shift, False), True, *extra)
        q_far = q_operand(shift, True)

        def body(i, carry):
            tile_fn(i, q_far, False, *extra)
            return carry
        lax.fori_loop(1, n_wide, body, 0)

    k_max = jnp.sqrt(kn_scr[0:1, 0:1]) * 1.02
    q_norm = [jnp.sqrt(jnp.sum(jnp.square((q[:, h * HEAD:(h + 1) * HEAD] * (HEAD ** -0.5)).astype(BF16).astype(F32)),
                               axis=1, keepdims=True)) for h in range(N_HEADS)]
    bound = [q_norm[h] * k_max + far_ref[1, h] for h in range(N_HEADS)]
    spread = [2.0 * q_norm[h] * k_max + (far_ref[1, h] - far_ref[2, h]) for h in range(N_HEADS)]
    worst = jnp.max(jnp.maximum(jnp.maximum(spread[0], spread[1]), jnp.maximum(spread[2], spread[3])))
    acc_scr[...] = jnp.zeros_like(acc_scr)

    @pl.when(worst <= EXP_RANGE)
    def _():
        run(sum_tile, [-bound[h] for h in range(N_HEADS)], True)

    @pl.when(jnp.logical_not(worst <= EXP_RANGE))
    def _():
        mx_scr[...] = jnp.full(mx_scr.shape, NEG, F32)
        run(max_tile, None)
        neg_max = [-jnp.max(mx_scr[h * n:(h + 1) * n, :], axis=1, keepdims=True) for h in range(N_HEADS)]
        run(sum_tile, neg_max)

    o_ref[0] = jnp.concatenate(
        [acc_scr[h * n:(h + 1) * n, HEAD:LANES] / acc_scr[h * n:(h + 1) * n, LANES:LANES + 1]
         for h in range(N_HEADS)], axis=1).astype(o_ref.dtype)


def _t5_bucket(dist):
    n = jnp.maximum(dist, 0)
    max_exact = REL_BUCKETS // 2
    log_ratio = jnp.log(jnp.maximum(n, 1).astype(F32) / max_exact) / math.log(REL_MAX_DIST / max_exact)
    large = jnp.minimum(max_exact + (log_ratio * (REL_BUCKETS - max_exact)).astype(I32), REL_BUCKETS - 1)
    return jnp.where(n < max_exact, n, large)


def _dsa(u_c, kv_norm_w, w_uk, w_uv, rel_bias):
    b, l, _ = u_c.shape
    top_k = min(TOPK_MAX, l // 4)
    assert DSA_WIDE - Q_BLOCK + 1 >= REL_MAX_DIST and l % DSA_WIDE == 0
    dist = (jnp.arange(Q_BLOCK, dtype=I32)[:, None] + (DSA_WIDE - Q_BLOCK)) - jnp.arange(DSA_WIDE, dtype=I32)[None, :]
    onehot = _t5_bucket(dist)[None] == jnp.arange(REL_BUCKETS, dtype=I32)[:, None, None]
    table = rel_bias.astype(F32)
    bias = jnp.stack([jnp.sum(jnp.where(onehot, table[:, h][:, None, None], 0.0), axis=0) for h in range(N_HEADS)])
    w_kv = jnp.concatenate([w_uk, w_uv], axis=1).astype(F32)
    kern = functools.partial(_dsa_kernel, top_k=top_k)
    lp = l + DSA_WIDE
    return pl.pallas_call(
        kern,
        grid=(b, l // Q_BLOCK),
        in_specs=[pl.BlockSpec((1, Q_BLOCK, C_COLS), lambda i, j: (i, j, 0)),
                  _full((1, LANES)), _full((LANES, LANES)), _full((N_HEADS, Q_BLOCK, DSA_WIDE)),
                  pl.BlockSpec(memory_space=pltpu.SMEM)],
        out_specs=pl.BlockSpec((1, Q_BLOCK, W_GROUP), lambda i, j: (i, j, 0)),
        out_shape=jax.ShapeDtypeStruct((b, l, W_GROUP), BF16),
        scratch_shapes=[pltpu.VMEM((LANES, lp), BF16), pltpu.VMEM((lp, 2 * LANES), BF16), pltpu.VMEM((lp, LANES), BF16),
                        pltpu.VMEM((lp, Q_BLOCK), I32), pltpu.VMEM((32, lp // BIT_ROWS * 8, Q_BLOCK), I32),
                        pltpu.VMEM((Q_BLOCK, lp), F32),
                        pltpu.VMEM((N_HEADS * Q_BLOCK, 2 * LANES), F32), pltpu.VMEM((N_HEADS * Q_BLOCK, LANES), F32),
                        pltpu.VMEM((8, Q_BLOCK), I32), pltpu.VMEM((8, LANES), F32)],
        compiler_params=_params("parallel", "arbitrary"),
        name="dsa",
    )(u_c, kv_norm_w.astype(F32)[None, :], w_kv, bias,
      jnp.stack([table[REL_BUCKETS - 1], jnp.max(table, axis=0), jnp.min(table, axis=0)]))


def _s5_kernel(u_ref, are_ref, aim_ref, ldt_ref, bre_ref, bim_ref, cre_ref, cim_ref, dsk_ref, gw_ref, gb_ref,
               o_ref, pre, pim, bdr, bdi, hre, him, cre_s, cim_s, tre_s, tim_s):
    first = jnp.logical_and(pl.program_id(0) == 0, pl.program_id(1) == 0)
    n = S_BLOCK
    pad = LANES

    @pl.when(first)
    def _():
        lam_re = jnp.minimum(are_ref[...], -1e-4)
        lam_im = aim_ref[...]
        dt = jnp.exp(ldt_ref[...])
        mag = jnp.exp(dt * lam_re)
        ab_re = mag * jnp.cos(dt * lam_im)
        ab_im = mag * jnp.sin(dt * lam_im)
        den = lam_re * lam_re + lam_im * lam_im
        f_re = ((ab_re - 1.0) * lam_re + ab_im * lam_im) / den
        f_im = (ab_im * lam_re - (ab_re - 1.0) * lam_im) / den
        b_re = bre_ref[...]
        b_im = bim_ref[...]
        bb_re = f_re * b_re - f_im * b_im
        bb_im = f_re * b_im + f_im * b_re
        rg = lax.broadcasted_iota(I32, (W_GROUP, S_LANES), 0) // S_GROUP_CH
        cg = lax.broadcasted_iota(I32, (W_GROUP, S_LANES), 1) // S_STATE
        same = rg == cg
        bdr[...] = jnp.where(same, jnp.concatenate([bb_re] * S_GROUPS, axis=0), 0.0)
        bdi[...] = jnp.where(same, jnp.concatenate([bb_im] * S_GROUPS, axis=0), 0.0)
        pre[0:1, :] = ab_re
        pim[0:1, :] = ab_im
        d = 1
        while d < S_WIN:
            sr = pre[d - 1:d, :]
            si = pim[d - 1:d, :]
            xr = pre[0:d, :]
            xi = pim[0:d, :]
            pre[d:2 * d, :] = xr * sr - xi * si
            pim[d:2 * d, :] = xr * si + xi * sr
            d *= 2
        hre[...] = jnp.zeros_like(hre)
        him[...] = jnp.zeros_like(him)

    @pl.when(pl.program_id(1) == 0)
    def _():
        cre_s[...] = jnp.zeros_like(cre_s)
        cim_s[...] = jnp.zeros_like(cim_s)
        tre_s[...] = jnp.zeros_like(tre_s)
        tim_s[...] = jnp.zeros_like(tim_s)

    u = u_ref[0]
    bu_re = _mm(u, bdr[...])
    bu_im = _mm(u, bdi[...])
    lo = pad - S_WIN
    hre[0, lo:pad, :] = tre_s[...]
    him[0, lo:pad, :] = tim_s[...]
    hre[0, pad:pad + n, :] = bu_re
    him[0, pad:pad + n, :] = bu_im
    tre_s[...] = bu_re[n - S_WIN:, :]
    tim_s[...] = bu_im[n - S_WIN:, :]

    def col_block(cb, c_):
        cs = pl.ds(pl.multiple_of(cb * LANES, LANES), LANES)
        src = 0
        d = 1
        while d < S_WIN:
            ar = pre[d - 1:d, cs]
            ai = pim[d - 1:d, cs]
            xr = hre[src, pl.ds(lo - d, n + S_WIN), cs]
            xi = him[src, pl.ds(lo - d, n + S_WIN), cs]
            hre[1 - src, lo:pad + n, cs] = hre[src, lo:pad + n, cs] + ar * xr - ai * xi
            him[1 - src, lo:pad + n, cs] = him[src, lo:pad + n, cs] + ar * xi + ai * xr
            src = 1 - src
            d *= 2
        ar = jnp.broadcast_to(pre[S_WIN - 1:S_WIN, cs], (S_WIN, LANES))
        ai = jnp.broadcast_to(pim[S_WIN - 1:S_WIN, cs], (S_WIN, LANES))
        pr = cre_s[:, cs]
        pi = cim_s[:, cs]
        for v in range(n // S_WIN):
            rows = slice(pad + v * S_WIN, pad + (v + 1) * S_WIN)
            nr = hre[src, rows, cs] + ar * pr - ai * pi
            ni = him[src, rows, cs] + ar * pi + ai * pr
            hre[1 - src, rows, cs] = nr
            him[1 - src, rows, cs] = ni
            pr, pi = nr, ni
        cre_s[:, cs] = pr
        cim_s[:, cs] = pi
        assert src == 1
        return c_

    lax.fori_loop(0, S_LANES // LANES, col_block, 0)

    h_re = hre[0, pad:pad + n, :]
    h_im = him[0, pad:pad + n, :]
    y = _mm(h_re, cre_ref[...]) - _mm(h_im, cim_ref[...]) + dsk_ref[...] * u
    y = jax.nn.gelu(y)
    gl = _mm(y, gw_ref[...]) + gb_ref[...]
    o_ref[0] = (gl[:, 0:W_GROUP] * jax.nn.sigmoid(gl[:, W_GROUP:])).astype(o_ref.dtype)


def _s5(u_s, a_re, a_im, b_re, b_im, c_re, c_im, d_skip, log_dt, glu_w, glu_b):
    b, l, _ = u_s.shape
    f = lambda t: t.astype(F32).reshape(1, -1)
    eye = jnp.eye(S_GROUPS, dtype=F32)
    bt = lambda t: jnp.transpose(t.astype(F32), (2, 0, 1)).reshape(S_GROUP_CH, S_LANES)
    cbd = lambda t: jnp.einsum('gpn,gh->gnhp', t.astype(F32), eye).reshape(S_LANES, W_GROUP)
    ldt = jnp.repeat(log_dt.astype(F32), S_STATE)[None, :]
    return pl.pallas_call(
        _s5_kernel,
        grid=(b, l // S_BLOCK),
        in_specs=[pl.BlockSpec((1, S_BLOCK, S_COLS), lambda i, j: (i, j, 0)),
                  _full((1, S_LANES)), _full((1, S_LANES)), _full((1, S_LANES)),
                  _full((S_GROUP_CH, S_LANES)), _full((S_GROUP_CH, S_LANES)),
                  _full((S_LANES, W_GROUP)), _full((S_LANES, W_GROUP)),
                  _full((1, W_GROUP)), _full((W_GROUP, 2 * W_GROUP)), _full((1, 2 * W_GROUP))],
        out_specs=pl.BlockSpec((1, S_BLOCK, W_GROUP), lambda i, j: (i, j, 0)),
        out_shape=jax.ShapeDtypeStruct((b, l, W_GROUP), BF16),
        scratch_shapes=[pltpu.VMEM((S_WIN, S_LANES), F32), pltpu.VMEM((S_WIN, S_LANES), F32),
                        pltpu.VMEM((W_GROUP, S_LANES), F32), pltpu.VMEM((W_GROUP, S_LANES), F32),
                        pltpu.VMEM((2, S_BLOCK + LANES, S_LANES), F32), pltpu.VMEM((2, S_BLOCK + LANES, S_LANES), F32),
                        pltpu.VMEM((S_WIN, S_LANES), F32), pltpu.VMEM((S_WIN, S_LANES), F32),
                        pltpu.VMEM((S_WIN, S_LANES), F32), pltpu.VMEM((S_WIN, S_LANES), F32)],
        compiler_params=_params("arbitrary", "arbitrary"),
        name="s5",
    )(u_s, f(a_re), f(a_im), ldt, bt(b_re), bt(b_im), cbd(c_re), cbd(c_im), f(d_skip),
      glu_w.astype(F32), f(glu_b))


def _out_proj_kernel(x_ref, ym_ref, yr_ref, yc_ref, ys_ref, w_ref, o_ref):
    acc = x_ref[...]
    for i, y_ref in enumerate((ym_ref, yr_ref, yc_ref, ys_ref)):
        acc = acc + jnp.dot(y_ref[...], w_ref[i * W_GROUP:(i + 1) * W_GROUP, :], preferred_element_type=F32)
    o_ref[...] = acc


def _out_proj(x2, ys, w_out, tm=512):
    t, d = x2.shape
    yspec = pl.BlockSpec((tm, W_GROUP), lambda i: (i, 0))
    return pl.pallas_call(
        _out_proj_kernel,
        grid=(t // tm,),
        in_specs=[pl.BlockSpec((tm, d), lambda i: (i, 0)), yspec, yspec, yspec, yspec, _full(w_out.shape)],
        out_specs=pl.BlockSpec((tm, d), lambda i: (i, 0)),
        out_shape=jax.ShapeDtypeStruct((t, d), F32),
        compiler_params=_params("parallel"),
        name="out_proj",
    )(x2, *ys, w_out)


def _moe_kernel(x_ref, nw_ref, wr_ref, br_ref, wg_ref, wu_ref, wd_ref, fw_ref, o_ref,
                t_scr, cw_scr, rk_scr, rkt_scr, cnt_scr, acc_scr, *, final_norm):
    g = pl.program_id(1)
    tm = x_ref.shape[0]
    cap = MOE_CAP
    lane = lax.broadcasted_iota(I32, (tm, LANES), 1)

    @pl.when(g == 0)
    def _():
        x = x_ref[...]
        t = x * lax.rsqrt(jnp.mean(x * x, axis=-1, keepdims=True) + NORM_EPS) * nw_ref[...]
        t_hi = t.astype(BF16)
        t_scr[...] = t_hi
        t_lo = (t - t_hi.astype(F32)).astype(BF16)
        w_r = wr_ref[...]
        w_hi = w_r.astype(BF16)
        w_lo = (w_r - w_hi.astype(F32)).astype(BF16)
        logits = (jnp.dot(t_hi, w_hi, preferred_element_type=F32) + jnp.dot(t_lo, w_hi, preferred_element_type=F32)
                  + jnp.dot(t_hi, w_lo, preferred_element_type=F32)) + br_ref[...]
        big = jnp.int32(LANES)
        is_g = lane < E_GROUPS
        gl = jnp.where(is_g, logits, -jnp.inf)
        gmax = jnp.max(gl, axis=1, keepdims=True)
        g_sel = jnp.min(jnp.where(jnp.logical_and(is_g, gl == gmax), lane, big), axis=1, keepdims=True)
        g_gate = 1.0 / jnp.sum(jnp.exp(gl - gmax), axis=1, keepdims=True)
        lo = E_GROUPS + g_sel * E_PER_GROUP
        in_g = jnp.logical_and(lane >= lo, lane < lo + E_PER_GROUP)
        el = jnp.where(in_g, logits, -jnp.inf)
        m1 = jnp.max(el, axis=1, keepdims=True)
        i1 = jnp.min(jnp.where(el == m1, lane, big), axis=1, keepdims=True)
        el2 = jnp.where(lane == i1, -jnp.inf, el)
        m2 = jnp.max(el2, axis=1, keepdims=True)
        i2 = jnp.min(jnp.where(el2 == m2, lane, big), axis=1, keepdims=True)
        e2 = jnp.exp(m2 - m1)
        w1 = 1.0 / (1.0 + e2)
        w2 = e2 / (1.0 + e2)
        cw = jnp.where(lane == i1, w1, jnp.where(lane == i2, w2, 0.0)) * g_gate
        cw_hi = cw.astype(BF16)
        cw_scr[...] = jnp.concatenate([cw_hi, (cw - cw_hi.astype(F32)).astype(BF16)], axis=1)
        member = lane == g_sel
        ones = jnp.where(member, 1.0, 0.0)
        row = lax.broadcasted_iota(I32, (LANES, LANES), 0)
        col = lax.broadcasted_iota(I32, (LANES, LANES), 1)
        before = jnp.where(row > col, 1.0, 0.0).astype(BF16)
        running = jnp.zeros((1, LANES), F32)
        ranks = []
        for j in range(tm // LANES):
            blk = ones[j * LANES:(j + 1) * LANES, :]
            ranks.append(jnp.dot(before, blk.astype(BF16), preferred_element_type=F32) + running)
            running = running + jnp.sum(blk, axis=0, keepdims=True)
        rk = jnp.where(member, jnp.concatenate(ranks, axis=0), -1.0)
        rk_scr[...] = rk
        rkt_scr[...] = rk.T[0:8, :]
        cnt_scr[0:1, :] = running
        acc_scr[...] = jnp.zeros_like(acc_scr)

    lane1 = lax.broadcasted_iota(I32, (1, LANES), 1)
    n_tok = jnp.sum(jnp.where(lane1 == g, cnt_scr[0:1, :], 0.0)).astype(I32)
    rk_row = rkt_scr[pl.ds(g, 1), :]
    rk_col = jnp.sum(jnp.where(lane == g, rk_scr[...], 0.0), axis=1, keepdims=True)
    slot_r = lax.broadcasted_iota(I32, (cap, tm), 0).astype(F32)
    slot_c = lax.broadcasted_iota(I32, (tm, cap), 1).astype(F32)
    lane_c = lax.broadcasted_iota(I32, (cap, LANES), 1)

    def one_pass(blk, carry):
        base = (blk * cap).astype(F32)
        gather = jnp.where(rk_row - base == slot_r, 1.0, 0.0).astype(BF16)
        scatter = jnp.where(rk_col - base == slot_c, 1.0, 0.0).astype(BF16)
        xg = jnp.dot(gather, t_scr[...], preferred_element_type=F32).astype(BF16)
        cwg = jnp.dot(gather, cw_scr[...], preferred_element_type=F32)
        cwg = cwg[:, 0:LANES] + cwg[:, LANES:]
        parts = []
        for e in range(E_PER_GROUP):
            hid = _silu(jnp.dot(xg, wg_ref[0, e], preferred_element_type=F32)) * jnp.dot(xg, wu_ref[0, e], preferred_element_type=F32)
            w_e = jnp.sum(jnp.where(lane_c == E_GROUPS + g * E_PER_GROUP + e, cwg, 0.0), axis=1, keepdims=True)
            parts.append((hid * w_e).astype(BF16))
        y = jnp.dot(jnp.concatenate(parts, axis=1), wd_ref[0], preferred_element_type=F32)
        acc_scr[...] += jnp.dot(scatter, y.astype(BF16), preferred_element_type=F32)
        return carry

    lax.fori_loop(0, (n_tok + cap - 1) // cap, one_pass, 0)

    @pl.when(g == pl.num_programs(1) - 1)
    def _():
        y = x_ref[...] + acc_scr[...]
        if final_norm:
            y = y * lax.rsqrt(jnp.mean(y * y, axis=-1, keepdims=True) + NORM_EPS) * fw_ref[...]
        o_ref[...] = y


def _moe(x2, norm_w, wr_g, br_g, wr_e, br_e, w_gate, w_up, w_down, final_w, final_norm, tm=1024):
    t, d = x2.shape
    wr = jnp.zeros((d, LANES), F32).at[:, 0:E_GROUPS].set(wr_g.astype(F32))
    wr = wr.at[:, E_GROUPS:E_GROUPS + N_EXPERTS].set(wr_e.astype(F32))
    br = jnp.zeros((1, LANES), F32).at[0, 0:E_GROUPS].set(br_g.astype(F32))
    br = br.at[0, E_GROUPS:E_GROUPS + N_EXPERTS].set(br_e.astype(F32))
    wide = E_PER_GROUP * D_EXPERT
    wg = w_gate.astype(BF16)
    wu = w_up.astype(BF16)
    wd = w_down.reshape(E_GROUPS, wide, d).astype(BF16)
    kern = functools.partial(_moe_kernel, final_norm=final_norm)
    return pl.pallas_call(
        kern,
        grid=(t // tm, E_GROUPS),
        in_specs=[pl.BlockSpec((tm, d), lambda i, g: (i, 0)), _full((1, d)), _full((d, LANES)), _full((1, LANES)),
                  pl.BlockSpec((1, E_PER_GROUP, d, D_EXPERT), lambda i, g: (g, 0, 0, 0)),
                  pl.BlockSpec((1, E_PER_GROUP, d, D_EXPERT), lambda i, g: (g, 0, 0, 0)),
                  pl.BlockSpec((1, wide, d), lambda i, g: (g, 0, 0)), _full((1, d))],
        out_specs=pl.BlockSpec((tm, d), lambda i, g: (i, 0)),
        out_shape=jax.ShapeDtypeStruct((t, d), F32),
        scratch_shapes=[pltpu.VMEM((tm, d), BF16), pltpu.VMEM((tm, 2 * LANES), BF16), pltpu.VMEM((tm, LANES), F32),
                        pltpu.VMEM((8, tm), F32), pltpu.VMEM((8, LANES), F32), pltpu.VMEM((tm, d), F32)],
        compiler_params=_params("parallel", "arbitrary"),
        name="moe",
    )(x2, norm_w.astype(F32)[None, :], wr, br, wg, wu, wd, final_w.astype(F32)[None, :])


def _pad_in_proj(w):
    d = w.shape[0]
    m_in, r_in = 772, 896
    c_used = 680
    z = lambda k: jnp.zeros((d, k), w.dtype)
    wm = w[:, 0:m_in]
    wr = w[:, m_in:m_in + r_in]
    wc = w[:, m_in + r_in:m_in + r_in + c_used]
    ws = w[:, m_in + r_in + c_used:]
    return jnp.concatenate([wm, z(M_COLS - m_in), wr, wc, z(C_COLS - c_used), ws], axis=1).astype(BF16)


def kernel(x, norm1_w, w_in, m_conv_w, m_conv_b, m_dt_bias, m_a_log, m_d, m_norm_w, r_mu, r_w0, r_w2, r_a0, r_a2, r_g2, r_k_k, r_k_a, r_r_k, r_ln_w, r_ln_b, c_kv_norm_w, c_w_uk, c_w_uv, rel_bias, s_a_re, s_a_im, s_b_re, s_b_im, s_c_re, s_c_im, s_d, s_log_dt, s_glu_w, s_glu_b, w_out, norm2_w, moe_wr_group, moe_br_group, moe_wr_exp, moe_br_exp, moe_w_gate, moe_w_up, moe_w_down, final_norm_w):
    bsz, seq, d = x.shape
    depth = w_in.shape[0]
    x2 = x.astype(F32).reshape(bsz * seq, d)
    for i in range(depth):
        u_m, u_r, u_c, u_s = _in_proj(x2, norm1_w[i].astype(F32)[None, :], _pad_in_proj(w_in[i]))
        sh = lambda t: t.reshape(bsz, seq, t.shape[-1])
        y_m = _mamba(sh(u_m), m_conv_w[i], m_conv_b[i], m_dt_bias[i], m_a_log[i], m_d[i], m_norm_w[i])
        y_r = _rwkv(sh(u_r), r_mu[i], r_w0[i], r_w2[i], r_a0[i], r_a2[i], r_g2[i],
                    r_k_k[i], r_k_a[i], r_r_k[i], r_ln_w[i], r_ln_b[i])
        y_c = _dsa(sh(u_c), c_kv_norm_w[i], c_w_uk[i], c_w_uv[i], rel_bias)
        y_s = _s5(sh(u_s), s_a_re[i], s_a_im[i], s_b_re[i], s_b_im[i], s_c_re[i], s_c_im[i],
                  s_d[i], s_log_dt[i], s_glu_w[i], s_glu_b[i])
        ys = [t.reshape(bsz * seq, W_GROUP) for t in (y_m, y_r, y_c, y_s)]
        x2 = _out_proj(x2, ys, w_out[i].astype(BF16))
        x2 = _moe(x2, norm2_w[i], moe_wr_group[i], moe_br_group[i], moe_wr_exp[i], moe_br_exp[i],
                  moe_w_gate[i], moe_w_up[i], moe_w_down[i], final_norm_w, final_norm=(i == depth - 1))
    return x2.reshape(bsz, seq, d).astype(x.dtype)
```

```python
import functools
import math

import jax
import jax.numpy as jnp
from jax import lax
from jax.experimental import pallas as pl
from jax.experimental.pallas import tpu as pltpu

F32 = jnp.float32
BF16 = jnp.bfloat16
I32 = jnp.int32
HIGHEST = lax.Precision.HIGHEST

LANES = 128
VMEM_LIMIT = 56 * 1024 * 1024

NORM_EPS = 1e-6
W_GROUP = 256
HEAD = 64
N_HEADS = W_GROUP // HEAD
M_CONV_CH = 512
M_CONV = 4
M_COLS = 896
R_COLS = 896
C_COLS = 768
S_COLS = 256
SSD_CHUNK = 128
SSD_BLOCK = 128
R_CHUNK = 64
R_BLOCK = 512
R_GN_EPS = 64e-5
I_HEADS = 8
I_DIM = 32
Q_BLOCK = 128
DSA_WIDE = 512
BIT_ROWS = 256
EXP_RANGE = 80.0
DSA_AUX = 4
TOPK_MAX = 256
REL_BUCKETS = 32
REL_MAX_DIST = 128
S_GROUPS = 16
S_GROUP_CH = 16
S_STATE = 64
S_LANES = S_GROUPS * S_STATE
S_BLOCK = 256
S_WIN = 8
E_GROUPS = 4
E_PER_GROUP = 8
N_EXPERTS = E_GROUPS * E_PER_GROUP
D_EXPERT = 256
MOE_CAP = 288
INT_MIN = -2 ** 31
NEG = -1e30


def _mm(a, b, precise=False):
    if precise:
        return jnp.dot(a.astype(F32), b.astype(F32), precision=HIGHEST, preferred_element_type=F32)
    return jnp.dot(a.astype(BF16), b.astype(BF16), preferred_element_type=F32)


def _mm_nt(a, b, precise=False):
    dn = (((1,), (1,)), ((), ()))
    if precise:
        return lax.dot_general(a.astype(F32), b.astype(F32), dn, precision=HIGHEST, preferred_element_type=F32)
    return lax.dot_general(a.astype(BF16), b.astype(BF16), dn, preferred_element_type=F32)


def _mm_tn(a, b, precise=False):
    dn = (((0,), (0,)), ((), ()))
    if precise:
        return lax.dot_general(a.astype(F32), b.astype(F32), dn, precision=HIGHEST, preferred_element_type=F32)
    return lax.dot_general(a.astype(BF16), b.astype(BF16), dn, preferred_element_type=F32)


def _mm_split(a, b, split_rhs=False):
    x = b if split_rhs else a
    hi = x.astype(BF16)
    lo = (x - hi.astype(F32)).astype(BF16)
    if split_rhs:
        a = a.astype(BF16)
        return jnp.dot(a, hi, preferred_element_type=F32) + jnp.dot(a, lo, preferred_element_type=F32)
    b = b.astype(BF16)
    return jnp.dot(hi, b, preferred_element_type=F32) + jnp.dot(lo, b, preferred_element_type=F32)


def _i32(v):
    return jnp.int32(v - (1 << 32) if v >= (1 << 31) else v)


def _bit_transpose(words):
    a = list(words)
    j, mask = 16, 0x0000FFFF
    while j:
        k = 0
        while k < 32:
            t = (a[k] ^ lax.shift_right_logical(a[k + j], jnp.int32(j))) & _i32(mask)
            a[k] = a[k] ^ t
            a[k + j] = a[k + j] ^ lax.shift_left(t, jnp.int32(j))
            k = (k + j + 1) & ~j
        j >>= 1
        mask = (mask ^ (mask << j)) & 0xFFFFFFFF
    return a


def _silu(x):
    return x * jax.nn.sigmoid(x)


def _softplus(x):
    return jnp.maximum(x, 0.0) + jnp.log(1.0 + jnp.exp(-jnp.abs(x)))


def _params(*sem):
    return pltpu.CompilerParams(dimension_semantics=sem, vmem_limit_bytes=VMEM_LIMIT)


def _full(shape):
    nd = len(shape)
    return pl.BlockSpec(shape, lambda *_: (0,) * nd)


def _in_proj_kernel(x_ref, nw_ref, w_ref, om_ref, or_ref, oc_ref, os_ref):
    x = x_ref[...]
    h = x * lax.rsqrt(jnp.mean(x * x, axis=-1, keepdims=True) + NORM_EPS) * nw_ref[...]
    h = h.astype(BF16)
    off = 0
    for o_ref, width in ((om_ref, M_COLS), (or_ref, R_COLS), (oc_ref, C_COLS), (os_ref, S_COLS)):
        o_ref[...] = jnp.dot(h, w_ref[:, off:off + width], preferred_element_type=F32)
        off += width


def _in_proj(x2, norm_w, w_pad, tm=512):
    t, d = x2.shape
    n = w_pad.shape[1]
    return pl.pallas_call(
        _in_proj_kernel,
        grid=(t // tm,),
        in_specs=[pl.BlockSpec((tm, d), lambda i: (i, 0)), _full((1, d)), _full((d, n))],
        out_specs=[pl.BlockSpec((tm, c), lambda i: (i, 0)) for c in (M_COLS, R_COLS, C_COLS, S_COLS)],
        out_shape=[jax.ShapeDtypeStruct((t, c), F32) for c in (M_COLS, R_COLS, C_COLS, S_COLS)],
        compiler_params=_params("parallel"),
        name="in_proj",
    )(x2, norm_w, w_pad)


def _mamba_kernel(u_ref, cw_ref, cb_ref, dtb_ref, alog_ref, dsk_ref, nw_ref, o_ref, xbuf, state):
    c = pl.program_id(1)
    n = SSD_BLOCK

    @pl.when(c == 0)
    def _():
        xbuf[0:8, :] = jnp.zeros((8, M_CONV_CH), F32)
        state[...] = jnp.zeros_like(state)

    u = u_ref[0]
    z = u[:, 0:W_GROUP]
    dtr = u[:, W_GROUP + M_CONV_CH:M_COLS]
    xbuf[8:8 + n, :] = u[:, W_GROUP:W_GROUP + M_CONV_CH]
    acc = jnp.broadcast_to(cb_ref[...], (n, M_CONV_CH))
    for j in range(M_CONV):
        acc = acc + cw_ref[j:j + 1, :] * xbuf[pl.ds(8 - (M_CONV - 1) + j, n), :]
    xbuf[0:8, :] = xbuf[n:n + 8, :]
    xc = _silu(acc)
    xs = xc[:, 0:W_GROUP]
    bm = xc[:, W_GROUP:W_GROUP + 2 * HEAD]
    cm = xc[:, W_GROUP + 2 * HEAD:]

    dt = _softplus(dtr + dtb_ref[...])
    a = dt * (-jnp.exp(alog_ref[...]))
    cs = SSD_CHUNK
    row = lax.broadcasted_iota(I32, (n, n), 0)
    col = lax.broadcasted_iota(I32, (n, n), 1)
    same_chunk = jnp.logical_and(row >= col, (row // cs) == (col // cs))
    a_cum = _mm(same_chunk.astype(F32), a, precise=True)
    a_cum_t = a_cum.T
    causal = same_chunk[0:cs, 0:cs]

    ys = []
    for h in range(N_HEADS):
        g = h // 2
        st = state[h]
        ys_h = []
        for ci in range(n // cs):
            rows = slice(ci * cs, (ci + 1) * cs)
            b_h = bm[rows, g * HEAD:(g + 1) * HEAD]
            c_h = cm[rows, g * HEAD:(g + 1) * HEAD]
            x_h = xs[rows, h * HEAD:(h + 1) * HEAD]
            xdt = x_h * dt[rows, h:h + 1]
            ac = a_cum[rows, h:h + 1]
            dec = jnp.exp(jnp.where(causal, ac - a_cum_t[h:h + 1, rows], -jnp.inf))
            y = _mm(_mm_nt(c_h, b_h) * dec, xdt)
            y = y + _mm_nt(c_h, st) * jnp.exp(ac)
            al = a_cum[ci * cs + cs - 1:ci * cs + cs, h:h + 1]
            st = st * jnp.exp(al) + _mm_tn(xdt, b_h * jnp.exp(al - ac))
            ys_h.append(y + x_h * dsk_ref[:, h * HEAD:(h + 1) * HEAD])
        state[h] = st
        ys.append(jnp.concatenate(ys_h, axis=0))
    y = jnp.concatenate(ys, axis=1) * _silu(z)
    y = y * lax.rsqrt(jnp.mean(y * y, axis=-1, keepdims=True) + NORM_EPS) * nw_ref[...]
    o_ref[0] = y.astype(o_ref.dtype)


def _mamba(u_m, conv_w, conv_b, dt_bias, a_log, d_skip, norm_w):
    b, l, _ = u_m.shape
    pad = lambda v: jnp.pad(v.astype(F32), (0, LANES - v.shape[0]))[None, :]
    return pl.pallas_call(
        _mamba_kernel,
        grid=(b, l // SSD_BLOCK),
        in_specs=[pl.BlockSpec((1, SSD_BLOCK, M_COLS), lambda i, j: (i, j, 0)),
                  _full((M_CONV, M_CONV_CH)), _full((1, M_CONV_CH)), _full((1, LANES)), _full((1, LANES)),
                  _full((1, W_GROUP)), _full((1, W_GROUP))],
        out_specs=pl.BlockSpec((1, SSD_BLOCK, W_GROUP), lambda i, j: (i, j, 0)),
        out_shape=jax.ShapeDtypeStruct((b, l, W_GROUP), BF16),
        scratch_shapes=[pltpu.VMEM((SSD_BLOCK + 8, M_CONV_CH), F32), pltpu.VMEM((N_HEADS, HEAD, HEAD), F32)],
        compiler_params=_params("parallel", "arbitrary"),
        name="mamba_ssd",
    )(u_m, conv_w.astype(F32), conv_b.astype(F32)[None, :], pad(dt_bias), pad(a_log),
      jnp.repeat(d_skip.astype(F32), HEAD)[None, :], norm_w.astype(F32)[None, :])


def _rwkv_kernel(u_ref, mu_ref, wl_ref, w0_ref, a0_ref, kk_ref, ka_ref, rk_ref, lnw_ref, lnb_ref, seg_ref,
                 o_ref, ubuf, state):
    c = pl.program_id(1)
    n = R_BLOCK
    cs = R_CHUNK

    @pl.when(c == 0)
    def _():
        ubuf[0:8, :] = jnp.zeros((8, R_COLS), F32)
        state[...] = jnp.zeros_like(state)

    u = u_ref[0]
    ubuf[8:8 + n, :] = u
    u = u + mu_ref[...] * (ubuf[pl.ds(7, n), :] - u)
    ubuf[0:8, :] = ubuf[n:n + 8, :]
    r = u[:, 0:W_GROUP]
    k = u[:, W_GROUP:2 * W_GROUP]
    v = u[:, 2 * W_GROUP:3 * W_GROUP]
    lo = u[:, 3 * W_GROUP:]
    lane = lax.broadcasted_iota(I32, lo.shape, 1)
    lo = jnp.where(lane < 32, jnp.tanh(lo), jnp.where(lane < 64, lo, jax.nn.sigmoid(lo)))
    proj = _mm(lo, wl_ref[...])
    w_log = -_softplus(-(w0_ref[...] + proj[:, 0:W_GROUP])) - 0.5
    ld = -jnp.exp(w_log)
    a_lr = jax.nn.sigmoid(a0_ref[...] + proj[:, W_GROUP:2 * W_GROUP])
    g = proj[:, 2 * W_GROUP:]
    seg = seg_ref[...]
    kk = k * kk_ref[...]
    kk = kk * lax.rsqrt(jnp.maximum(_mm_split(kk * kk, seg), 1e-24))
    k = k * (1.0 + (a_lr - 1.0) * ka_ref[...])
    a_v = -kk
    b_v = kk * a_lr

    row = lax.broadcasted_iota(I32, (n, n), 0)
    col = lax.broadcasted_iota(I32, (n, n), 1)
    tri2 = ((row >= col) & ((row // cs) == (col // cs))).astype(F32)
    lw = _mm_split(tri2, ld, split_rhs=True)
    w_inc = jnp.exp(lw)
    w_inv = jnp.exp(-lw)
    rt = r * w_inc
    at = a_v * jnp.exp(lw - ld)
    bt = b_v * w_inv
    kt = k * w_inv

    wg = W_GROUP
    r2 = lax.broadcasted_iota(I32, (wg, wg), 0)
    c2 = lax.broadcasted_iota(I32, (wg, wg), 1)
    lower = (r2 % cs) >= (c2 % cs)
    strict = (r2 % cs) > (c2 % cs)
    eye = (r2 == c2).astype(F32)
    lane_head = lax.broadcasted_iota(I32, (cs, wg), 1) // HEAD

    def blocks(xc, dtype=BF16):
        return jnp.concatenate([jnp.where(lane_head == h, xc, 0.0) for h in range(N_HEADS)], axis=0).astype(dtype)

    s_cur = state[...]
    outs = []
    for ci in range(n // cs):
        rows = slice(ci * cs, (ci + 1) * cs)
        lw_end = lw[ci * cs + cs - 1:ci * cs + cs, :]
        ratio = jnp.exp(lw_end - lw[rows, :])
        at_b, rt_b, bt_b, kt_b = blocks(at[rows, :]), blocks(rt[rows, :], F32), blocks(bt[rows, :]), blocks(kt[rows, :])
        v_b = blocks(v[rows, :])
        bh = blocks(b_v[rows, :] * ratio)
        kh = blocks(k[rows, :] * ratio)
        ar_b = jnp.concatenate([at_b, rt_b.astype(BF16)], axis=0)
        g_b = _mm_nt(ar_b, bt_b)
        g_k = _mm_nt(ar_b, kt_b)
        a_ab = jnp.where(strict, g_b[0:wg], 0.0)
        a_ak = jnp.where(strict, g_k[0:wg], 0.0)
        a_rb = jnp.where(lower, g_b[wg:], 0.0)
        a_rk = jnp.where(lower, g_k[wg:], 0.0)
        tinv = eye + a_ab
        pw = a_ab
        for _ in range(int(math.log2(cs)) - 1):
            pw = _mm(pw, pw)
            tinv = tinv + _mm(pw, tinv)
        ta = _mm(tinv, at_b)
        akv = _mm(jnp.concatenate([a_ak, a_rk], axis=0), v_b)
        pv = _mm(tinv, akv[0:wg])
        q_eff = rt_b + _mm(a_rb, ta)
        o_v = _mm(a_rb, pv) + akv[wg:]
        tb = _mm_tn(jnp.concatenate([ta, pv], axis=1), bh)
        s_v = tb[wg:] + _mm_tn(v_b, kh)
        o_b = _mm_nt(q_eff, s_cur) + o_v
        outs.append(o_b[0:cs] + o_b[cs:2 * cs] + o_b[2 * cs:3 * cs] + o_b[3 * cs:])
        s_cur = s_cur * jnp.exp(lw_end) + _mm_split(s_cur, tb[0:wg]) + s_v
    state[...] = s_cur
    o = jnp.concatenate(outs, axis=0)

    inv = 1.0 / HEAD
    mean = _mm_split(o, seg) * inv
    d = o - mean
    var = _mm_split(d * d, seg) * inv
    o = d * lax.rsqrt(var + R_GN_EPS) * lnw_ref[...] + lnb_ref[...]
    o = o + _mm_split(r * k * rk_ref[...], seg) * v
    o_ref[0] = (o * g).astype(o_ref.dtype)


def _rwkv(u_r, mu, w0, w2, a0, a2, g2, k_k, k_a, r_k, ln_w, ln_b):
    b, l, _ = u_r.shape
    f = lambda t: t.astype(F32).reshape(1, -1)
    w_lora = jnp.zeros((LANES, 3 * W_GROUP), F32)
    w_lora = w_lora.at[0:32, 0:W_GROUP].set(w2.astype(F32))
    w_lora = w_lora.at[32:64, W_GROUP:2 * W_GROUP].set(a2.astype(F32))
    w_lora = w_lora.at[64:128, 2 * W_GROUP:].set(g2.astype(F32))
    hid = jnp.arange(W_GROUP) // HEAD
    seg = (hid[:, None] == hid[None, :]).astype(F32)
    vec = _full((1, W_GROUP))
    return pl.pallas_call(
        _rwkv_kernel,
        grid=(b, l // R_BLOCK),
        in_specs=[pl.BlockSpec((1, R_BLOCK, R_COLS), lambda i, j: (i, j, 0)),
                  _full((1, R_COLS)), _full((LANES, 3 * W_GROUP)),
                  vec, vec, vec, vec, vec, vec, vec, _full((W_GROUP, W_GROUP))],
        out_specs=pl.BlockSpec((1, R_BLOCK, W_GROUP), lambda i, j: (i, j, 0)),
        out_shape=jax.ShapeDtypeStruct((b, l, W_GROUP), BF16),
        scratch_shapes=[pltpu.VMEM((R_BLOCK + 8, R_COLS), F32), pltpu.VMEM((W_GROUP, W_GROUP), F32)],
        compiler_params=_params("parallel", "arbitrary"),
        name="rwkv7",
    )(u_r, f(mu), w_lora, f(w0), f(a0), f(k_k), f(k_a), f(r_k), f(ln_w), f(ln_b), seg)


def _dsa_kernel(u_ref, kvw_ref, wkv_ref, bias_ref, far_ref, o_ref,
                kt_scr, kv_scr, tail_scr, key_scr, plane_scr, mask_scr, acc_scr, mx_scr, cut_scr, kn_scr,
                *, top_k):
    qb = pl.program_id(1)
    n = Q_BLOCK
    wide = DSA_WIDE
    seq = key_scr.shape[0] - wide

    @pl.when(qb == 0)
    def _():
        kt_scr[...] = jnp.zeros_like(kt_scr)
        kv_scr[...] = jnp.zeros_like(kv_scr)
        tail_scr[...] = jnp.zeros_like(tail_scr)
        key_scr[...] = jnp.full(key_scr.shape, INT_MIN, I32)
        plane_scr[...] = jnp.zeros_like(plane_scr)
        mask_scr[...] = jnp.full(mask_scr.shape, NEG, F32)
        kn_scr[...] = jnp.zeros_like(kn_scr)

    u = u_ref[0]
    q = u[:, 0:W_GROUP]
    ckv = u[:, W_GROUP:W_GROUP + LANES]
    qi = u[:, W_GROUP + LANES:2 * W_GROUP + LANES]
    tail = u[:, 2 * W_GROUP + LANES:]
    ckv = ckv * lax.rsqrt(jnp.mean(ckv * ckv, axis=-1, keepdims=True) + NORM_EPS) * kvw_ref[...]
    kv = _mm(ckv, wkv_ref[...])
    start = pl.multiple_of(qb * n, n)
    end = pl.multiple_of(start + wide + n, n)
    sub = lax.broadcasted_iota(I32, (HEAD, n), 0)
    kt_scr[:, pl.ds(end - n, n)] = jnp.concatenate(
        [kv.T[0:HEAD, :], jnp.where(sub < DSA_AUX, 1.0, 0.0)], axis=0).astype(BF16)
    kv_scr[pl.ds(end - n, n), :] = jnp.concatenate([kv, jnp.ones((n, LANES), F32)], axis=1).astype(BF16)
    tail_scr[pl.ds(end - n, n), :] = tail.astype(BF16)
    k_sq = jnp.sum(jnp.square(kv.astype(BF16).astype(F32)[:, 0:HEAD]), axis=1, keepdims=True)
    kn_scr[...] = jnp.maximum(kn_scr[...], jnp.max(k_sq))
    qi_t = qi.T
    w_qi = jnp.concatenate(
        [jnp.concatenate([qi_t[h * I_DIM:(h + 1) * I_DIM, :] for h in range(I_HEADS)], axis=1),
         jnp.zeros((LANES - I_DIM, I_HEADS * n), F32)], axis=0).astype(BF16)
    wi_t = tail.T[I_DIM:I_DIM + I_HEADS, :] * (I_HEADS ** -0.5 * I_DIM ** -0.5)

    q_pos = start + lax.broadcasted_iota(I32, (wide, n), 1)
    row_w = lax.broadcasted_iota(I32, (wide, n), 0)
    n_wide = (start + n + wide - 1) // wide

    def tile_start(i):
        return pl.multiple_of(end - (i + 1) * wide, n)

    def idx_tile(i, carry, check_causal=False, check_pad=False):
        ks = tile_start(i)
        s_all = jnp.dot(tail_scr[pl.ds(ks, wide), :], w_qi, preferred_element_type=F32)
        s = jnp.zeros((wide, n), F32)
        for h in range(I_HEADS):
            s = s + jnp.maximum(s_all[:, h * n:(h + 1) * n], 0.0) * wi_t[h:h + 1, :]
        s = s + 0.0
        bits = pltpu.bitcast(s, I32)
        key = jnp.where(bits < 0, bits ^ jnp.int32(0x7FFFFFFF), bits)
        idx = ks - wide + row_w
        if check_causal:
            key = jnp.where(idx <= q_pos, key, jnp.int32(INT_MIN))
        if check_pad:
            key = jnp.where(idx >= 0, key, jnp.int32(INT_MIN))
        key_scr[pl.ds(ks, wide), :] = key
        ukey = key ^ jnp.int32(INT_MIN)
        for blk in range(wide // BIT_ROWS):
            words = _bit_transpose([ukey[blk * BIT_ROWS + 8 * k:blk * BIT_ROWS + 8 * k + 8, :] for k in range(32)])
            row0 = pl.multiple_of((i * (wide // BIT_ROWS) + blk) * 8, 8)
            for b in range(32):
                plane_scr[b, pl.ds(row0, 8), :] = words[31 - b]
        return carry

    idx_tile(0, 0, check_causal=True, check_pad=True)
    lax.fori_loop(1, n_wide - 1, idx_tile, 0)

    @pl.when(n_wide > 1)
    def _():
        idx_tile(n_wide - 1, 0, check_pad=True)

    blk_id = lax.broadcasted_iota(I32, (plane_scr.shape[1], n), 0) // 8
    live = jnp.where(blk_id < n_wide * (wide // BIT_ROWS), jnp.int32(-1), jnp.int32(0))
    above = jnp.zeros((1, n), I32)
    thr = jnp.zeros((1, n), I32)
    def n_set(words):
        return jnp.sum(lax.population_count(words), axis=0, keepdims=True)

    for b in range(31, 0, -2):
        hi = live & plane_scr[b]
        lo = live ^ hi
        t3 = hi & plane_scr[b - 1]
        t2 = hi ^ t3
        t1 = lo & plane_scr[b - 1]
        t0 = lo ^ t1
        g3 = above + n_set(t3)
        g2 = g3 + n_set(t2)
        g1 = g2 + n_set(t1)
        d3, d2, d1 = g3 >= top_k, g2 >= top_k, g1 >= top_k
        digit = jnp.where(d3, 3, jnp.where(d2, 2, jnp.where(d1, 1, 0)))
        thr = thr | lax.shift_left(digit, jnp.int32(b - 1))
        above = jnp.where(d3, above, jnp.where(d2, g3, jnp.where(d1, g2, g1)))
        live = jnp.where(d3, t3, jnp.where(d2, t2, jnp.where(d1, t1, t0)))
    thr = thr ^ jnp.int32(INT_MIN)
    n_ge = above + jnp.sum(lax.population_count(live), axis=0, keepdims=True)
    tied = jnp.logical_and(n_ge > top_k, thr != INT_MIN)
    cut_scr[0:1, :] = jnp.full((1, n), seq, I32)

    @pl.when(jnp.max(tied.astype(I32)) > 0)
    def _():
        need = top_k - above
        row = lax.broadcasted_iota(I32, (plane_scr.shape[1], n), 0)
        per_tile = wide // BIT_ROWS
        first = (start + n) - (blk_id // per_tile + 1) * wide + (blk_id % per_tile) * BIT_ROWS + row % 8

        def below(m):
            kc = jnp.clip(lax.shift_right_arithmetic(m - first + 7, jnp.int32(3)), 0, 32)
            keep = jnp.where(kc > 0, lax.shift_left(jnp.int32(-1), 32 - jnp.maximum(kc, 1)), 0)
            return n_set(live & keep)

        m = jnp.zeros((1, n), I32)
        for bit in range(int(math.log2(seq)) - 1, -1, -1):
            cand = m | jnp.int32(1 << bit)
            m = jnp.where(below(cand) < need, cand, m)
        cut_scr[0:1, :] = jnp.where(tied, m, seq)

    cut = cut_scr[0:1, :]

    def build_mask(ks):
        kk = key_scr[pl.ds(ks, wide), :]
        sel = jnp.logical_or(kk > thr, jnp.logical_and(kk == thr, ks - wide + row_w <= cut))
        sel = jnp.logical_and(sel, kk != INT_MIN)
        m_t = jnp.where(sel, 0.0, NEG)
        msk = jnp.concatenate([m_t[c * n:(c + 1) * n, :].T for c in range(wide // n)], axis=1)
        mask_scr[:, pl.ds(ks, wide)] = msk
        return msk

    lane = lax.broadcasted_iota(I32, (n, LANES), 1)

    def split(x):
        hi = x.astype(BF16).astype(F32)
        return hi, x - hi

    def q_operand(shift, far):
        rows = []
        for h in range(N_HEADS):
            op = jnp.concatenate([q[:, h * HEAD:(h + 1) * HEAD] * (HEAD ** -0.5), jnp.zeros((n, HEAD), F32)], axis=1)
            if shift is not None:
                hi, lo = split(shift[h])
                op = op + jnp.where(lane == HEAD, hi, 0.0) + jnp.where(lane == HEAD + 1, lo, 0.0)
            if far:
                hi, lo = split(jnp.full((n, 1), far_ref[0, h], F32))
                op = op + jnp.where(lane == HEAD + 2, hi, 0.0) + jnp.where(lane == HEAD + 3, lo, 0.0)
            rows.append(op)
        return jnp.concatenate(rows, axis=0).astype(BF16)

    def logits(i, q_op, near, first_pass):
        ks = tile_start(i)
        s_all = jnp.dot(q_op, kt_scr[:, pl.ds(ks, wide)], preferred_element_type=F32)
        msk = build_mask(ks) if first_pass else mask_scr[:, pl.ds(ks, wide)]
        out = []
        for h in range(N_HEADS):
            s = s_all[h * n:(h + 1) * n, :] + msk
            out.append(s + bias_ref[h] if near else s)
        return ks, out

    def max_tile(i, q_op, near):
        _, ss = logits(i, q_op, near, True)
        for h in range(N_HEADS):
            m = mx_scr[h * n:(h + 1) * n, :]
            for c in range(wide // n):
                m = jnp.maximum(m, ss[h][:, c * n:(c + 1) * n])
            mx_scr[h * n:(h + 1) * n, :] = m

    def sum_tile(i, q_op, near, first_pass=False):
        ks, ss = logits(i, q_op, near, first_pass)
        ps = []
        for h in range(N_HEADS):
            ps.append(jnp.exp(ss[h]).astype(BF16))
        acc_scr[...] += jnp.dot(jnp.concatenate(ps, axis=0), kv_scr[pl.ds(ks, wide), :], preferred_element_type=F32)

    def run(tile_fn, shift, *extra):
        tile_fn(0, q_operand(shift, False), True, *extra)
        q_far = q_operand(shift, True)

        def body(i, carry):
            tile_fn(i, q_far, False, *extra)
            return carry
        lax.fori_loop(1, n_wide, body, 0)

    k_max = jnp.sqrt(kn_scr[0:1, 0:1]) * 1.02
    q_norm = [jnp.sqrt(jnp.sum(jnp.square((q[:, h * HEAD:(h + 1) * HEAD] * (HEAD ** -0.5)).astype(BF16).astype(F32)),
                               axis=1, keepdims=True)) for h in range(N_HEADS)]
    bound = [q_norm[h] * k_max + far_ref[1, h] for h in range(N_HEADS)]
    spread = [2.0 * q_norm[h] * k_max + (far_ref[1, h] - far_ref[2, h]) for h in range(N_HEADS)]
    worst = jnp.max(jnp.maximum(jnp.maximum(spread[0], spread[1]), jnp.maximum(spread[2], spread[3])))
    acc_scr[...] = jnp.zeros_like(acc_scr)

    @pl.when(worst <= EXP_RANGE)
    def _():
        run(sum_tile, [-bound[h] for h in range(N_HEADS)], True)

    @pl.when(jnp.logical_not(worst <= EXP_RANGE))
    def _():
        mx_scr[...] = jnp.full(mx_scr.shape, NEG, F32)
        run(max_tile, None)
        neg_max = [-jnp.max(mx_scr[h * n:(h + 1) * n, :], axis=1, keepdims=True) for h in range(N_HEADS)]
        run(sum_tile, neg_max)

    o_ref[0] = jnp.concatenate(
        [acc_scr[h * n:(h + 1) * n, HEAD:LANES] / acc_scr[h * n:(h + 1) * n, LANES:LANES + 1]
         for h in range(N_HEADS)], axis=1).astype(o_ref.dtype)


def _t5_bucket(dist):
    n = jnp.maximum(dist, 0)
    max_exact = REL_BUCKETS // 2
    log_ratio = jnp.log(jnp.maximum(n, 1).astype(F32) / max_exact) / math.log(REL_MAX_DIST / max_exact)
    large = jnp.minimum(max_exact + (log_ratio * (REL_BUCKETS - max_exact)).astype(I32), REL_BUCKETS - 1)
    return jnp.where(n < max_exact, n, large)


def _dsa(u_c, kv_norm_w, w_uk, w_uv, rel_bias):
    b, l, _ = u_c.shape
    top_k = min(TOPK_MAX, l // 4)
    assert DSA_WIDE - Q_BLOCK + 1 >= REL_MAX_DIST and l % DSA_WIDE == 0
    dist = (jnp.arange(Q_BLOCK, dtype=I32)[:, None] + (DSA_WIDE - Q_BLOCK)) - jnp.arange(DSA_WIDE, dtype=I32)[None, :]
    onehot = _t5_bucket(dist)[None] == jnp.arange(REL_BUCKETS, dtype=I32)[:, None, None]
    table = rel_bias.astype(F32)
    bias = jnp.stack([jnp.sum(jnp.where(onehot, table[:, h][:, None, None], 0.0), axis=0) for h in range(N_HEADS)])
    w_kv = jnp.concatenate([w_uk, w_uv], axis=1).astype(F32)
    kern = functools.partial(_dsa_kernel, top_k=top_k)
    lp = l + DSA_WIDE
    return pl.pallas_call(
        kern,
        grid=(b, l // Q_BLOCK),
        in_specs=[pl.BlockSpec((1, Q_BLOCK, C_COLS), lambda i, j: (i, j, 0)),
                  _full((1, LANES)), _full((LANES, LANES)), _full((N_HEADS, Q_BLOCK, DSA_WIDE)),
                  pl.BlockSpec(memory_space=pltpu.SMEM)],
        out_specs=pl.BlockSpec((1, Q_BLOCK, W_GROUP), lambda i, j: (i, j, 0)),
        out_shape=jax.ShapeDtypeStruct((b, l, W_GROUP), BF16),
        scratch_shapes=[pltpu.VMEM((LANES, lp), BF16), pltpu.VMEM((lp, 2 * LANES), BF16), pltpu.VMEM((lp, LANES), BF16),
                        pltpu.VMEM((lp, Q_BLOCK), I32), pltpu.VMEM((32, lp // BIT_ROWS * 8, Q_BLOCK), I32),
                        pltpu.VMEM((Q_BLOCK, lp), F32),
                        pltpu.VMEM((N_HEADS * Q_BLOCK, 2 * LANES), F32), pltpu.VMEM((N_HEADS * Q_BLOCK, LANES), F32),
                        pltpu.VMEM((8, Q_BLOCK), I32), pltpu.VMEM((8, LANES), F32)],
        compiler_params=_params("parallel", "arbitrary"),
        name="dsa",
    )(u_c, kv_norm_w.astype(F32)[None, :], w_kv, bias,
      jnp.stack([table[REL_BUCKETS - 1], jnp.max(table, axis=0), jnp.min(table, axis=0)]))


def _s5_kernel(u_ref, are_ref, aim_ref, ldt_ref, bre_ref, bim_ref, cre_ref, cim_ref, dsk_ref, gw_ref, gb_ref,
               o_ref, pre, pim, bdr, bdi, hre, him, cre_s, cim_s, tre_s, tim_s):
    first = jnp.logical_and(pl.program_id(0) == 0, pl.program_id(1) == 0)
    n = S_BLOCK
    pad = LANES

    @pl.when(first)
    def _():
        lam_re = jnp.minimum(are_ref[...], -1e-4)
        lam_im = aim_ref[...]
        dt = jnp.exp(ldt_ref[...])
        mag = jnp.exp(dt * lam_re)
        ab_re = mag * jnp.cos(dt * lam_im)
        ab_im = mag * jnp.sin(dt * lam_im)
        den = lam_re * lam_re + lam_im * lam_im
        f_re = ((ab_re - 1.0) * lam_re + ab_im * lam_im) / den
        f_im = (ab_im * lam_re - (ab_re - 1.0) * lam_im) / den
        b_re = bre_ref[...]
        b_im = bim_ref[...]
        bb_re = f_re * b_re - f_im * b_im
        bb_im = f_re * b_im + f_im * b_re
        rg = lax.broadcasted_iota(I32, (W_GROUP, S_LANES), 0) // S_GROUP_CH
        cg = lax.broadcasted_iota(I32, (W_GROUP, S_LANES), 1) // S_STATE
        same = rg == cg
        bdr[...] = jnp.where(same, jnp.concatenate([bb_re] * S_GROUPS, axis=0), 0.0)
        bdi[...] = jnp.where(same, jnp.concatenate([bb_im] * S_GROUPS, axis=0), 0.0)
        pre[0:1, :] = ab_re
        pim[0:1, :] = ab_im
        d = 1
        while d < S_WIN:
            sr = pre[d - 1:d, :]
            si = pim[d - 1:d, :]
            xr = pre[0:d, :]
            xi = pim[0:d, :]
            pre[d:2 * d, :] = xr * sr - xi * si
            pim[d:2 * d, :] = xr * si + xi * sr
            d *= 2
        hre[...] = jnp.zeros_like(hre)
        him[...] = jnp.zeros_like(him)

    @pl.when(pl.program_id(1) == 0)
    def _():
        cre_s[...] = jnp.zeros_like(cre_s)
        cim_s[...] = jnp.zeros_like(cim_s)
        tre_s[...] = jnp.zeros_like(tre_s)
        tim_s[...] = jnp.zeros_like(tim_s)

    u = u_ref[0]
    bu_re = _mm(u, bdr[...])
    bu_im = _mm(u, bdi[...])
    lo = pad - S_WIN
    hre[0, lo:pad, :] = tre_s[...]
    him[0, lo:pad, :] = tim_s[...]
    hre[0, pad:pad + n, :] = bu_re
    him[0, pad:pad + n, :] = bu_im
    tre_s[...] = bu_re[n - S_WIN:, :]
    tim_s[...] = bu_im[n - S_WIN:, :]

    def col_block(cb, c_):
        cs = pl.ds(pl.multiple_of(cb * LANES, LANES), LANES)
        src = 0
        d = 1
        while d < S_WIN:
            ar = pre[d - 1:d, cs]
            ai = pim[d - 1:d, cs]
            xr = hre[src, pl.ds(lo - d, n + S_WIN), cs]
            xi = him[src, pl.ds(lo - d, n + S_WIN), cs]
            hre[1 - src, lo:pad + n, cs] = hre[src, lo:pad + n, cs] + ar * xr - ai * xi
            him[1 - src, lo:pad + n, cs] = him[src, lo:pad + n, cs] + ar * xi + ai * xr
            src = 1 - src
            d *= 2
        ar = jnp.broadcast_to(pre[S_WIN - 1:S_WIN, cs], (S_WIN, LANES))
        ai = jnp.broadcast_to(pim[S_WIN - 1:S_WIN, cs], (S_WIN, LANES))
        pr = cre_s[:, cs]
        pi = cim_s[:, cs]
        for v in range(n // S_WIN):
            rows = slice(pad + v * S_WIN, pad + (v + 1) * S_WIN)
            nr = hre[src, rows, cs] + ar * pr - ai * pi
            ni = him[src, rows, cs] + ar * pi + ai * pr
            hre[1 - src, rows, cs] = nr
            him[1 - src, rows, cs] = ni
            pr, pi = nr, ni
        cre_s[:, cs] = pr
        cim_s[:, cs] = pi
        assert src == 1
        return c_

    lax.fori_loop(0, S_LANES // LANES, col_block, 0)

    h_re = hre[0, pad:pad + n, :]
    h_im = him[0, pad:pad + n, :]
    y = _mm(h_re, cre_ref[...]) - _mm(h_im, cim_ref[...]) + dsk_ref[...] * u
    y = jax.nn.gelu(y)
    gl = _mm(y, gw_ref[...]) + gb_ref[...]
    o_ref[0] = (gl[:, 0:W_GROUP] * jax.nn.sigmoid(gl[:, W_GROUP:])).astype(o_ref.dtype)


def _s5(u_s, a_re, a_im, b_re, b_im, c_re, c_im, d_skip, log_dt, glu_w, glu_b):
    b, l, _ = u_s.shape
    f = lambda t: t.astype(F32).reshape(1, -1)
    eye = jnp.eye(S_GROUPS, dtype=F32)
    bt = lambda t: jnp.transpose(t.astype(F32), (2, 0, 1)).reshape(S_GROUP_CH, S_LANES)
    cbd = lambda t: jnp.einsum('gpn,gh->gnhp', t.astype(F32), eye).reshape(S_LANES, W_GROUP)
    ldt = jnp.repeat(log_dt.astype(F32), S_STATE)[None, :]
    return pl.pallas_call(
        _s5_kernel,
        grid=(b, l // S_BLOCK),
        in_specs=[pl.BlockSpec((1, S_BLOCK, S_COLS), lambda i, j: (i, j, 0)),
                  _full((1, S_LANES)), _full((1, S_LANES)), _full((1, S_LANES)),
                  _full((S_GROUP_CH, S_LANES)), _full((S_GROUP_CH, S_LANES)),
                  _full((S_LANES, W_GROUP)), _full((S_LANES, W_GROUP)),
                  _full((1, W_GROUP)), _full((W_GROUP, 2 * W_GROUP)), _full((1, 2 * W_GROUP))],
        out_specs=pl.BlockSpec((1, S_BLOCK, W_GROUP), lambda i, j: (i, j, 0)),
        out_shape=jax.ShapeDtypeStruct((b, l, W_GROUP), BF16),
        scratch_shapes=[pltpu.VMEM((S_WIN, S_LANES), F32), pltpu.VMEM((S_WIN, S_LANES), F32),
                        pltpu.VMEM((W_GROUP, S_LANES), F32), pltpu.VMEM((W_GROUP, S_LANES), F32),
                        pltpu.VMEM((2, S_BLOCK + LANES, S_LANES), F32), pltpu.VMEM((2, S_BLOCK + LANES, S_LANES), F32),
                        pltpu.VMEM((S_WIN, S_LANES), F32), pltpu.VMEM((S_WIN, S_LANES), F32),
                        pltpu.VMEM((S_WIN, S_LANES), F32), pltpu.VMEM((S_WIN, S_LANES), F32)],
        compiler_params=_params("arbitrary", "arbitrary"),
        name="s5",
    )(u_s, f(a_re), f(a_im), ldt, bt(b_re), bt(b_im), cbd(c_re), cbd(c_im), f(d_skip),
      glu_w.astype(F32), f(glu_b))


def _out_proj_kernel(x_ref, ym_ref, yr_ref, yc_ref, ys_ref, w_ref, o_ref):
    acc = x_ref[...]
    for i, y_ref in enumerate((ym_ref, yr_ref, yc_ref, ys_ref)):
        acc = acc + jnp.dot(y_ref[...], w_ref[i * W_GROUP:(i + 1) * W_GROUP, :], preferred_element_type=F32)
    o_ref[...] = acc


def _out_proj(x2, ys, w_out, tm=512):
    t, d = x2.shape
    yspec = pl.BlockSpec((tm, W_GROUP), lambda i: (i, 0))
    return pl.pallas_call(
        _out_proj_kernel,
        grid=(t // tm,),
        in_specs=[pl.BlockSpec((tm, d), lambda i: (i, 0)), yspec, yspec, yspec, yspec, _full(w_out.shape)],
        out_specs=pl.BlockSpec((tm, d), lambda i: (i, 0)),
        out_shape=jax.ShapeDtypeStruct((t, d), F32),
        compiler_params=_params("parallel"),
        name="out_proj",
    )(x2, *ys, w_out)


def _moe_kernel(x_ref, nw_ref, wr_ref, br_ref, wg_ref, wu_ref, wd_ref, fw_ref, o_ref,
                t_scr, cw_scr, rk_scr, rkt_scr, cnt_scr, acc_scr, *, final_norm):
    g = pl.program_id(1)
    tm = x_ref.shape[0]
    cap = MOE_CAP
    lane = lax.broadcasted_iota(I32, (tm, LANES), 1)

    @pl.when(g == 0)
    def _():
        x = x_ref[...]
        t = x * lax.rsqrt(jnp.mean(x * x, axis=-1, keepdims=True) + NORM_EPS) * nw_ref[...]
        t_hi = t.astype(BF16)
        t_scr[...] = t_hi
        t_lo = (t - t_hi.astype(F32)).astype(BF16)
        w_r = wr_ref[...]
        w_hi = w_r.astype(BF16)
        w_lo = (w_r - w_hi.astype(F32)).astype(BF16)
        logits = (jnp.dot(t_hi, w_hi, preferred_element_type=F32) + jnp.dot(t_lo, w_hi, preferred_element_type=F32)
                  + jnp.dot(t_hi, w_lo, preferred_element_type=F32)) + br_ref[...]
        big = jnp.int32(LANES)
        is_g = lane < E_GROUPS
        gl = jnp.where(is_g, logits, -jnp.inf)
        gmax = jnp.max(gl, axis=1, keepdims=True)
        g_sel = jnp.min(jnp.where(jnp.logical_and(is_g, gl == gmax), lane, big), axis=1, keepdims=True)
        g_gate = 1.0 / jnp.sum(jnp.exp(gl - gmax), axis=1, keepdims=True)
        lo = E_GROUPS + g_sel * E_PER_GROUP
        in_g = jnp.logical_and(lane >= lo, lane < lo + E_PER_GROUP)
        el = jnp.where(in_g, logits, -jnp.inf)
        m1 = jnp.max(el, axis=1, keepdims=True)
        i1 = jnp.min(jnp.where(el == m1, lane, big), axis=1, keepdims=True)
        el2 = jnp.where(lane == i1, -jnp.inf, el)
        m2 = jnp.max(el2, axis=1, keepdims=True)
        i2 = jnp.min(jnp.where(el2 == m2, lane, big), axis=1, keepdims=True)
        e2 = jnp.exp(m2 - m1)
        w1 = 1.0 / (1.0 + e2)
        w2 = e2 / (1.0 + e2)
        cw = jnp.where(lane == i1, w1, jnp.where(lane == i2, w2, 0.0)) * g_gate
        cw_hi = cw.astype(BF16)
        cw_scr[...] = jnp.concatenate([cw_hi, (cw - cw_hi.astype(F32)).astype(BF16)], axis=1)
        member = lane == g_sel
        ones = jnp.where(member, 1.0, 0.0)
        row = lax.broadcasted_iota(I32, (LANES, LANES), 0)
        col = lax.broadcasted_iota(I32, (LANES, LANES), 1)
        before = jnp.where(row > col, 1.0, 0.0).astype(BF16)
        running = jnp.zeros((1, LANES), F32)
        ranks = []
        for j in range(tm // LANES):
            blk = ones[j * LANES:(j + 1) * LANES, :]
            ranks.append(jnp.dot(before, blk.astype(BF16), preferred_element_type=F32) + running)
            running = running + jnp.sum(blk, axis=0, keepdims=True)
        rk = jnp.where(member, jnp.concatenate(ranks, axis=0), -1.0)
        rk_scr[...] = rk
        rkt_scr[...] = rk.T[0:8, :]
        cnt_scr[0:1, :] = running
        acc_scr[...] = jnp.zeros_like(acc_scr)

    lane1 = lax.broadcasted_iota(I32, (1, LANES), 1)
    n_tok = jnp.sum(jnp.where(lane1 == g, cnt_scr[0:1, :], 0.0)).astype(I32)
    rk_row = rkt_scr[pl.ds(g, 1), :]
    rk_col = jnp.sum(jnp.where(lane == g, rk_scr[...], 0.0), axis=1, keepdims=True)
    slot_r = lax.broadcasted_iota(I32, (cap, tm), 0).astype(F32)
    slot_c = lax.broadcasted_iota(I32, (tm, cap), 1).astype(F32)
    lane_c = lax.broadcasted_iota(I32, (cap, LANES), 1)

    def one_pass(blk, carry):
        base = (blk * cap).astype(F32)
        gather = jnp.where(rk_row - base == slot_r, 1.0, 0.0).astype(BF16)
        scatter = jnp.where(rk_col - base == slot_c, 1.0, 0.0).astype(BF16)
        xg = jnp.dot(gather, t_scr[...], preferred_element_type=F32).astype(BF16)
        cwg = jnp.dot(gather, cw_scr[...], preferred_element_type=F32)
        cwg = cwg[:, 0:LANES] + cwg[:, LANES:]
        parts = []
        for e in range(E_PER_GROUP):
            hid = _silu(jnp.dot(xg, wg_ref[0, e], preferred_element_type=F32)) * jnp.dot(xg, wu_ref[0, e], preferred_element_type=F32)
            w_e = jnp.sum(jnp.where(lane_c == E_GROUPS + g * E_PER_GROUP + e, cwg, 0.0), axis=1, keepdims=True)
            parts.append((hid * w_e).astype(BF16))
        y = jnp.dot(jnp.concatenate(parts, axis=1), wd_ref[0], preferred_element_type=F32)
        acc_scr[...] += jnp.dot(scatter, y.astype(BF16), preferred_element_type=F32)
        return carry

    lax.fori_loop(0, (n_tok + cap - 1) // cap, one_pass, 0)

    @pl.when(g == pl.num_programs(1) - 1)
    def _():
        y = x_ref[...] + acc_scr[...]
        if final_norm:
            y = y * lax.rsqrt(jnp.mean(y * y, axis=-1, keepdims=True) + NORM_EPS) * fw_ref[...]
        o_ref[...] = y


def _moe(x2, norm_w, wr_g, br_g, wr_e, br_e, w_gate, w_up, w_down, final_w, final_norm, tm=1024):
    t, d = x2.shape
    wr = jnp.zeros((d, LANES), F32).at[:, 0:E_GROUPS].set(wr_g.astype(F32))
    wr = wr.at[:, E_GROUPS:E_GROUPS + N_EXPERTS].set(wr_e.astype(F32))
    br = jnp.zeros((1, LANES), F32).at[0, 0:E_GROUPS].set(br_g.astype(F32))
    br = br.at[0, E_GROUPS:E_GROUPS + N_EXPERTS].set(br_e.astype(F32))
    wide = E_PER_GROUP * D_EXPERT
    wg = w_gate.astype(BF16)
    wu = w_up.astype(BF16)
    wd = w_down.reshape(E_GROUPS, wide, d).astype(BF16)
    kern = functools.partial(_moe_kernel, final_norm=final_norm)
    return pl.pallas_call(
        kern,
        grid=(t // tm, E_GROUPS),
        in_specs=[pl.BlockSpec((tm, d), lambda i, g: (i, 0)), _full((1, d)), _full((d, LANES)), _full((1, LANES)),
                  pl.BlockSpec((1, E_PER_GROUP, d, D_EXPERT), lambda i, g: (g, 0, 0, 0)),
                  pl.BlockSpec((1, E_PER_GROUP, d, D_EXPERT), lambda i, g: (g, 0, 0, 0)),
                  pl.BlockSpec((1, wide, d), lambda i, g: (g, 0, 0)), _full((1, d))],
        out_specs=pl.BlockSpec((tm, d), lambda i, g: (i, 0)),
        out_shape=jax.ShapeDtypeStruct((t, d), F32),
        scratch_shapes=[pltpu.VMEM((tm, d), BF16), pltpu.VMEM((tm, 2 * LANES), BF16), pltpu.VMEM((tm, LANES), F32),
                        pltpu.VMEM((8, tm), F32), pltpu.VMEM((8, LANES), F32), pltpu.VMEM((tm, d), F32)],
        compiler_params=_params("parallel", "arbitrary"),
        name="moe",
    )(x2, norm_w.astype(F32)[None, :], wr, br, wg, wu, wd, final_w.astype(F32)[None, :])


def _pad_in_proj(w):
    d = w.shape[0]
    m_in, r_in = 772, 896
    c_used = 680
    z = lambda k: jnp.zeros((d, k), w.dtype)
    wm = w[:, 0:m_in]
    wr = w[:, m_in:m_in + r_in]
    wc = w[:, m_in + r_in:m_in + r_in + c_used]
    ws = w[:, m_in + r_in + c_used:]
    return jnp.concatenate([wm, z(M_COLS - m_in), wr, wc, z(C_COLS - c_used), ws], axis=1).astype(BF16)


def kernel(x, norm1_w, w_in, m_conv_w, m_conv_b, m_dt_bias, m_a_log, m_d, m_norm_w, r_mu, r_w0, r_w2, r_a0, r_a2, r_g2, r_k_k, r_k_a, r_r_k, r_ln_w, r_ln_b, c_kv_norm_w, c_w_uk, c_w_uv, rel_bias, s_a_re, s_a_im, s_b_re, s_b_im, s_c_re, s_c_im, s_d, s_log_dt, s_glu_w, s_glu_b, w_out, norm2_w, moe_wr_group, moe_br_group, moe_wr_exp, moe_br_exp, moe_w_gate, moe_w_up, moe_w_down, final_norm_w):
    bsz, seq, d = x.shape
    depth = w_in.shape[0]
    x2 = x.astype(F32).reshape(bsz * seq, d)
    for i in range(depth):
        u_m, u_r, u_c, u_s = _in_proj(x2, norm1_w[i].astype(F32)[None, :], _pad_in_proj(w_in[i]))
        sh = lambda t: t.reshape(bsz, seq, t.shape[-1])
        y_m = _mamba(sh(u_m), m_conv_w[i], m_conv_b[i], m_dt_bias[i], m_a_log[i], m_d[i], m_norm_w[i])
        y_r = _rwkv(sh(u_r), r_mu[i], r_w0[i], r_w2[i], r_a0[i], r_a2[i], r_g2[i],
                    r_k_k[i], r_k_a[i], r_r_k[i], r_ln_w[i], r_ln_b[i])
        y_c = _dsa(sh(u_c), c_kv_norm_w[i], c_w_uk[i], c_w_uv[i], rel_bias)
        y_s = _s5(sh(u_s), s_a_re[i], s_a_im[i], s_b_re[i], s_b_im[i], s_c_re[i], s_c_im[i],
                  s_d[i], s_log_dt[i], s_glu_w[i], s_glu_b[i])
        ys = [t.reshape(bsz * seq, W_GROUP) for t in (y_m, y_r, y_c, y_s)]
        x2 = _out_proj(x2, ys, w_out[i].astype(BF16))
        x2 = _moe(x2, norm2_w[i], moe_wr_group[i], moe_br_group[i], moe_wr_exp[i], moe_br_exp[i],
                  moe_w_gate[i], moe_w_up[i], moe_w_down[i], final_norm_w, final_norm=(i == depth - 1))
    return x2.reshape(bsz, seq, d).astype(x.dtype)
```
